```python
import math
import jax, jax.numpy as jnp
from jax import lax
import numpy as np

D_MODEL = 1024
BATCH = 32
SEQ = 2048
DEPTH = 2

N_A_LAYERS = DEPTH // 2
N_B_LAYERS = DEPTH - N_A_LAYERS

DEEPNORM_ALPHA = (2 * DEPTH) ** 0.25
DEEPNORM_BETA = (8 * DEPTH) ** -0.25
LN_EPS = 1e-5
RMS_EPS = 1e-6

GDN_HEADS = 8
GDN_DK = 128
GDN_DV = 128
GDN_CONV = 4
GDN_CHUNK = 64
GDN_QK = GDN_HEADS * GDN_DK
GDN_V = GDN_HEADS * GDN_DV
GDN_IN = 2 * GDN_QK + 2 * GDN_V + 2 * GDN_HEADS

MLA_HEADS = 16
MLA_NOPE = 64
MLA_ROPE = 32
MLA_V = 64
MLA_KV_RANK = 256
MLA_Q_RANK = 384
MLA_QBLOCK = 128
ROPE_THETA = 10000.0

PEER_HEADS = 8
PEER_NKEYS = 128
PEER_N = PEER_NKEYS * PEER_NKEYS
PEER_DQ = 256
PEER_DHALF = PEER_DQ // 2
PEER_TOPK = 16
PEER_TOKEN_BLOCK = 512

kernel_name = 'yoco_gdn_mla_peer_deepnorm'


def layer_norm(x, g, b):
    xf = x.astype(jnp.float32)
    mu = xf.mean(-1, keepdims=True)
    var = jnp.square(xf - mu).mean(-1, keepdims=True)
    return ((xf - mu) * lax.rsqrt(var + LN_EPS) * g.astype(jnp.float32) + b.astype(jnp.float32)).astype(x.dtype)


def rms_norm(x, g):
    xf = x.astype(jnp.float32)
    return (xf * lax.rsqrt(jnp.mean(xf * xf, -1, keepdims=True) + RMS_EPS) * g.astype(jnp.float32)).astype(x.dtype)


def l2_normalize(x):
    xf = x.astype(jnp.float32)
    return xf * lax.rsqrt(jnp.sum(xf * xf, -1, keepdims=True) + RMS_EPS)


def rope(x, positions):
    half = MLA_ROPE // 2
    inv_freq = ROPE_THETA ** (-jnp.arange(half, dtype=jnp.float32) / half)
    ang = positions.astype(jnp.float32)[..., None] * inv_freq
    ang = ang.reshape(ang.shape[:2] + (1,) * (x.ndim - 3) + (half,))
    cos, sin = jnp.cos(ang), jnp.sin(ang)
    xf = x.astype(jnp.float32)
    x1, x2 = xf[..., :half], xf[..., half:]
    return jnp.concatenate([x1 * cos - x2 * sin, x2 * cos + x1 * sin], -1).astype(x.dtype)


def causal_depthwise_conv(x, w):
    c = x.shape[-1]
    return lax.conv_general_dilated(x, w[:, None, :].astype(x.dtype), window_strides=(1,),
                                    padding=[(GDN_CONV - 1, 0)],
                                    dimension_numbers=('NWC', 'WIO', 'NWC'),
                                    feature_group_count=c)


def chunk_gated_delta_rule(q, k, v, g, beta):
    b, s, h, dk = q.shape
    dv = v.shape[-1]
    c = GDN_CHUNK
    n = s // c

    def to_chunks(t):
        t = t.reshape((b, n, c, h) + t.shape[3:])
        return jnp.moveaxis(t, (1, 3), (0, 2))

    q = to_chunks(q * dk ** -0.5)
    k = to_chunks(k)
    v = to_chunks(v)
    beta = to_chunks(beta)
    g = jnp.cumsum(to_chunks(g), axis=-1)
    tril = jnp.tril(jnp.ones((c, c), bool))
    strict = jnp.tril(jnp.ones((c, c), bool), -1)
    gdiff = jnp.where(tril, g[..., :, None] - g[..., None, :], 0.0)
    decay = jnp.where(tril, jnp.exp(gdiff), 0.0)
    k_beta = k * beta[..., None]
    lower = jnp.where(strict, jnp.einsum('nbhcd,nbhed->nbhce', k_beta, k) * decay, 0.0)
    rhs = jnp.concatenate([v * beta[..., None], k_beta * jnp.exp(g)[..., None]], -1)
    sol = lax.linalg.triangular_solve(lower, rhs, left_side=True, lower=True, unit_diagonal=True)
    u, w = sol[..., :dv], sol[..., dv:]

    def step(state, inp):
        q_i, k_i, u_i, w_i, g_i, decay_i = inp
        v_new = u_i - jnp.einsum('bhcd,bhde->bhce', w_i, state)
        attn = jnp.einsum('bhcd,bhed->bhce', q_i, k_i) * decay_i
        o = (jnp.einsum('bhcd,bhde->bhce', q_i * jnp.exp(g_i)[..., None], state)
             + jnp.einsum('bhce,bhef->bhcf', attn, v_new))
        g_last = g_i[..., -1:]
        state = (state * jnp.exp(g_last)[..., None]
                 + jnp.einsum('bhcd,bhce->bhde', k_i * jnp.exp(g_last - g_i)[..., None], v_new))
        return state, o

    state0 = jnp.zeros((b, h, dk, dv), jnp.float32)
    _, o = lax.scan(step, state0, (q, k, u, w, g, decay))
    return jnp.moveaxis(o, (0, 2), (1, 3)).reshape(b, s, h, dv)


def gated_deltanet(h, w_in, conv_w, a_log, dt_bias, norm_g, w_out):
    b, s, _ = h.shape
    proj = h @ w_in
    qkv, z, b_logit, a_logit = jnp.split(
        proj, [2 * GDN_QK + GDN_V, 2 * GDN_QK + 2 * GDN_V, 2 * GDN_QK + 2 * GDN_V + GDN_HEADS], -1)
    qkv = jax.nn.silu(causal_depthwise_conv(qkv, conv_w))
    q, k, v = jnp.split(qkv, [GDN_QK, 2 * GDN_QK], -1)
    q = l2_normalize(q.reshape(b, s, GDN_HEADS, GDN_DK))
    k = l2_normalize(k.reshape(b, s, GDN_HEADS, GDN_DK))
    v = v.reshape(b, s, GDN_HEADS, GDN_DV).astype(jnp.float32)
    beta = jax.nn.sigmoid(b_logit.astype(jnp.float32))
    g = -jnp.exp(a_log.astype(jnp.float32)) * jax.nn.softplus(a_logit.astype(jnp.float32) + dt_bias.astype(jnp.float32))
    o = chunk_gated_delta_rule(q, k, v, g, beta)
    o = rms_norm(o, norm_g) * jax.nn.silu(z.reshape(b, s, GDN_HEADS, GDN_DV).astype(jnp.float32))
    return o.reshape(b, s, GDN_V).astype(h.dtype) @ w_out


def mla_shared_kv(h, positions, w_kv_a, kv_norm_g, w_uk, w_uv):
    b, s, _ = h.shape
    kv = h @ w_kv_a
    c_kv = rms_norm(kv[..., :MLA_KV_RANK], kv_norm_g)
    k_rope = rope(kv[..., MLA_KV_RANK:], positions)
    k_nope = (c_kv @ w_uk).reshape(b, s, MLA_HEADS, MLA_NOPE)
    v = (c_kv @ w_uv).reshape(b, s, MLA_HEADS, MLA_V)
    return k_nope, k_rope, v


def mla_attention(h, positions, k_nope, k_rope, v, w_qa, q_norm_g, w_qb, w_out):
    b, s, _ = h.shape
    q = (rms_norm(h @ w_qa, q_norm_g) @ w_qb).reshape(b, s, MLA_HEADS, MLA_NOPE + MLA_ROPE)
    q_nope = q[..., :MLA_NOPE]
    q_rope = rope(q[..., MLA_NOPE:], positions)
    nb = s // MLA_QBLOCK
    scale = (MLA_NOPE + MLA_ROPE) ** -0.5
    key_pos = jnp.arange(s)

    def blocks(t):
        return jnp.moveaxis(t.reshape((b, nb, MLA_QBLOCK) + t.shape[2:]), 1, 0)

    def attend(inp):
        qn, qr, blk = inp
        sc = (jnp.einsum('bqhd,bkhd->bhqk', qn, k_nope, preferred_element_type=jnp.float32)
              + jnp.einsum('bqhr,bkr->bhqk', qr, k_rope, preferred_element_type=jnp.float32))
        qpos = blk * MLA_QBLOCK + jnp.arange(MLA_QBLOCK)
        sc = jnp.where(key_pos[None, :] <= qpos[:, None], sc * scale, -jnp.inf)
        p = jax.nn.softmax(sc, -1).astype(v.dtype)
        return jnp.einsum('bhqk,bkhd->bqhd', p, v)

    o = lax.map(attend, (blocks(q_nope), blocks(q_rope), jnp.arange(nb)))
    o = jnp.moveaxis(o, 0, 1).reshape(b, s, MLA_HEADS * MLA_V)
    return o @ w_out


def peer(h, w_q, sub_keys, u_tab, v_tab):
    b, s, d = h.shape
    t = b * s
    tb = math.gcd(t, PEER_TOKEN_BLOCK)
    xs = h.reshape(t // tb, tb, d)

    def block(xb):
        q = (xb @ w_q).reshape(tb, PEER_HEADS, 2, PEER_DHALF)
        sc = jnp.einsum('thpd,hpnd->thpn', q, sub_keys, preferred_element_type=jnp.float32)
        top_s, top_i = lax.top_k(sc, PEER_TOPK)
        cand_s = top_s[:, :, 0, :, None] + top_s[:, :, 1, None, :]
        cand_id = top_i[:, :, 0, :, None] * PEER_NKEYS + top_i[:, :, 1, None, :]
        best_s, best_j = lax.top_k(cand_s.reshape(tb, PEER_HEADS, PEER_TOPK * PEER_TOPK), PEER_TOPK)
        ids = jnp.take_along_axis(cand_id.reshape(tb, PEER_HEADS, PEER_TOPK * PEER_TOPK), best_j, -1)
        gate = jax.nn.softmax(best_s, -1).reshape(tb, PEER_HEADS * PEER_TOPK)
        ids = ids.reshape(tb, PEER_HEADS * PEER_TOPK)
        hid = jnp.einsum('td,ted->te', xb, u_tab[ids], preferred_element_type=jnp.float32)
        act = (gate * jax.nn.gelu(hid, approximate=False)).astype(xb.dtype)
        return jnp.einsum('te,ted->td', act, v_tab[ids])

    return lax.map(block, xs).reshape(b, s, d)


def setup_inputs(seed: int = 0) -> dict:
    key = jax.random.key(seed)
    ks = jax.random.split(key, 32)
    f32 = jnp.float32

    def dense(k, shape, fan_in, scale=1.0):
        return jax.random.normal(k, shape, f32) * (scale * fan_in ** -0.5)

    def gain(k, shape):
        return 1.0 + 0.02 * jax.random.normal(k, shape, f32)

    x = jax.random.normal(ks[0], (BATCH, SEQ, D_MODEL), f32)
    offset = jax.random.randint(ks[1], (BATCH, 1), 0, 4096, dtype=jnp.int32)
    positions = offset + jnp.arange(SEQ, dtype=jnp.int32)[None, :]

    a_w_in = jnp.concatenate([
        dense(ks[2], (N_A_LAYERS, D_MODEL, 2 * GDN_QK), D_MODEL),
        dense(ks[3], (N_A_LAYERS, D_MODEL, GDN_V), D_MODEL, DEEPNORM_BETA),
        dense(ks[4], (N_A_LAYERS, D_MODEL, GDN_V + 2 * GDN_HEADS), D_MODEL),
    ], -1)
    a_conv_w = dense(ks[5], (N_A_LAYERS, GDN_CONV, 2 * GDN_QK + GDN_V), GDN_CONV)
    a_a_log = jnp.log(jax.random.uniform(ks[6], (N_A_LAYERS, GDN_HEADS), f32, 1.0, 16.0))
    dt = jnp.exp(jax.random.uniform(ks[7], (N_A_LAYERS, GDN_HEADS), f32, math.log(1e-3), math.log(1e-1)))
    a_dt_bias = jnp.log(jnp.expm1(dt))
    a_norm_g = gain(ks[8], (N_A_LAYERS, GDN_DV))
    a_w_out = dense(ks[9], (N_A_LAYERS, GDN_V, D_MODEL), GDN_V, DEEPNORM_BETA)

    kv_w_a = dense(ks[10], (D_MODEL, MLA_KV_RANK + MLA_ROPE), D_MODEL)
    kv_norm_g = gain(ks[11], (MLA_KV_RANK,))
    kv_w_uk = dense(ks[12], (MLA_KV_RANK, MLA_HEADS * MLA_NOPE), MLA_KV_RANK)
    kv_w_uv = dense(ks[13], (MLA_KV_RANK, MLA_HEADS * MLA_V), MLA_KV_RANK, DEEPNORM_BETA)

    b_w_qa = dense(ks[14], (N_B_LAYERS, D_MODEL, MLA_Q_RANK), D_MODEL)
    b_q_norm_g = gain(ks[15], (N_B_LAYERS, MLA_Q_RANK))
    b_w_qb = dense(ks[16], (N_B_LAYERS, MLA_Q_RANK, MLA_HEADS * (MLA_NOPE + MLA_ROPE)), MLA_Q_RANK)
    b_w_out = dense(ks[17], (N_B_LAYERS, MLA_HEADS * MLA_V, D_MODEL), MLA_HEADS * MLA_V, DEEPNORM_BETA)

    peer_w_q = dense(ks[18], (DEPTH, D_MODEL, PEER_HEADS * PEER_DQ), D_MODEL)
    peer_sub_keys = dense(ks[19], (DEPTH, PEER_HEADS, 2, PEER_NKEYS, PEER_DHALF), PEER_DHALF)
    peer_u = dense(ks[20], (DEPTH, PEER_N, D_MODEL), D_MODEL)
    peer_v = jax.random.normal(ks[21], (DEPTH, PEER_N, D_MODEL), f32) * DEEPNORM_BETA

    ln_g = gain(ks[22], (DEPTH, 2, D_MODEL))
    ln_b = 0.02 * jax.random.normal(ks[23], (DEPTH, 2, D_MODEL), f32)

    return {'x': x, 'positions': positions,
            'a_w_in': a_w_in, 'a_conv_w': a_conv_w, 'a_a_log': a_a_log, 'a_dt_bias': a_dt_bias,
            'a_norm_g': a_norm_g, 'a_w_out': a_w_out,
            'kv_w_a': kv_w_a, 'kv_norm_g': kv_norm_g, 'kv_w_uk': kv_w_uk, 'kv_w_uv': kv_w_uv,
            'b_w_qa': b_w_qa, 'b_q_norm_g': b_q_norm_g, 'b_w_qb': b_w_qb, 'b_w_out': b_w_out,
            'peer_w_q': peer_w_q, 'peer_sub_keys': peer_sub_keys, 'peer_u': peer_u, 'peer_v': peer_v,
            'ln_g': ln_g, 'ln_b': ln_b}


def reference(x, positions, a_w_in, a_conv_w, a_a_log, a_dt_bias, a_norm_g, a_w_out,
              kv_w_a, kv_norm_g, kv_w_uk, kv_w_uv, b_w_qa, b_q_norm_g, b_w_qb, b_w_out,
              peer_w_q, peer_sub_keys, peer_u, peer_v, ln_g, ln_b):
    h = x
    k_nope = k_rope = v_shared = None
    for layer in range(DEPTH):
        if layer < N_A_LAYERS:
            i = layer
            mix = gated_deltanet(h, a_w_in[i], a_conv_w[i], a_a_log[i], a_dt_bias[i], a_norm_g[i], a_w_out[i])
        else:
            if layer == N_A_LAYERS:
                k_nope, k_rope, v_shared = mla_shared_kv(h, positions, kv_w_a, kv_norm_g, kv_w_uk, kv_w_uv)
            j = layer - N_A_LAYERS
            mix = mla_attention(h, positions, k_nope, k_rope, v_shared, b_w_qa[j], b_q_norm_g[j], b_w_qb[j], b_w_out[j])
        h = layer_norm(DEEPNORM_ALPHA * h + mix, ln_g[layer, 0], ln_b[layer, 0])
        h = layer_norm(DEEPNORM_ALPHA * h + peer(h, peer_w_q[layer], peer_sub_keys[layer], peer_u[layer], peer_v[layer]),
                       ln_g[layer, 1], ln_b[layer, 1])
    return h
```

```python
import functools
import math

import jax
import jax.numpy as jnp
from jax import lax
from jax.experimental import pallas as pl
from jax.experimental.pallas import tpu as pltpu

F32 = jnp.float32
BF16 = jnp.bfloat16
HIGHEST = lax.Precision.HIGHEST

LN_EPS = 1e-5
RMS_EPS = 1e-6
ROPE_THETA = 10000.0
GDN_CONV = 4
PEER_TOPK = 16
LANES = 128
SUBLANES = 8
GDN_KERNEL_CHUNK = 128
VMEM_LIMIT_BYTES = 56 * 1024 * 1024


def _cparams(*sem):
    return pltpu.CompilerParams(dimension_semantics=sem, vmem_limit_bytes=VMEM_LIMIT_BYTES)


def _dot(a, b):
    return jnp.dot(a, b, preferred_element_type=F32)


def _dot_nt(a, b):
    return lax.dot_general(a, b, (((1,), (1,)), ((), ())), preferred_element_type=F32)


def _dot_tn(a, b):
    return lax.dot_general(a, b, (((0,), (0,)), ((), ())), preferred_element_type=F32)


def _dot_hi(a, b):
    return jnp.dot(a, b, precision=HIGHEST, preferred_element_type=F32)


def _sigmoid(x):
    return 1.0 / (1.0 + jnp.exp(-x))


def _softplus(x):
    return jnp.maximum(x, 0.0) + jnp.log1p(jnp.exp(-jnp.abs(x)))


def _layer_norm(y, g, b):
    mu = jnp.mean(y, -1, keepdims=True)
    d = y - mu
    var = jnp.mean(d * d, -1, keepdims=True)
    return d * lax.rsqrt(var + LN_EPS) * g + b


def _rms(x, g):
    return x * lax.rsqrt(jnp.mean(x * x, -1, keepdims=True) + RMS_EPS) * g


def _proj_ln_kernel(a_ref, w_ref, res_ref, g_ref, b_ref, o_ref, *, alpha):
    y = alpha * res_ref[...] + _dot(a_ref[...], w_ref[...])
    o_ref[...] = _layer_norm(y, g_ref[...], b_ref[...])


def _proj_ln(a, w, res, g, b, alpha, tm):
    t, k = a.shape
    d = w.shape[1]
    return pl.pallas_call(
        functools.partial(_proj_ln_kernel, alpha=alpha),
        grid=(t // tm,),
        in_specs=[pl.BlockSpec((tm, k), lambda i: (i, 0)),
                  pl.BlockSpec((k, d), lambda i: (0, 0)),
                  pl.BlockSpec((tm, d), lambda i: (i, 0)),
                  pl.BlockSpec((1, d), lambda i: (0, 0)),
                  pl.BlockSpec((1, d), lambda i: (0, 0))],
        out_specs=pl.BlockSpec((tm, d), lambda i: (i, 0)),
        out_shape=jax.ShapeDtypeStruct((t, d), F32),
        compiler_params=_cparams("parallel"),
        name="proj_ln",
    )(a, w, res, g, b)


def _gdn_proj_kernel(x_ref, wqkv_ref, wz_ref, wbg_ref, convw_ref, alog_ref, dtb_ref, tri_ref,
                     q_ref, k_ref, v_ref, z_ref, bg_ref, buf_ref, *, ts, nqk, nv, heads, chunk):
    s = pl.program_id(1)
    halo = SUBLANES
    xb = x_ref[...].astype(BF16)

    @pl.when(s == 0)
    def _():
        buf_ref[0:halo, :] = jnp.zeros((halo, buf_ref.shape[1]), F32)

    buf_ref[halo:halo + ts, :] = _dot(xb, wqkv_ref[...])
    z_ref[...] = _dot(xb, wz_ref[...]).astype(BF16)

    dk = nqk // heads
    for grp in range((2 * nqk + nv) // LANES):
        c0 = grp * LANES
        acc = None
        for j in range(GDN_CONV):
            r0 = halo - (GDN_CONV - 1) + j
            term = buf_ref[r0:r0 + ts, c0:c0 + LANES] * convw_ref[j:j + 1, c0:c0 + LANES]
            acc = term if acc is None else acc + term
        y = acc * _sigmoid(acc)
        if c0 < 2 * nqk:
            y = y * lax.rsqrt(jnp.sum(y * y, -1, keepdims=True) + RMS_EPS)
        if c0 < nqk:
            q_ref[:, c0:c0 + LANES] = (y * dk ** -0.5).astype(BF16)
        elif c0 < 2 * nqk:
            k_ref[:, c0 - nqk:c0 - nqk + LANES] = y.astype(BF16)
        else:
            v_ref[:, c0 - 2 * nqk:c0 - 2 * nqk + LANES] = y.astype(BF16)

    buf_ref[0:halo, :] = buf_ref[ts:ts + halo, :]

    bgp = _dot(xb, wbg_ref[...])
    lane = lax.broadcasted_iota(jnp.int32, (chunk, LANES), 1)
    for c in range(ts // chunk):
        blk = bgp[c * chunk:(c + 1) * chunk, :]
        beta = _sigmoid(blk)
        g = -jnp.exp(alog_ref[...]) * _softplus(blk + dtb_ref[...])
        gc = _dot_hi(tri_ref[...], g)
        bg_ref[c * chunk:(c + 1) * chunk, :] = jnp.where(lane < heads, beta, gc)


def _gdn_proj(x2d, wqkv, wz, wbg, convw, alog, dtb, batch, seq, heads, nqk, nv, ts, chunk):
    t, d = x2d.shape
    ns = seq // ts
    cw = 2 * nqk + nv
    tri = jnp.tril(jnp.ones((chunk, chunk), F32))
    row = lambda b, s: (b * ns + s, 0)
    fixed = lambda b, s: (0, 0)
    return pl.pallas_call(
        functools.partial(_gdn_proj_kernel, ts=ts, nqk=nqk, nv=nv, heads=heads, chunk=chunk),
        grid=(batch, ns),
        in_specs=[pl.BlockSpec((ts, d), row),
                  pl.BlockSpec((d, cw), fixed),
                  pl.BlockSpec((d, nv), fixed),
                  pl.BlockSpec((d, LANES), fixed),
                  pl.BlockSpec((GDN_CONV, cw), fixed),
                  pl.BlockSpec((1, LANES), fixed),
                  pl.BlockSpec((1, LANES), fixed),
                  pl.BlockSpec((chunk, chunk), fixed)],
        out_specs=[pl.BlockSpec((ts, nqk), row),
                   pl.BlockSpec((ts, nqk), row),
                   pl.BlockSpec((ts, nv), row),
                   pl.BlockSpec((ts, nv), row),
                   pl.BlockSpec((ts, LANES), row)],
        out_shape=[jax.ShapeDtypeStruct((t, nqk), BF16),
                   jax.ShapeDtypeStruct((t, nqk), BF16),
                   jax.ShapeDtypeStruct((t, nv), BF16),
                   jax.ShapeDtypeStruct((t, nv), BF16),
                   jax.ShapeDtypeStruct((t, LANES), F32)],
        scratch_shapes=[pltpu.VMEM((ts + 2 * SUBLANES, cw), F32)],
        compiler_params=_cparams("parallel", "arbitrary"),
        name="gdn_proj",
    )(x2d, wqkv, wz, wbg, convw, alog, dtb, tri)


def _unit_lower_inverse(low, eye):
    n = low.shape[0]
    inv = eye - low
    power = low
    k = 1
    while 2 * k < n:
        power = _dot_hi(power, power)
        inv = inv + _dot_hi(inv, power)
        k *= 2
    return inv


def _gdn_chunk_kernel(q_ref, k_ref, v_ref, z_ref, bg_ref, ng_ref, o_ref, state_ref, *, tc, heads, chunk):
    @pl.when(pl.program_id(1) == 0)
    def _():
        state_ref[...] = jnp.zeros(state_ref.shape, F32)

    rows = lax.broadcasted_iota(jnp.int32, (chunk, chunk), 0)
    cols = lax.broadcasted_iota(jnp.int32, (chunk, chunk), 1)
    tril = rows >= cols
    strict = rows > cols
    eye = jnp.where(rows == cols, 1.0, 0.0).astype(F32)

    def body(c, carry):
        r0 = pl.multiple_of(c * chunk, chunk)
        bg = bg_ref[pl.ds(r0, chunk), :]
        for h in range(heads):
            hs = slice(h * LANES, (h + 1) * LANES)
            beta = bg[:, h:h + 1]
            gcol = bg[:, heads + h:heads + h + 1]
            gmat = jnp.broadcast_to(gcol, (chunk, chunk))
            diff = jnp.where(tril, gmat - gmat.T, 0.0)
            decay = jnp.where(tril, jnp.exp(diff), 0.0)
            eg = jnp.exp(gcol)
            glast = gcol[chunk - 1:chunk, :]
            qb = q_ref[pl.ds(r0, chunk), hs]
            kb = k_ref[pl.ds(r0, chunk), hs]
            kf = kb.astype(F32)
            vf = v_ref[pl.ds(r0, chunk), hs].astype(F32)
            kbeta = kf * beta
            low = jnp.where(strict, _dot_nt(kbeta.astype(BF16), kb) * decay, 0.0)
            inv = _unit_lower_inverse(low, eye).astype(BF16)
            u = _dot(inv, (vf * beta).astype(BF16))
            w = _dot(inv, (kbeta * eg).astype(BF16))
            st = state_ref[h]
            stb = st.astype(BF16)
            v_new = u - _dot(w.astype(BF16), stb)
            vnb = v_new.astype(BF16)
            attn = _dot_nt(qb, kb) * decay
            o = _dot((qb.astype(F32) * eg).astype(BF16), stb) + _dot(attn.astype(BF16), vnb)
            kdec = (kf * jnp.exp(glast - gcol)).astype(BF16)
            state_ref[h] = st * jnp.exp(glast) + _dot_tn(kdec, vnb)
            zf = z_ref[pl.ds(r0, chunk), hs].astype(F32)
            on = _rms(o, ng_ref[...]) * (zf * _sigmoid(zf))
            o_ref[pl.ds(r0, chunk), hs] = on.astype(BF16)
        return carry

    lax.fori_loop(0, tc // chunk, body, 0)


def _gdn_chunk(q, k, v, z, bg, norm_g, batch, seq, heads, tc, chunk):
    t, nqk = q.shape
    nv = v.shape[1]
    ns = seq // tc
    row = lambda b, s: (b * ns + s, 0)
    return pl.pallas_call(
        functools.partial(_gdn_chunk_kernel, tc=tc, heads=heads, chunk=chunk),
        grid=(batch, ns),
        in_specs=[pl.BlockSpec((tc, nqk), row),
                  pl.BlockSpec((tc, nqk), row),
                  pl.BlockSpec((tc, nv), row),
                  pl.BlockSpec((tc, nv), row),
                  pl.BlockSpec((tc, LANES), row),
                  pl.BlockSpec((1, LANES), lambda b, s: (0, 0))],
        out_specs=pl.BlockSpec((tc, nv), row),
        out_shape=jax.ShapeDtypeStruct((t, nv), BF16),
        scratch_shapes=[pltpu.VMEM((heads, nqk // heads, nv // heads), F32)],
        compiler_params=_cparams("parallel", "arbitrary"),
        name="gdn_chunk",
    )(q, k, v, z, bg, norm_g)


def _mla_prep_kernel(x_ref, pos_ref, invf_ref, wkva_ref, gkv_ref, wuk_ref, wuv_ref, wqa_ref, gq_ref,
                     wqb_ref, wqbs_ref, q_ref, k_ref, v_ref, *, heads, rank, nope, rope, scale):
    xb = x_ref[...].astype(BF16)
    ts = xb.shape[0]
    lane = lax.broadcasted_iota(jnp.int32, (ts, LANES), 1)
    ang = pos_ref[...] * invf_ref[...]
    is_rope = (lane >= nope) & (lane < nope + rope)
    cosm = jnp.where(lane < nope, 1.0, jnp.where(is_rope, jnp.cos(ang), 0.0))
    sinm = jnp.where(is_rope, jnp.sin(ang), 0.0)

    kv = _dot(xb, wkva_ref[...])
    ckv = _rms(kv[:, :rank], gkv_ref[...]).astype(BF16)
    krope = kv[:, rank:rank + LANES] * cosm + kv[:, rank + LANES:rank + 2 * LANES] * sinm
    v_ref[...] = _dot(ckv, wuv_ref[...]).astype(BF16)
    knope = _dot(ckv, wuk_ref[...])

    qn = _rms(_dot(xb, wqa_ref[...]), gq_ref[...]).astype(BF16)
    qa = _dot(qn, wqb_ref[...])
    qs = _dot(qn, wqbs_ref[...])
    for h in range(heads):
        hs = slice(h * LANES, (h + 1) * LANES)
        k_ref[:, hs] = (knope[:, hs] + krope).astype(BF16)
        q_ref[:, hs] = ((qa[:, hs] * cosm + qs[:, hs] * sinm) * scale).astype(BF16)


def _mla_prep(x2d, posf, invf, wkva, gkv, wuk, wuv, wqa, gq, wqb, wqbs, heads, rank, nope, rope, vdim, ts):
    t, d = x2d.shape
    qrank = wqa.shape[1]
    scale = (nope + rope) ** -0.5
    row = lambda i: (i, 0)
    fixed = lambda i: (0, 0)
    full = lambda a: pl.BlockSpec(a.shape, fixed)
    return pl.pallas_call(
        functools.partial(_mla_prep_kernel, heads=heads, rank=rank, nope=nope, rope=rope, scale=scale),
        grid=(t // ts,),
        in_specs=[pl.BlockSpec((ts, d), row), pl.BlockSpec((ts, 1), row), full(invf), full(wkva), full(gkv),
                  full(wuk), full(wuv), full(wqa), full(gq), full(wqb), full(wqbs)],
        out_specs=[pl.BlockSpec((ts, heads * LANES), row),
                   pl.BlockSpec((ts, heads * LANES), row),
                   pl.BlockSpec((ts, heads * vdim), row)],
        out_shape=[jax.ShapeDtypeStruct((t, heads * LANES), BF16),
                   jax.ShapeDtypeStruct((t, heads * LANES), BF16),
                   jax.ShapeDtypeStruct((t, heads * vdim), BF16)],
        compiler_params=_cparams("parallel"),
        name="mla_prep",
    )(x2d, posf, invf, wkva, gkv, wuk, wuv, wqa, gq, wqb, wqbs)


def _flash_kernel(q_ref, k_ref, v_ref, o_ref, *, tq, tk, vdim):
    qi = pl.program_id(2)
    hpp = LANES // vdim
    qpos = qi * tq + lax.broadcasted_iota(jnp.int32, (tq, tk), 0)
    kloc = lax.broadcasted_iota(jnp.int32, (tq, tk), 1)
    lane = lax.broadcasted_iota(jnp.int32, (tq, LANES), 1)
    nkv = (qi * tq + tq + tk - 1) // tk

    out = jnp.zeros((tq, LANES), F32)
    for hh in range(hpp):
        hs = slice(hh * LANES, (hh + 1) * LANES)
        qh = q_ref[:, hs]

        def body(c, carry, hs=hs, qh=qh):
            m, l, acc = carry
            k0 = pl.multiple_of(c * tk, tk)
            s = _dot_nt(qh, k_ref[pl.ds(k0, tk), hs])
            s = jnp.where(k0 + kloc <= qpos, s, -jnp.inf)
            m_new = jnp.maximum(m, jnp.max(s, -1, keepdims=True))
            p = jnp.exp(s - m_new)
            corr = jnp.exp(m - m_new)
            l = corr * l + jnp.sum(p, -1, keepdims=True)
            acc = corr * acc + _dot(p.astype(BF16), v_ref[pl.ds(k0, tk), :])
            return m_new, l, acc

        init = (jnp.full((tq, 1), -jnp.inf, F32), jnp.zeros((tq, 1), F32), jnp.zeros((tq, LANES), F32))
        m, l, acc = lax.fori_loop(0, nkv, body, init)
        out = jnp.where((lane >= hh * vdim) & (lane < (hh + 1) * vdim), acc / l, out)
    o_ref[...] = out.astype(BF16)


def _flash(q, k, v, batch, seq, heads, vdim, tq, tk):
    t = q.shape[0]
    hpp = LANES // vdim
    nq = seq // tq
    return pl.pallas_call(
        functools.partial(_flash_kernel, tq=tq, tk=tk, vdim=vdim),
        grid=(batch, heads // hpp, nq),
        in_specs=[pl.BlockSpec((tq, hpp * LANES), lambda b, h, i: (b * nq + i, h)),
                  pl.BlockSpec((seq, hpp * LANES), lambda b, h, i: (b, h)),
                  pl.BlockSpec((seq, LANES), lambda b, h, i: (b, h))],
        out_specs=pl.BlockSpec((tq, LANES), lambda b, h, i: (b * nq + i, h)),
        out_shape=jax.ShapeDtypeStruct((t, heads * vdim), BF16),
        compiler_params=_cparams("parallel", "parallel", "arbitrary"),
        name="mla_flash",
    )(q, k, v)


def _top_values(sc, n):
    vals = []
    cur = sc
    for _ in range(n):
        m = jnp.max(cur, axis=0, keepdims=True)
        vals.append(m)
        cur = jnp.where(cur == m, -jnp.inf, cur)
    return vals


def _peer_route_kernel(x_ref, wq_ref, keys_ref, thr_ref, e1_ref, s2_ref, e2_ref, *, heads, topk):
    xb = x_ref[...].astype(BF16)
    q = _dot(xb, wq_ref[...])
    nk = keys_ref.shape[1]
    for h in range(heads):
        sc = []
        top = []
        for p in range(2):
            hp = 2 * h + p
            qs = q[:, hp * nk:(hp + 1) * nk].astype(BF16)
            s = _dot_nt(keys_ref[hp], qs)
            sc.append(s)
            top.append(_top_values(s, topk + 1))
        a, b = top
        cands = [a[i] + b[j] for i in range(topk + 1) for j in range(topk + 1) if (i + 1) * (j + 1) <= topk + 1]
        best = _top_values(jnp.concatenate(cands, axis=0), topk + 1)
        zsum = None
        for r in range(topk):
            e = jnp.exp(best[r] - best[0])
            zsum = e if zsum is None else zsum + e
        tau = 0.5 * (best[topk - 1] + best[topk])
        thr_ref[0, h] = tau - sc[0]
        e1_ref[0, h] = jnp.exp(sc[0] - a[0]) / zsum
        s2_ref[0, h] = sc[1]
        e2_ref[0, h] = jnp.exp(sc[1] - b[0])


def _peer_route(x2d, wq, keys, heads, ts):
    t, d = x2d.shape
    nk = keys.shape[1]
    nt = t // ts
    shp = jax.ShapeDtypeStruct((nt, heads, nk, ts), F32)
    ospec = pl.BlockSpec((1, heads, nk, ts), lambda i: (i, 0, 0, 0))
    return pl.pallas_call(
        functools.partial(_peer_route_kernel, heads=heads, topk=PEER_TOPK),
        grid=(nt,),
        in_specs=[pl.BlockSpec((ts, d), lambda i: (i, 0)),
                  pl.BlockSpec(wq.shape, lambda i: (0, 0)),
                  pl.BlockSpec(keys.shape, lambda i: (0, 0, 0))],
        out_specs=[ospec, ospec, ospec, ospec],
        out_shape=[shp, shp, shp, shp],
        compiler_params=_cparams("parallel"),
        name="peer_route",
    )(x2d, wq, keys)


def _gelu(x):
    return 0.5 * x * (1.0 + lax.erf(x * (2.0 ** -0.5)))


def _peer_dense_kernel(x_ref, u_ref, vt_ref, thr_ref, e1_ref, s2_ref, e2_ref, g_ref, b_ref, o_ref,
                       acc_ref, act_ref, *, heads, nk, eb, alpha):
    e = pl.program_id(1)

    @pl.when(e == 0)
    def _():
        acc_ref[...] = jnp.zeros(acc_ref.shape, F32)

    xb = x_ref[...].astype(BF16)
    for il in range(eb // nk):
        i_glob = e * (eb // nk) + il
        hid = _dot_nt(u_ref[il * nk:(il + 1) * nk, :], xb)
        gate = None
        for h in range(heads):
            thr = thr_ref[0, h, pl.ds(i_glob, 1), :]
            e1 = e1_ref[0, h, pl.ds(i_glob, 1), :]
            term = jnp.where(s2_ref[0, h] >= thr, e2_ref[0, h], 0.0) * e1
            gate = term if gate is None else gate + term
        act_ref[il * nk:(il + 1) * nk, :] = (gate * _gelu(hid)).astype(BF16)
    acc_ref[...] += _dot(vt_ref[...], act_ref[...])

    @pl.when(e == pl.num_programs(1) - 1)
    def _():
        y = alpha * x_ref[...] + acc_ref[...].T
        o_ref[...] = _layer_norm(y, g_ref[...], b_ref[...])


def _peer_dense(x2d, u, vt, thr, e1, s2, e2, g, b, alpha, heads, ts, eb):
    t, d = x2d.shape
    n = u.shape[0]
    nk = thr.shape[2]
    rspec = pl.BlockSpec((1, heads, nk, ts), lambda i, e: (i, 0, 0, 0))
    return pl.pallas_call(
        functools.partial(_peer_dense_kernel, heads=heads, nk=nk, eb=eb, alpha=alpha),
        grid=(t // ts, n // eb),
        in_specs=[pl.BlockSpec((ts, d), lambda i, e: (i, 0)),
                  pl.BlockSpec((eb, d), lambda i, e: (e, 0)),
                  pl.BlockSpec((d, eb), lambda i, e: (0, e)),
                  rspec, rspec, rspec, rspec,
                  pl.BlockSpec((1, d), lambda i, e: (0, 0)),
                  pl.BlockSpec((1, d), lambda i, e: (0, 0))],
        out_specs=pl.BlockSpec((ts, d), lambda i, e: (i, 0)),
        out_shape=jax.ShapeDtypeStruct((t, d), F32),
        scratch_shapes=[pltpu.VMEM((d, ts), F32), pltpu.VMEM((eb, ts), BF16)],
        compiler_params=_cparams("parallel", "arbitrary"),
        name="peer_dense",
    )(x2d, u, vt, thr, e1, s2, e2, g, b)


def _peer(h2d, w_q, sub_keys, u_tab, v_tab, g, b, alpha):
    t, d = h2d.shape
    heads, _, nk, dh = sub_keys.shape
    assert nk == LANES and dh == LANES
    ts = min(512, t)
    eb = min(1024, u_tab.shape[0])
    keys = sub_keys.reshape(heads * 2, nk, dh).astype(BF16)
    thr, e1, s2, e2 = _peer_route(h2d, w_q.astype(BF16), keys, heads, ts)
    return _peer_dense(h2d, u_tab.astype(BF16), v_tab.astype(BF16).T, thr, e1, s2, e2,
                       g.reshape(1, d), b.reshape(1, d), alpha, heads, ts, eb)


def _gdn_layer(h2d, batch, seq, w_in, conv_w, a_log, dt_bias, norm_g, w_out, g, b, alpha):
    t, d = h2d.shape
    heads = a_log.shape[0]
    dv = norm_g.shape[0]
    nv = heads * dv
    nqk = (conv_w.shape[1] - nv) // 2
    assert nqk // heads == LANES and dv == LANES and 2 * heads <= LANES
    chunk = min(GDN_KERNEL_CHUNK, seq)
    ts = min(256, seq)
    tc = min(512, seq)
    cw = 2 * nqk + nv
    wqkv = w_in[:, :cw].astype(BF16)
    wz = w_in[:, cw:cw + nv].astype(BF16)
    wbg = jnp.pad(w_in[:, cw + nv:], ((0, 0), (0, LANES - 2 * heads))).astype(BF16)
    alog = jnp.pad(a_log, (heads, LANES - 2 * heads)).reshape(1, LANES)
    dtb = jnp.pad(dt_bias, (heads, LANES - 2 * heads)).reshape(1, LANES)
    q, k, v, z, bg = _gdn_proj(h2d, wqkv, wz, wbg, conv_w, alog, dtb, batch, seq, heads, nqk, nv, ts, chunk)
    o = _gdn_chunk(q, k, v, z, bg, norm_g.reshape(1, dv), batch, seq, heads, tc, chunk)
    return _proj_ln(o, w_out.astype(BF16), h2d, g.reshape(1, d), b.reshape(1, d), alpha, min(512, t))


def _head_pad(w, heads, width):
    r = w.shape[0]
    return jnp.pad(w.reshape(r, heads, width), ((0, 0), (0, 0), (0, LANES - width))).reshape(r, heads * LANES)


def _rope_swap(w, half):
    return jnp.concatenate([-w[..., half:], w[..., :half]], -1)


def _mla_layer(h2d, positions, batch, seq, shared_kv, kv_w_a, kv_norm_g, kv_w_uk, kv_w_uv, w_qa, q_norm_g, w_qb,
               w_out, g, b, alpha):
    t, d = h2d.shape
    rank = kv_norm_g.shape[0]
    rope = kv_w_a.shape[1] - rank
    qrank = w_qa.shape[1]
    heads = (w_qb.shape[1] - kv_w_uk.shape[1]) // rope
    nope = kv_w_uk.shape[1] // heads
    vdim = kv_w_uv.shape[1] // heads
    half = rope // 2
    assert nope + rope <= LANES and LANES % vdim == 0 and rank % LANES == 0

    inv_freq = ROPE_THETA ** (-jnp.arange(half, dtype=F32) / half)
    invf = jnp.zeros((LANES,), F32).at[nope:nope + half].set(inv_freq).at[nope + half:nope + rope].set(inv_freq)
    posf = positions.astype(F32).reshape(t, 1)

    kr = kv_w_a[:, rank:]
    place = lambda w: jnp.pad(w, ((0, 0), (nope, LANES - nope - rope)))
    wkva = jnp.concatenate([kv_w_a[:, :rank], place(kr), place(_rope_swap(kr, half))], -1).astype(BF16)
    wuk = _head_pad(kv_w_uk, heads, nope).astype(BF16)
    wq3 = w_qb.reshape(qrank, heads, nope + rope)
    wqb = _head_pad(w_qb, heads, nope + rope).astype(BF16)
    wq_sw = jnp.concatenate([jnp.zeros((qrank, heads, nope), F32), _rope_swap(wq3[..., nope:], half)], -1)
    wqbs = _head_pad(wq_sw.reshape(qrank, heads * (nope + rope)), heads, nope + rope).astype(BF16)

    ts = min(512, t)
    q, k, v = _mla_prep(h2d, posf, invf.reshape(1, LANES), wkva, kv_norm_g.reshape(1, rank), wuk,
                        kv_w_uv.astype(BF16), w_qa.astype(BF16), q_norm_g.reshape(1, qrank), wqb, wqbs,
                        heads, rank, nope, rope, vdim, ts)
    if shared_kv is None:
        shared_kv = (k, v)
    tq = min(512, seq)
    o = _flash(q, shared_kv[0], shared_kv[1], batch, seq, heads, vdim, tq, tq)
    h_new = _proj_ln(o, w_out.astype(BF16), h2d, g.reshape(1, d), b.reshape(1, d), alpha, min(512, t))
    return h_new, shared_kv


def kernel(x, positions, a_w_in, a_conv_w, a_a_log, a_dt_bias, a_norm_g, a_w_out, kv_w_a, kv_norm_g, kv_w_uk,
           kv_w_uv, b_w_qa, b_q_norm_g, b_w_qb, b_w_out, peer_w_q, peer_sub_keys, peer_u, peer_v, ln_g, ln_b):
    batch, seq, d = x.shape
    depth = ln_g.shape[0]
    n_a = a_w_in.shape[0]
    alpha = (2 * depth) ** 0.25
    h = x.reshape(batch * seq, d)
    shared_kv = None
    for layer in range(depth):
        if layer < n_a:
            i = layer
            h = _gdn_layer(h, batch, seq, a_w_in[i], a_conv_w[i], a_a_log[i], a_dt_bias[i], a_norm_g[i],
                           a_w_out[i], ln_g[layer, 0], ln_b[layer, 0], alpha)
        else:
            j = layer - n_a
            h, shared_kv = _mla_layer(h, positions, batch, seq, shared_kv, kv_w_a, kv_norm_g, kv_w_uk, kv_w_uv,
                                      b_w_qa[j], b_q_norm_g[j], b_w_qb[j], b_w_out[j],
                                      ln_g[layer, 0], ln_b[layer, 0], alpha)
        h = _peer(h, peer_w_q[layer], peer_sub_keys[layer], peer_u[layer], peer_v[layer],
                  ln_g[layer, 1], ln_b[layer, 1], alpha)
    return h.reshape(batch, seq, d)
```

```python
import functools
import math

import jax
import jax.numpy as jnp
from jax import lax
from jax.experimental import pallas as pl
from jax.experimental.pallas import tpu as pltpu

F32 = jnp.float32
BF16 = jnp.bfloat16
HIGHEST = lax.Precision.HIGHEST

LN_EPS = 1e-5
RMS_EPS = 1e-6
ROPE_THETA = 10000.0
GDN_CONV = 4
PEER_TOPK = 16
LANES = 128
SUBLANES = 8
GDN_KERNEL_CHUNK = 128
VMEM_LIMIT_BYTES = 56 * 1024 * 1024


def _cparams(*sem):
    return pltpu.CompilerParams(dimension_semantics=sem, vmem_limit_bytes=VMEM_LIMIT_BYTES)


def _dot(a, b):
    return jnp.dot(a, b, preferred_element_type=F32)


def _dot_nt(a, b):
    return lax.dot_general(a, b, (((1,), (1,)), ((), ())), preferred_element_type=F32)


def _dot_tn(a, b):
    return lax.dot_general(a, b, (((0,), (0,)), ((), ())), preferred_element_type=F32)


def _dot_hi(a, b):
    return jnp.dot(a, b, precision=HIGHEST, preferred_element_type=F32)


def _sigmoid(x):
    return 1.0 / (1.0 + jnp.exp(-x))


def _softplus(x):
    return jnp.maximum(x, 0.0) + jnp.log1p(jnp.exp(-jnp.abs(x)))


def _layer_norm(y, g, b):
    mu = jnp.mean(y, -1, keepdims=True)
    d = y - mu
    var = jnp.mean(d * d, -1, keepdims=True)
    return d * lax.rsqrt(var + LN_EPS) * g + b


def _rms(x, g):
    return x * lax.rsqrt(jnp.mean(x * x, -1, keepdims=True) + RMS_EPS) * g


def _proj_ln_kernel(a_ref, w_ref, res_ref, g_ref, b_ref, o_ref, *, alpha):
    y = alpha * res_ref[...] + _dot(a_ref[...], w_ref[...])
    o_ref[...] = _layer_norm(y, g_ref[...], b_ref[...])


def _proj_ln(a, w, res, g, b, alpha, tm):
    t, k = a.shape
    d = w.shape[1]
    return pl.pallas_call(
        functools.partial(_proj_ln_kernel, alpha=alpha),
        grid=(t // tm,),
        in_specs=[pl.BlockSpec((tm, k), lambda i: (i, 0)),
                  pl.BlockSpec((k, d), lambda i: (0, 0)),
                  pl.BlockSpec((tm, d), lambda i: (i, 0)),
                  pl.BlockSpec((1, d), lambda i: (0, 0)),
                  pl.BlockSpec((1, d), lambda i: (0, 0))],
        out_specs=pl.BlockSpec((tm, d), lambda i: (i, 0)),
        out_shape=jax.ShapeDtypeStruct((t, d), F32),
        compiler_params=_cparams("parallel"),
        name="proj_ln",
    )(a, w, res, g, b)


def _gdn_proj_kernel(x_ref, wqkv_ref, wz_ref, wbg_ref, convw_ref, alog_ref, dtb_ref, tri_ref,
                     q_ref, k_ref, v_ref, z_ref, bg_ref, buf_ref, *, ts, nqk, nv, heads, chunk):
    s = pl.program_id(1)
    halo = SUBLANES
    xb = x_ref[...].astype(BF16)

    @pl.when(s == 0)
    def _():
        buf_ref[0:halo, :] = jnp.zeros((halo, buf_ref.shape[1]), F32)

    buf_ref[halo:halo + ts, :] = _dot(xb, wqkv_ref[...])
    z_ref[...] = _dot(xb, wz_ref[...]).astype(BF16)

    dk = nqk // heads
    for grp in range((2 * nqk + nv) // LANES):
        c0 = grp * LANES
        acc = None
        for j in range(GDN_CONV):
            r0 = halo - (GDN_CONV - 1) + j
            term = buf_ref[r0:r0 + ts, c0:c0 + LANES] * convw_ref[j:j + 1, c0:c0 + LANES]
            acc = term if acc is None else acc + term
        y = acc * _sigmoid(acc)
        if c0 < 2 * nqk:
            y = y * lax.rsqrt(jnp.sum(y * y, -1, keepdims=True) + RMS_EPS)
        if c0 < nqk:
            q_ref[:, c0:c0 + LANES] = (y * dk ** -0.5).astype(BF16)
        elif c0 < 2 * nqk:
            k_ref[:, c0 - nqk:c0 - nqk + LANES] = y.astype(BF16)
        else:
            v_ref[:, c0 - 2 * nqk:c0 - 2 * nqk + LANES] = y.astype(BF16)

    buf_ref[0:halo, :] = buf_ref[ts:ts + halo, :]

    bgp = _dot(xb, wbg_ref[...])
    lane = lax.broadcasted_iota(jnp.int32, (chunk, LANES), 1)
    for c in range(ts // chunk):
        blk = bgp[c * chunk:(c + 1) * chunk, :]
        beta = _sigmoid(blk)
        g = -jnp.exp(alog_ref[...]) * _softplus(blk + dtb_ref[...])
        gc = _dot_hi(tri_ref[...], g)
        bg_ref[c * chunk:(c + 1) * chunk, :] = jnp.where(lane < heads, beta, gc)


def _gdn_proj(x2d, wqkv, wz, wbg, convw, alog, dtb, batch, seq, heads, nqk, nv, ts, chunk):
    t, d = x2d.shape
    ns = seq // ts
    cw = 2 * nqk + nv
    tri = jnp.tril(jnp.ones((chunk, chunk), F32))
    row = lambda b, s: (b * ns + s, 0)
    fixed = lambda b, s: (0, 0)
    return pl.pallas_call(
        functools.partial(_gdn_proj_kernel, ts=ts, nqk=nqk, nv=nv, heads=heads, chunk=chunk),
        grid=(batch, ns),
        in_specs=[pl.BlockSpec((ts, d), row),
                  pl.BlockSpec((d, cw), fixed),
                  pl.BlockSpec((d, nv), fixed),
                  pl.BlockSpec((d, LANES), fixed),
                  pl.BlockSpec((GDN_CONV, cw), fixed),
                  pl.BlockSpec((1, LANES), fixed),
                  pl.BlockSpec((1, LANES), fixed),
                  pl.BlockSpec((chunk, chunk), fixed)],
        out_specs=[pl.BlockSpec((ts, nqk), row),
                   pl.BlockSpec((ts, nqk), row),
                   pl.BlockSpec((ts, nv), row),
                   pl.BlockSpec((ts, nv), row),
                   pl.BlockSpec((ts, LANES), row)],
        out_shape=[jax.ShapeDtypeStruct((t, nqk), BF16),
                   jax.ShapeDtypeStruct((t, nqk), BF16),
                   jax.ShapeDtypeStruct((t, nv), BF16),
                   jax.ShapeDtypeStruct((t, nv), BF16),
                   jax.ShapeDtypeStruct((t, LANES), F32)],
        scratch_shapes=[pltpu.VMEM((ts + 2 * SUBLANES, cw), F32)],
        compiler_params=_cparams("parallel", "arbitrary"),
        name="gdn_proj",
    )(x2d, wqkv, wz, wbg, convw, alog, dtb, tri)


def _unit_lower_inverse(low, eye):
    n = low.shape[0]
    inv = eye - low
    power = low
    k = 1
    while 2 * k < n:
        pb = power.astype(BF16)
        power = _dot(pb, pb)
        inv = inv + _dot(inv.astype(BF16), power.astype(BF16))
        k *= 2
    return inv


def _gdn_chunk_kernel(q_ref, k_ref, v_ref, z_ref, bg_ref, ng_ref, o_ref, state_ref, *, tc, heads, chunk):
    @pl.when(pl.program_id(1) == 0)
    def _():
        state_ref[...] = jnp.zeros(state_ref.shape, F32)

    rows = lax.broadcasted_iota(jnp.int32, (chunk, chunk), 0)
    cols = lax.broadcasted_iota(jnp.int32, (chunk, chunk), 1)
    tril = rows >= cols
    strict = rows > cols
    eye = jnp.where(rows == cols, 1.0, 0.0).astype(F32)

    def body(c, carry):
        r0 = pl.multiple_of(c * chunk, chunk)
        bg = bg_ref[pl.ds(r0, chunk), :]
        for h in range(heads):
            hs = slice(h * LANES, (h + 1) * LANES)
            beta = bg[:, h:h + 1]
            gcol = bg[:, heads + h:heads + h + 1]
            gmat = jnp.broadcast_to(gcol, (chunk, chunk))
            diff = jnp.where(tril, gmat - gmat.T, 0.0)
            decay = jnp.where(tril, jnp.exp(diff), 0.0)
            eg = jnp.exp(gcol)
            glast = gcol[chunk - 1:chunk, :]
            qb = q_ref[pl.ds(r0, chunk), hs]
            kb = k_ref[pl.ds(r0, chunk), hs]
            kf = kb.astype(F32)
            vf = v_ref[pl.ds(r0, chunk), hs].astype(F32)
            kbeta = kf * beta
            low = jnp.where(strict, _dot_nt(kbeta.astype(BF16), kb) * decay, 0.0)
            inv = _unit_lower_inverse(low, eye).astype(BF16)
            u = _dot(inv, (vf * beta).astype(BF16))
            w = _dot(inv, (kbeta * eg).astype(BF16))
            st = state_ref[h]
            stb = st.astype(BF16)
            v_new = u - _dot(w.astype(BF16), stb)
            vnb = v_new.astype(BF16)
            attn = _dot_nt(qb, kb) * decay
            o = _dot((qb.astype(F32) * eg).astype(BF16), stb) + _dot(attn.astype(BF16), vnb)
            kdec = (kf * jnp.exp(glast - gcol)).astype(BF16)
            state_ref[h] = st * jnp.exp(glast) + _dot_tn(kdec, vnb)
            zf = z_ref[pl.ds(r0, chunk), hs].astype(F32)
            on = _rms(o, ng_ref[...]) * (zf * _sigmoid(zf))
            o_ref[pl.ds(r0, chunk), hs] = on.astype(BF16)
        return carry

    lax.fori_loop(0, tc // chunk, body, 0)


def _gdn_chunk(q, k, v, z, bg, norm_g, batch, seq, heads, tc, chunk):
    t, nqk = q.shape
    nv = v.shape[1]
    ns = seq // tc
    row = lambda b, s: (b * ns + s, 0)
    return pl.pallas_call(
        functools.partial(_gdn_chunk_kernel, tc=tc, heads=heads, chunk=chunk),
        grid=(batch, ns),
        in_specs=[pl.BlockSpec((tc, nqk), row),
                  pl.BlockSpec((tc, nqk), row),
                  pl.BlockSpec((tc, nv), row),
                  pl.BlockSpec((tc, nv), row),
                  pl.BlockSpec((tc, LANES), row),
                  pl.BlockSpec((1, LANES), lambda b, s: (0, 0))],
        out_specs=pl.BlockSpec((tc, nv), row),
        out_shape=jax.ShapeDtypeStruct((t, nv), BF16),
        scratch_shapes=[pltpu.VMEM((heads, nqk // heads, nv // heads), F32)],
        compiler_params=_cparams("parallel", "arbitrary"),
        name="gdn_chunk",
    )(q, k, v, z, bg, norm_g)


def _mla_prep_kernel(x_ref, pos_ref, invf_ref, wkva_ref, gkv_ref, wuk_ref, wuv_ref, wqa_ref, gq_ref,
                     wqb_ref, wqbs_ref, q_ref, k_ref, v_ref, *, heads, rank, nope, rope, scale):
    xb = x_ref[...].astype(BF16)
    ts = xb.shape[0]
    lane = lax.broadcasted_iota(jnp.int32, (ts, LANES), 1)
    ang = pos_ref[...] * invf_ref[...]
    is_rope = (lane >= nope) & (lane < nope + rope)
    cosm = jnp.where(lane < nope, 1.0, jnp.where(is_rope, jnp.cos(ang), 0.0))
    sinm = jnp.where(is_rope, jnp.sin(ang), 0.0)

    kv = _dot(xb, wkva_ref[...])
    ckv = _rms(kv[:, :rank], gkv_ref[...]).astype(BF16)
    krope = kv[:, rank:rank + LANES] * cosm + kv[:, rank + LANES:rank + 2 * LANES] * sinm
    v_ref[...] = _dot(ckv, wuv_ref[...]).astype(BF16)
    knope = _dot(ckv, wuk_ref[...])

    qn = _rms(_dot(xb, wqa_ref[...]), gq_ref[...]).astype(BF16)
    qa = _dot(qn, wqb_ref[...])
    qs = _dot(qn, wqbs_ref[...])
    for h in range(heads):
        hs = slice(h * LANES, (h + 1) * LANES)
        k_ref[:, hs] = (knope[:, hs] + krope).astype(BF16)
        q_ref[:, hs] = ((qa[:, hs] * cosm + qs[:, hs] * sinm) * scale).astype(BF16)


def _mla_prep(x2d, posf, invf, wkva, gkv, wuk, wuv, wqa, gq, wqb, wqbs, heads, rank, nope, rope, vdim, ts):
    t, d = x2d.shape
    qrank = wqa.shape[1]
    scale = (nope + rope) ** -0.5
    row = lambda i: (i, 0)
    fixed = lambda i: (0, 0)
    full = lambda a: pl.BlockSpec(a.shape, fixed)
    return pl.pallas_call(
        functools.partial(_mla_prep_kernel, heads=heads, rank=rank, nope=nope, rope=rope, scale=scale),
        grid=(t // ts,),
        in_specs=[pl.BlockSpec((ts, d), row), pl.BlockSpec((ts, 1), row), full(invf), full(wkva), full(gkv),
                  full(wuk), full(wuv), full(wqa), full(gq), full(wqb), full(wqbs)],
        out_specs=[pl.BlockSpec((ts, heads * LANES), row),
                   pl.BlockSpec((ts, heads * LANES), row),
                   pl.BlockSpec((ts, heads * vdim), row)],
        out_shape=[jax.ShapeDtypeStruct((t, heads * LANES), BF16),
                   jax.ShapeDtypeStruct((t, heads * LANES), BF16),
                   jax.ShapeDtypeStruct((t, heads * vdim), BF16)],
        compiler_params=_cparams("parallel"),
        name="mla_prep",
    )(x2d, posf, invf, wkva, gkv, wuk, wuv, wqa, gq, wqb, wqbs)


def _flash_kernel(q_ref, k_ref, v_ref, o_ref, *, tq, tk, vdim):
    qi = pl.program_id(2)
    hpp = LANES // vdim
    qpos = qi * tq + lax.broadcasted_iota(jnp.int32, (tq, tk), 0)
    kloc = lax.broadcasted_iota(jnp.int32, (tq, tk), 1)
    lane = lax.broadcasted_iota(jnp.int32, (tq, LANES), 1)
    nkv = (qi * tq + tq + tk - 1) // tk

    out = jnp.zeros((tq, LANES), F32)
    for hh in range(hpp):
        hs = slice(hh * LANES, (hh + 1) * LANES)
        qh = q_ref[:, hs]

        def body(c, carry, hs=hs, qh=qh):
            m, l, acc = carry
            k0 = pl.multiple_of(c * tk, tk)
            s = _dot_nt(qh, k_ref[pl.ds(k0, tk), hs])
            s = jnp.where(k0 + kloc <= qpos, s, -jnp.inf)
            m_new = jnp.maximum(m, jnp.max(s, -1, keepdims=True))
            p = jnp.exp(s - m_new)
            corr = jnp.exp(m - m_new)
            l = corr * l + jnp.sum(p, -1, keepdims=True)
            acc = corr * acc + _dot(p.astype(BF16), v_ref[pl.ds(k0, tk), :])
            return m_new, l, acc

        init = (jnp.full((tq, 1), -jnp.inf, F32), jnp.zeros((tq, 1), F32), jnp.zeros((tq, LANES), F32))
        m, l, acc = lax.fori_loop(0, nkv, body, init)
        out = jnp.where((lane >= hh * vdim) & (lane < (hh + 1) * vdim), acc / l, out)
    o_ref[...] = out.astype(BF16)


def _flash(q, k, v, batch, seq, heads, vdim, tq, tk):
    t = q.shape[0]
    hpp = LANES // vdim
    nq = seq // tq
    return pl.pallas_call(
        functools.partial(_flash_kernel, tq=tq, tk=tk, vdim=vdim),
        grid=(batch, heads // hpp, nq),
        in_specs=[pl.BlockSpec((tq, hpp * LANES), lambda b, h, i: (b * nq + i, h)),
                  pl.BlockSpec((seq, hpp * LANES), lambda b, h, i: (b, h)),
                  pl.BlockSpec((seq, LANES), lambda b, h, i: (b, h))],
        out_specs=pl.BlockSpec((tq, LANES), lambda b, h, i: (b * nq + i, h)),
        out_shape=jax.ShapeDtypeStruct((t, heads * vdim), BF16),
        compiler_params=_cparams("parallel", "parallel", "arbitrary"),
        name="mla_flash",
    )(q, k, v)


def _top_values(sc, n):
    vals = []
    cur = sc
    for _ in range(n):
        m = jnp.max(cur, axis=0, keepdims=True)
        vals.append(m)
        cur = jnp.where(cur == m, -jnp.inf, cur)
    return vals


def _peer_route_kernel(x_ref, wq_ref, keys_ref, xb_ref, thr_ref, e1_ref, s2_ref, e2_ref, *, heads, topk):
    xb = x_ref[...].astype(BF16)
    xb_ref[...] = xb
    q = _dot(xb, wq_ref[...])
    nk = keys_ref.shape[1]
    for h in range(heads):
        s1 = _dot_nt(keys_ref[2 * h], q[:, 2 * h * nk:(2 * h + 1) * nk].astype(BF16))
        s2 = _dot_nt(keys_ref[2 * h + 1], q[:, (2 * h + 1) * nk:(2 * h + 2) * nk].astype(BF16))
        a = _top_values(s1, topk + 1)
        b = _top_values(s2, topk + 1)
        cands = [a[i] + b[j] for i in range(topk + 1) for j in range(topk + 1) if (i + 1) * (j + 1) <= topk + 1]
        best = _top_values(jnp.concatenate(cands, axis=0), topk + 1)
        zsum = None
        for r in range(topk):
            e = jnp.exp(best[r] - best[0])
            zsum = e if zsum is None else zsum + e
        thr = 0.5 * (best[topk - 1] + best[topk]) - s1
        outs = ((thr_ref, thr), (e1_ref, jnp.exp(s1 - a[0]) / zsum), (s2_ref, s2), (e2_ref, jnp.exp(s2 - b[0])))
        for ref, val in outs:
            for tg in range(val.shape[1] // LANES):
                ref[0, h, tg] = val[:, tg * LANES:(tg + 1) * LANES]


def _peer_route(x2d, wq, keys, heads, ts):
    t, d = x2d.shape
    nk = keys.shape[1]
    nt = t // ts
    shape = (nt, heads, ts // LANES, nk, LANES)
    ospec = pl.BlockSpec((1,) + shape[1:], lambda i: (i, 0, 0, 0, 0))
    return pl.pallas_call(
        functools.partial(_peer_route_kernel, heads=heads, topk=PEER_TOPK),
        grid=(nt,),
        in_specs=[pl.BlockSpec((ts, d), lambda i: (i, 0)),
                  pl.BlockSpec(wq.shape, lambda i: (0, 0)),
                  pl.BlockSpec(keys.shape, lambda i: (0, 0, 0))],
        out_specs=[pl.BlockSpec((ts, d), lambda i: (i, 0)), ospec, ospec, ospec, ospec],
        out_shape=[jax.ShapeDtypeStruct((t, d), BF16)] + [jax.ShapeDtypeStruct(shape, F32)] * 4,
        compiler_params=_cparams("parallel"),
        name="peer_route",
    )(x2d, wq, keys)


def _gelu(x):
    return 0.5 * x * (1.0 + lax.erf(x * (2.0 ** -0.5)))


GATE_I = 2
GATE_ROWS = 64


def _peer_dense_kernel(x_ref, xb_ref, u_ref, vt_ref, thr_ref, e1_ref, s2_ref, e2_ref, g_ref, b_ref, o_ref,
                       acc_ref, act_ref, hid_ref, *, heads, nk, eb, alpha):
    e = pl.program_id(1)
    ts = xb_ref.shape[0]
    ni = eb // nk
    ntg = ts // LANES

    @pl.when(e == 0)
    def _():
        acc_ref[...] = jnp.zeros(acc_ref.shape, F32)

    hid = _dot_nt(u_ref[...], xb_ref[...])
    for il in range(ni):
        for tg in range(ntg):
            hid_ref[il, tg] = hid[il * nk:(il + 1) * nk, tg * LANES:(tg + 1) * LANES]

    def token_group(tg, carry):
        for ip in range(ni // GATE_I):
            for jc in range(nk // GATE_ROWS):
                js = slice(jc * GATE_ROWS, (jc + 1) * GATE_ROWS)
                gates = [None] * GATE_I
                for h in range(heads):
                    s2 = s2_ref[0, h, tg, js, :]
                    e2 = e2_ref[0, h, tg, js, :]
                    for a in range(GATE_I):
                        il = ip * GATE_I + a
                        thr = thr_ref[0, h, tg, il:il + 1, :]
                        e1 = e1_ref[0, h, tg, il:il + 1, :]
                        term = jnp.where(s2 >= thr, e2, 0.0) * e1
                        gates[a] = term if gates[a] is None else gates[a] + term
                for a in range(GATE_I):
                    il = ip * GATE_I + a
                    act = gates[a] * _gelu(hid_ref[il, tg, js, :])
                    act_ref[tg, il * nk + jc * GATE_ROWS:il * nk + (jc + 1) * GATE_ROWS, :] = act.astype(BF16)
        return carry

    lax.fori_loop(0, ntg, token_group, 0)
    act = jnp.concatenate([act_ref[tg] for tg in range(ntg)], axis=1)
    acc_ref[...] += _dot(vt_ref[...], act)

    @pl.when(e == pl.num_programs(1) - 1)
    def _():
        o_ref[...] = _layer_norm(alpha * x_ref[...] + acc_ref[...].T, g_ref[...], b_ref[...])


def _peer_dense(x2d, xb, u, vt, thr, e1, s2, e2, g, b, alpha, heads, ts, eb):
    t, d = x2d.shape
    n = u.shape[0]
    nk = s2.shape[3]
    ntg = ts // LANES
    assert (eb // nk) % GATE_I == 0 and nk % GATE_ROWS == 0
    jspec = pl.BlockSpec((1, heads, ntg, nk, LANES), lambda i, e: (i, 0, 0, 0, 0))
    ispec = pl.BlockSpec((1, heads, ntg, eb // nk, LANES), lambda i, e: (i, 0, 0, e, 0))
    return pl.pallas_call(
        functools.partial(_peer_dense_kernel, heads=heads, nk=nk, eb=eb, alpha=alpha),
        grid=(t // ts, n // eb),
        in_specs=[pl.BlockSpec((ts, d), lambda i, e: (i, 0)),
                  pl.BlockSpec((ts, d), lambda i, e: (i, 0)),
                  pl.BlockSpec((eb, d), lambda i, e: (e, 0)),
                  pl.BlockSpec((d, eb), lambda i, e: (0, e)),
                  ispec, ispec, jspec, jspec,
                  pl.BlockSpec((1, d), lambda i, e: (0, 0)),
                  pl.BlockSpec((1, d), lambda i, e: (0, 0))],
        out_specs=pl.BlockSpec((ts, d), lambda i, e: (i, 0)),
        out_shape=jax.ShapeDtypeStruct((t, d), F32),
        scratch_shapes=[pltpu.VMEM((d, ts), F32), pltpu.VMEM((ntg, eb, LANES), BF16),
                        pltpu.VMEM((eb // nk, ntg, nk, LANES), F32)],
        compiler_params=_cparams("parallel", "arbitrary"),
        name="peer_dense",
    )(x2d, xb, u, vt, thr, e1, s2, e2, g, b)


def _peer(h2d, w_q, sub_keys, u_tab, v_tab, g, b, alpha):
    t, d = h2d.shape
    heads, _, nk, dh = sub_keys.shape
    assert nk == LANES and dh == LANES
    ts = min(512, t)
    eb = min(1024, u_tab.shape[0])
    keys = sub_keys.reshape(heads * 2, nk, dh).astype(BF16)
    xb, thr, e1, s2, e2 = _peer_route(h2d, w_q.astype(BF16), keys, heads, ts)
    return _peer_dense(h2d, xb, u_tab.astype(BF16), v_tab.astype(BF16).T, thr, e1, s2, e2,
                       g.reshape(1, d), b.reshape(1, d), alpha, heads, ts, eb)


def _gdn_layer(h2d, batch, seq, w_in, conv_w, a_log, dt_bias, norm_g, w_out, g, b, alpha):
    t, d = h2d.shape
    heads = a_log.shape[0]
    dv = norm_g.shape[0]
    nv = heads * dv
    nqk = (conv_w.shape[1] - nv) // 2
    assert nqk // heads == LANES and dv == LANES and 2 * heads <= LANES
    chunk = min(GDN_KERNEL_CHUNK, seq)
    ts = min(256, seq)
    tc = min(512, seq)
    cw = 2 * nqk + nv
    wqkv = w_in[:, :cw].astype(BF16)
    wz = w_in[:, cw:cw + nv].astype(BF16)
    wbg = jnp.pad(w_in[:, cw + nv:], ((0, 0), (0, LANES - 2 * heads))).astype(BF16)
    alog = jnp.pad(a_log, (heads, LANES - 2 * heads)).reshape(1, LANES)
    dtb = jnp.pad(dt_bias, (heads, LANES - 2 * heads)).reshape(1, LANES)
    q, k, v, z, bg = _gdn_proj(h2d, wqkv, wz, wbg, conv_w, alog, dtb, batch, seq, heads, nqk, nv, ts, chunk)
    o = _gdn_chunk(q, k, v, z, bg, norm_g.reshape(1, dv), batch, seq, heads, tc, chunk)
    return _proj_ln(o, w_out.astype(BF16), h2d, g.reshape(1, d), b.reshape(1, d), alpha, min(512, t))


def _head_pad(w, heads, width):
    r = w.shape[0]
    return jnp.pad(w.reshape(r, heads, width), ((0, 0), (0, 0), (0, LANES - width))).reshape(r, heads * LANES)


def _rope_swap(w, half):
    return jnp.concatenate([-w[..., half:], w[..., :half]], -1)


def _mla_layer(h2d, positions, batch, seq, shared_kv, kv_w_a, kv_norm_g, kv_w_uk, kv_w_uv, w_qa, q_norm_g, w_qb,
               w_out, g, b, alpha):
    t, d = h2d.shape
    rank = kv_norm_g.shape[0]
    rope = kv_w_a.shape[1] - rank
    qrank = w_qa.shape[1]
    heads = (w_qb.shape[1] - kv_w_uk.shape[1]) // rope
    nope = kv_w_uk.shape[1] // heads
    vdim = kv_w_uv.shape[1] // heads
    half = rope // 2
    assert nope + rope <= LANES and LANES % vdim == 0 and rank % LANES == 0

    inv_freq = ROPE_THETA ** (-jnp.arange(half, dtype=F32) / half)
    invf = jnp.zeros((LANES,), F32).at[nope:nope + half].set(inv_freq).at[nope + half:nope + rope].set(inv_freq)
    posf = positions.astype(F32).reshape(t, 1)

    kr = kv_w_a[:, rank:]
    place = lambda w: jnp.pad(w, ((0, 0), (nope, LANES - nope - rope)))
    wkva = jnp.concatenate([kv_w_a[:, :rank], place(kr), place(_rope_swap(kr, half))], -1).astype(BF16)
    wuk = _head_pad(kv_w_uk, heads, nope).astype(BF16)
    wq3 = w_qb.reshape(qrank, heads, nope + rope)
    wqb = _head_pad(w_qb, heads, nope + rope).astype(BF16)
    wq_sw = jnp.concatenate([jnp.zeros((qrank, heads, nope), F32), _rope_swap(wq3[..., nope:], half)], -1)
    wqbs = _head_pad(wq_sw.reshape(qrank, heads * (nope + rope)), heads, nope + rope).astype(BF16)

    ts = min(512, t)
    q, k, v = _mla_prep(h2d, posf, invf.reshape(1, LANES), wkva, kv_norm_g.reshape(1, rank), wuk,
                        kv_w_uv.astype(BF16), w_qa.astype(BF16), q_norm_g.reshape(1, qrank), wqb, wqbs,
                        heads, rank, nope, rope, vdim, ts)
    if shared_kv is None:
        shared_kv = (k, v)
    tq = min(512, seq)
    o = _flash(q, shared_kv[0], shared_kv[1], batch, seq, heads, vdim, tq, tq)
    h_new = _proj_ln(o, w_out.astype(BF16), h2d, g.reshape(1, d), b.reshape(1, d), alpha, min(512, t))
    return h_new, shared_kv


def kernel(x, positions, a_w_in, a_conv_w, a_a_log, a_dt_bias, a_norm_g, a_w_out, kv_w_a, kv_norm_g, kv_w_uk,
           kv_w_uv, b_w_qa, b_q_norm_g, b_w_qb, b_w_out, peer_w_q, peer_sub_keys, peer_u, peer_v, ln_g, ln_b):
    batch, seq, d = x.shape
    depth = ln_g.shape[0]
    n_a = a_w_in.shape[0]
    alpha = (2 * depth) ** 0.25
    h = x.reshape(batch * seq, d)
    shared_kv = None
    for layer in range(depth):
        if layer < n_a:
            i = layer
            h = _gdn_layer(h, batch, seq, a_w_in[i], a_conv_w[i], a_a_log[i], a_dt_bias[i], a_norm_g[i],
                           a_w_out[i], ln_g[layer, 0], ln_b[layer, 0], alpha)
        else:
            j = layer - n_a
            h, shared_kv = _mla_layer(h, positions, batch, seq, shared_kv, kv_w_a, kv_norm_g, kv_w_uk, kv_w_uv,
                                      b_w_qa[j], b_q_norm_g[j], b_w_qb[j], b_w_out[j],
                                      ln_g[layer, 0], ln_b[layer, 0], alpha)
        h = _peer(h, peer_w_q[layer], peer_sub_keys[layer], peer_u[layer], peer_v[layer],
                  ln_g[layer, 1], ln_b[layer, 1], alpha)
    return h.reshape(batch, seq, d)
```

```python
import functools
import math

import jax
import jax.numpy as jnp
from jax import lax
from jax.experimental import pallas as pl
from jax.experimental.pallas import tpu as pltpu

F32 = jnp.float32
BF16 = jnp.bfloat16
HIGHEST = lax.Precision.HIGHEST

LN_EPS = 1e-5
RMS_EPS = 1e-6
ROPE_THETA = 10000.0
GDN_CONV = 4
PEER_TOPK = 16
LANES = 128
SUBLANES = 8
GDN_KERNEL_CHUNK = 128
VMEM_LIMIT_BYTES = 56 * 1024 * 1024


def _cparams(*sem):
    return pltpu.CompilerParams(dimension_semantics=sem, vmem_limit_bytes=VMEM_LIMIT_BYTES)


def _dot(a, b):
    return jnp.dot(a, b, preferred_element_type=F32)


def _dot_nt(a, b):
    return lax.dot_general(a, b, (((1,), (1,)), ((), ())), preferred_element_type=F32)


def _dot_tn(a, b):
    return lax.dot_general(a, b, (((0,), (0,)), ((), ())), preferred_element_type=F32)


def _dot_hi(a, b):
    return jnp.dot(a, b, precision=HIGHEST, preferred_element_type=F32)


def _sigmoid(x):
    return 1.0 / (1.0 + jnp.exp(-x))


def _softplus(x):
    return jnp.maximum(x, 0.0) + jnp.log1p(jnp.exp(-jnp.abs(x)))


def _layer_norm(y, g, b):
    mu = jnp.mean(y, -1, keepdims=True)
    d = y - mu
    var = jnp.mean(d * d, -1, keepdims=True)
    return d * lax.rsqrt(var + LN_EPS) * g + b


def _rms(x, g):
    return x * lax.rsqrt(jnp.mean(x * x, -1, keepdims=True) + RMS_EPS) * g


def _proj_ln_kernel(a_ref, w_ref, res_ref, g_ref, b_ref, o_ref, *, alpha):
    y = alpha * res_ref[...] + _dot(a_ref[...], w_ref[...])
    o_ref[...] = _layer_norm(y, g_ref[...], b_ref[...])


def _proj_ln(a, w, res, g, b, alpha, tm):
    t, k = a.shape
    d = w.shape[1]
    return pl.pallas_call(
        functools.partial(_proj_ln_kernel, alpha=alpha),
        grid=(t // tm,),
        in_specs=[pl.BlockSpec((tm, k), lambda i: (i, 0)),
                  pl.BlockSpec((k, d), lambda i: (0, 0)),
                  pl.BlockSpec((tm, d), lambda i: (i, 0)),
                  pl.BlockSpec((1, d), lambda i: (0, 0)),
                  pl.BlockSpec((1, d), lambda i: (0, 0))],
        out_specs=pl.BlockSpec((tm, d), lambda i: (i, 0)),
        out_shape=jax.ShapeDtypeStruct((t, d), F32),
        compiler_params=_cparams("parallel"),
        name="proj_ln",
    )(a, w, res, g, b)


def _gdn_proj_kernel(x_ref, wqkv_ref, wz_ref, wbg_ref, convw_ref, alog_ref, dtb_ref, tri_ref,
                     q_ref, k_ref, v_ref, z_ref, bg_ref, buf_ref, *, ts, nqk, nv, heads, chunk):
    s = pl.program_id(1)
    halo = SUBLANES
    xb = x_ref[...].astype(BF16)

    @pl.when(s == 0)
    def _():
        buf_ref[0:halo, :] = jnp.zeros((halo, buf_ref.shape[1]), F32)

    buf_ref[halo:halo + ts, :] = _dot(xb, wqkv_ref[...])
    z_ref[...] = _dot(xb, wz_ref[...]).astype(BF16)

    dk = nqk // heads
    for grp in range((2 * nqk + nv) // LANES):
        c0 = grp * LANES
        acc = None
        for j in range(GDN_CONV):
            r0 = halo - (GDN_CONV - 1) + j
            term = buf_ref[r0:r0 + ts, c0:c0 + LANES] * convw_ref[j:j + 1, c0:c0 + LANES]
            acc = term if acc is None else acc + term
        y = acc * _sigmoid(acc)
        if c0 < 2 * nqk:
            y = y * lax.rsqrt(jnp.sum(y * y, -1, keepdims=True) + RMS_EPS)
        if c0 < nqk:
            q_ref[:, c0:c0 + LANES] = (y * dk ** -0.5).astype(BF16)
        elif c0 < 2 * nqk:
            k_ref[:, c0 - nqk:c0 - nqk + LANES] = y.astype(BF16)
        else:
            v_ref[:, c0 - 2 * nqk:c0 - 2 * nqk + LANES] = y.astype(BF16)

    buf_ref[0:halo, :] = buf_ref[ts:ts + halo, :]

    bgp = _dot(xb, wbg_ref[...])
    lane = lax.broadcasted_iota(jnp.int32, (chunk, LANES), 1)
    for c in range(ts // chunk):
        blk = bgp[c * chunk:(c + 1) * chunk, :]
        beta = _sigmoid(blk)
        g = -jnp.exp(alog_ref[...]) * _softplus(blk + dtb_ref[...])
        gc = _dot_hi(tri_ref[...], g)
        bg_ref[c * chunk:(c + 1) * chunk, :] = jnp.where(lane < heads, beta, gc)


def _gdn_proj(x2d, wqkv, wz, wbg, convw, alog, dtb, batch, seq, heads, nqk, nv, ts, chunk):
    t, d = x2d.shape
    ns = seq // ts
    cw = 2 * nqk + nv
    tri = jnp.tril(jnp.ones((chunk, chunk), F32))
    row = lambda b, s: (b * ns + s, 0)
    fixed = lambda b, s: (0, 0)
    return pl.pallas_call(
        functools.partial(_gdn_proj_kernel, ts=ts, nqk=nqk, nv=nv, heads=heads, chunk=chunk),
        grid=(batch, ns),
        in_specs=[pl.BlockSpec((ts, d), row),
                  pl.BlockSpec((d, cw), fixed),
                  pl.BlockSpec((d, nv), fixed),
                  pl.BlockSpec((d, LANES), fixed),
                  pl.BlockSpec((GDN_CONV, cw), fixed),
                  pl.BlockSpec((1, LANES), fixed),
                  pl.BlockSpec((1, LANES), fixed),
                  pl.BlockSpec((chunk, chunk), fixed)],
        out_specs=[pl.BlockSpec((ts, nqk), row),
                   pl.BlockSpec((ts, nqk), row),
                   pl.BlockSpec((ts, nv), row),
                   pl.BlockSpec((ts, nv), row),
                   pl.BlockSpec((ts, LANES), row)],
        out_shape=[jax.ShapeDtypeStruct((t, nqk), BF16),
                   jax.ShapeDtypeStruct((t, nqk), BF16),
                   jax.ShapeDtypeStruct((t, nv), BF16),
                   jax.ShapeDtypeStruct((t, nv), BF16),
                   jax.ShapeDtypeStruct((t, LANES), F32)],
        scratch_shapes=[pltpu.VMEM((ts + 2 * SUBLANES, cw), F32)],
        compiler_params=_cparams("parallel", "arbitrary"),
        name="gdn_proj",
    )(x2d, wqkv, wz, wbg, convw, alog, dtb, tri)


def _gdn_chunk_kernel(q_ref, k_ref, v_ref, z_ref, bg_ref, ng_ref, o_ref, state_ref, *, tc, heads, chunk):
    @pl.when(pl.program_id(1) == 0)
    def _():
        state_ref[...] = jnp.zeros(state_ref.shape, F32)

    rows = lax.broadcasted_iota(jnp.int32, (chunk, chunk), 0)
    cols = lax.broadcasted_iota(jnp.int32, (chunk, chunk), 1)
    tril = rows >= cols
    strict = rows > cols
    eye = jnp.where(rows == cols, 1.0, 0.0).astype(F32)
    hr = range(heads)

    def body(c, carry):
        r0 = pl.multiple_of(c * chunk, chunk)
        bg = bg_ref[pl.ds(r0, chunk), :]
        hs = [slice(h * LANES, (h + 1) * LANES) for h in hr]
        beta = [bg[:, h:h + 1] for h in hr]
        gcol = [bg[:, heads + h:heads + h + 1] for h in hr]
        glast = [g[chunk - 1:chunk, :] for g in gcol]
        eg = [jnp.exp(g) for g in gcol]
        decay = []
        for h in hr:
            gmat = jnp.broadcast_to(gcol[h], (chunk, chunk))
            decay.append(jnp.where(tril, jnp.exp(jnp.where(tril, gmat - gmat.T, 0.0)), 0.0))
        qb = [q_ref[pl.ds(r0, chunk), hs[h]] for h in hr]
        kb = [k_ref[pl.ds(r0, chunk), hs[h]] for h in hr]
        kf = [k.astype(F32) for k in kb]
        kbeta = [kf[h] * beta[h] for h in hr]
        vbeta = [v_ref[pl.ds(r0, chunk), hs[h]].astype(F32) * beta[h] for h in hr]

        kk = [_dot_nt(jnp.concatenate([kbeta[h].astype(BF16), qb[h]], axis=0), kb[h]) for h in hr]
        low = [jnp.where(strict, kk[h][:chunk] * decay[h], 0.0) for h in hr]
        attn = [(kk[h][chunk:] * decay[h]).astype(BF16) for h in hr]

        inv = [eye - l for l in low]
        power = low
        span = 1
        while 2 * span < chunk:
            pb = [p.astype(BF16) for p in power]
            power = [_dot(p, p) for p in pb]
            inv = [inv[h] + _dot(inv[h].astype(BF16), power[h].astype(BF16)) for h in hr]
            span *= 2

        rhs = [jnp.concatenate([vbeta[h], kbeta[h] * eg[h]], axis=1).astype(BF16) for h in hr]
        sol = [_dot(inv[h].astype(BF16), rhs[h]) for h in hr]
        dv = vbeta[0].shape[1]

        st = [state_ref[h] for h in hr]
        lhs = [jnp.concatenate([sol[h][:, dv:], qb[h].astype(F32) * eg[h]], axis=0).astype(BF16) for h in hr]
        ws = [_dot(lhs[h], st[h].astype(BF16)) for h in hr]
        vnb = [(sol[h][:, :dv] - ws[h][:chunk]).astype(BF16) for h in hr]
        o = [ws[h][chunk:] + _dot(attn[h], vnb[h]) for h in hr]
        for h in hr:
            kdec = (kf[h] * jnp.exp(glast[h] - gcol[h])).astype(BF16)
            state_ref[h] = st[h] * jnp.exp(glast[h]) + _dot_tn(kdec, vnb[h])
        for h in hr:
            zf = z_ref[pl.ds(r0, chunk), hs[h]].astype(F32)
            on = _rms(o[h], ng_ref[...]) * (zf * _sigmoid(zf))
            o_ref[pl.ds(r0, chunk), hs[h]] = on.astype(BF16)
        return carry

    lax.fori_loop(0, tc // chunk, body, 0)


def _gdn_chunk(q, k, v, z, bg, norm_g, batch, seq, heads, tc, chunk):
    t, nqk = q.shape
    nv = v.shape[1]
    ns = seq // tc
    row = lambda b, s: (b * ns + s, 0)
    return pl.pallas_call(
        functools.partial(_gdn_chunk_kernel, tc=tc, heads=heads, chunk=chunk),
        grid=(batch, ns),
        in_specs=[pl.BlockSpec((tc, nqk), row),
                  pl.BlockSpec((tc, nqk), row),
                  pl.BlockSpec((tc, nv), row),
                  pl.BlockSpec((tc, nv), row),
                  pl.BlockSpec((tc, LANES), row),
                  pl.BlockSpec((1, LANES), lambda b, s: (0, 0))],
        out_specs=pl.BlockSpec((tc, nv), row),
        out_shape=jax.ShapeDtypeStruct((t, nv), BF16),
        scratch_shapes=[pltpu.VMEM((heads, nqk // heads, nv // heads), F32)],
        compiler_params=_cparams("parallel", "arbitrary"),
        name="gdn_chunk",
    )(q, k, v, z, bg, norm_g)


def _mla_prep_kernel(x_ref, pos_ref, invf_ref, wkva_ref, gkv_ref, wuk_ref, wuv_ref, wqa_ref, gq_ref,
                     wqb_ref, wqbs_ref, q_ref, k_ref, v_ref, *, heads, rank, nope, rope, scale):
    xb = x_ref[...].astype(BF16)
    ts = xb.shape[0]
    lane = lax.broadcasted_iota(jnp.int32, (ts, LANES), 1)
    ang = pos_ref[...] * invf_ref[...]
    is_rope = (lane >= nope) & (lane < nope + rope)
    cosm = jnp.where(lane < nope, 1.0, jnp.where(is_rope, jnp.cos(ang), 0.0))
    sinm = jnp.where(is_rope, jnp.sin(ang), 0.0)

    kv = _dot(xb, wkva_ref[...])
    ckv = _rms(kv[:, :rank], gkv_ref[...]).astype(BF16)
    krope = kv[:, rank:rank + LANES] * cosm + kv[:, rank + LANES:rank + 2 * LANES] * sinm
    v_ref[...] = _dot(ckv, wuv_ref[...]).astype(BF16)
    knope = _dot(ckv, wuk_ref[...])

    qn = _rms(_dot(xb, wqa_ref[...]), gq_ref[...]).astype(BF16)
    qa = _dot(qn, wqb_ref[...])
    qs = _dot(qn, wqbs_ref[...])
    for h in range(heads):
        hs = slice(h * LANES, (h + 1) * LANES)
        k_ref[:, hs] = (knope[:, hs] + krope).astype(BF16)
        q_ref[:, hs] = ((qa[:, hs] * cosm + qs[:, hs] * sinm) * scale).astype(BF16)


def _mla_prep(x2d, posf, invf, wkva, gkv, wuk, wuv, wqa, gq, wqb, wqbs, heads, rank, nope, rope, vdim, ts):
    t, d = x2d.shape
    qrank = wqa.shape[1]
    scale = (nope + rope) ** -0.5
    row = lambda i: (i, 0)
    fixed = lambda i: (0, 0)
    full = lambda a: pl.BlockSpec(a.shape, fixed)
    return pl.pallas_call(
        functools.partial(_mla_prep_kernel, heads=heads, rank=rank, nope=nope, rope=rope, scale=scale),
        grid=(t // ts,),
        in_specs=[pl.BlockSpec((ts, d), row), pl.BlockSpec((ts, 1), row), full(invf), full(wkva), full(gkv),
                  full(wuk), full(wuv), full(wqa), full(gq), full(wqb), full(wqbs)],
        out_specs=[pl.BlockSpec((ts, heads * LANES), row),
                   pl.BlockSpec((ts, heads * LANES), row),
                   pl.BlockSpec((ts, heads * vdim), row)],
        out_shape=[jax.ShapeDtypeStruct((t, heads * LANES), BF16),
                   jax.ShapeDtypeStruct((t, heads * LANES), BF16),
                   jax.ShapeDtypeStruct((t, heads * vdim), BF16)],
        compiler_params=_cparams("parallel"),
        name="mla_prep",
    )(x2d, posf, invf, wkva, gkv, wuk, wuv, wqa, gq, wqb, wqbs)


def _flash_kernel(q_ref, k_ref, v_ref, o_ref, *, tq, tk, vdim):
    qi = pl.program_id(2)
    hpp = LANES // vdim
    lane = lax.broadcasted_iota(jnp.int32, (tq, LANES), 1)
    qs = [q_ref[:, hh * LANES:(hh + 1) * LANES] for hh in range(hpp)]

    def chunk(c, carry, masked):
        k0 = pl.multiple_of(c * tk, tk)
        vc = v_ref[pl.ds(k0, tk), :]
        ss = [_dot_nt(qs[hh], k_ref[pl.ds(k0, tk), hh * LANES:(hh + 1) * LANES]) for hh in range(hpp)]
        if masked:
            keep = lax.broadcasted_iota(jnp.int32, (tq, tk), 1) <= lax.broadcasted_iota(jnp.int32, (tq, tk), 0)
            ss = [jnp.where(keep, s, -jnp.inf) for s in ss]
        out = []
        for hh in range(hpp):
            m, l, acc = carry[hh]
            m_new = jnp.maximum(m, jnp.max(ss[hh], -1, keepdims=True))
            p = jnp.exp(ss[hh] - m_new)
            corr = jnp.exp(m - m_new)
            l = corr * l + jnp.sum(p, -1, keepdims=True)
            acc = corr * acc + _dot(p.astype(BF16), vc)
            out.append((m_new, l, acc))
        return tuple(out)

    init = tuple((jnp.full((tq, 1), -jnp.inf, F32), jnp.zeros((tq, 1), F32), jnp.zeros((tq, LANES), F32))
                 for _ in range(hpp))
    carry = lax.fori_loop(0, qi, functools.partial(chunk, masked=False), init)
    carry = chunk(qi, carry, masked=True)
    out = jnp.zeros((tq, LANES), F32)
    for hh in range(hpp):
        m, l, acc = carry[hh]
        out = jnp.where((lane >= hh * vdim) & (lane < (hh + 1) * vdim), acc / l, out)
    o_ref[...] = out.astype(BF16)


def _flash(q, k, v, batch, seq, heads, vdim, tq, tk):
    t = q.shape[0]
    hpp = LANES // vdim
    nq = seq // tq
    assert tq == tk
    return pl.pallas_call(
        functools.partial(_flash_kernel, tq=tq, tk=tk, vdim=vdim),
        grid=(batch, heads // hpp, nq),
        in_specs=[pl.BlockSpec((tq, hpp * LANES), lambda b, h, i: (b * nq + i, h)),
                  pl.BlockSpec((seq, hpp * LANES), lambda b, h, i: (b, h)),
                  pl.BlockSpec((seq, LANES), lambda b, h, i: (b, h))],
        out_specs=pl.BlockSpec((tq, LANES), lambda b, h, i: (b * nq + i, h)),
        out_shape=jax.ShapeDtypeStruct((t, heads * vdim), BF16),
        compiler_params=_cparams("parallel", "parallel", "arbitrary"),
        name="mla_flash",
    )(q, k, v)


def _top_values(sc, n):
    vals = []
    cur = sc
    for _ in range(n):
        m = jnp.max(cur, axis=0, keepdims=True)
        vals.append(m)
        cur = jnp.where(cur == m, -jnp.inf, cur)
    return vals


def _peer_route_kernel(x_ref, wq_ref, keys_ref, xb_ref, thr_ref, e1_ref, s2_ref, e2_ref, *, heads, topk):
    xb = x_ref[...].astype(BF16)
    xb_ref[...] = xb
    q = _dot(xb, wq_ref[...])
    nk = keys_ref.shape[1]
    for h in range(heads):
        s1 = _dot_nt(keys_ref[2 * h], q[:, 2 * h * nk:(2 * h + 1) * nk].astype(BF16))
        s2 = _dot_nt(keys_ref[2 * h + 1], q[:, (2 * h + 1) * nk:(2 * h + 2) * nk].astype(BF16))
        a = _top_values(s1, topk + 1)
        b = _top_values(s2, topk + 1)
        cands = [a[i] + b[j] for i in range(topk + 1) for j in range(topk + 1) if (i + 1) * (j + 1) <= topk + 1]
        best = _top_values(jnp.concatenate(cands, axis=0), topk + 1)
        zsum = None
        for r in range(topk):
            e = jnp.exp(best[r] - best[0])
            zsum = e if zsum is None else zsum + e
        thr = 0.5 * (best[topk - 1] + best[topk]) - s1
        outs = ((thr_ref, thr), (e1_ref, jnp.exp(s1 - a[0]) / zsum), (s2_ref, s2), (e2_ref, jnp.exp(s2 - b[0])))
        for ref, val in outs:
            for tg in range(val.shape[1] // LANES):
                ref[0, h, tg] = val[:, tg * LANES:(tg + 1) * LANES]


def _peer_route(x2d, wq, keys, heads, ts):
    t, d = x2d.shape
    nk = keys.shape[1]
    nt = t // ts
    shape = (nt, heads, ts // LANES, nk, LANES)
    ospec = pl.BlockSpec((1,) + shape[1:], lambda i: (i, 0, 0, 0, 0))
    return pl.pallas_call(
        functools.partial(_peer_route_kernel, heads=heads, topk=PEER_TOPK),
        grid=(nt,),
        in_specs=[pl.BlockSpec((ts, d), lambda i: (i, 0)),
                  pl.BlockSpec(wq.shape, lambda i: (0, 0)),
                  pl.BlockSpec(keys.shape, lambda i: (0, 0, 0))],
        out_specs=[pl.BlockSpec((ts, d), lambda i: (i, 0)), ospec, ospec, ospec, ospec],
        out_shape=[jax.ShapeDtypeStruct((t, d), BF16)] + [jax.ShapeDtypeStruct(shape, F32)] * 4,
        compiler_params=_cparams("parallel"),
        name="peer_route",
    )(x2d, wq, keys)


def _gelu(x):
    return 0.5 * x * (1.0 + lax.erf(x * (2.0 ** -0.5)))


GATE_I = 2
GATE_ROWS = 64


def _peer_dense_kernel(x_ref, xb_ref, u_ref, vt_ref, thr_ref, e1_ref, s2_ref, e2_ref, g_ref, b_ref, o_ref,
                       acc_ref, act_ref, hid_ref, *, heads, nk, eb, alpha):
    e = pl.program_id(1)
    ts = xb_ref.shape[0]
    ni = eb // nk
    ntg = ts // LANES

    @pl.when(e == 0)
    def _():
        acc_ref[...] = jnp.zeros(acc_ref.shape, F32)

    hid = _dot_nt(u_ref[...], xb_ref[...])
    for il in range(ni):
        for tg in range(ntg):
            hid_ref[il, tg] = hid[il * nk:(il + 1) * nk, tg * LANES:(tg + 1) * LANES]

    def token_group(tg, carry):
        for ip in range(ni // GATE_I):
            for jc in range(nk // GATE_ROWS):
                js = slice(jc * GATE_ROWS, (jc + 1) * GATE_ROWS)
                gates = [None] * GATE_I
                for h in range(heads):
                    s2 = s2_ref[0, h, tg, js, :]
                    e2 = e2_ref[0, h, tg, js, :]
                    for a in range(GATE_I):
                        il = ip * GATE_I + a
                        thr = thr_ref[0, h, tg, il:il + 1, :]
                        e1 = e1_ref[0, h, tg, il:il + 1, :]
                        term = jnp.where(s2 >= thr, e2, 0.0) * e1
                        gates[a] = term if gates[a] is None else gates[a] + term
                for a in range(GATE_I):
                    il = ip * GATE_I + a
                    act = gates[a] * _gelu(hid_ref[il, tg, js, :])
                    act_ref[tg, il * nk + jc * GATE_ROWS:il * nk + (jc + 1) * GATE_ROWS, :] = act.astype(BF16)
        return carry

    lax.fori_loop(0, ntg, token_group, 0)
    act = jnp.concatenate([act_ref[tg] for tg in range(ntg)], axis=1)
    acc_ref[...] += _dot(vt_ref[...], act)

    @pl.when(e == pl.num_programs(1) - 1)
    def _():
        o_ref[...] = _layer_norm(alpha * x_ref[...] + acc_ref[...].T, g_ref[...], b_ref[...])


def _peer_dense(x2d, xb, u, vt, thr, e1, s2, e2, g, b, alpha, heads, ts, eb):
    t, d = x2d.shape
    n = u.shape[0]
    nk = s2.shape[3]
    ntg = ts // LANES
    assert (eb // nk) % GATE_I == 0 and nk % GATE_ROWS == 0
    jspec = pl.BlockSpec((1, heads, ntg, nk, LANES), lambda i, e: (i, 0, 0, 0, 0))
    ispec = pl.BlockSpec((1, heads, ntg, eb // nk, LANES), lambda i, e: (i, 0, 0, e, 0))
    return pl.pallas_call(
        functools.partial(_peer_dense_kernel, heads=heads, nk=nk, eb=eb, alpha=alpha),
        grid=(t // ts, n // eb),
        in_specs=[pl.BlockSpec((ts, d), lambda i, e: (i, 0)),
                  pl.BlockSpec((ts, d), lambda i, e: (i, 0)),
                  pl.BlockSpec((eb, d), lambda i, e: (e, 0)),
                  pl.BlockSpec((d, eb), lambda i, e: (0, e)),
                  ispec, ispec, jspec, jspec,
                  pl.BlockSpec((1, d), lambda i, e: (0, 0)),
                  pl.BlockSpec((1, d), lambda i, e: (0, 0))],
        out_specs=pl.BlockSpec((ts, d), lambda i, e: (i, 0)),
        out_shape=jax.ShapeDtypeStruct((t, d), F32),
        scratch_shapes=[pltpu.VMEM((d, ts), F32), pltpu.VMEM((ntg, eb, LANES), BF16),
                        pltpu.VMEM((eb // nk, ntg, nk, LANES), F32)],
        compiler_params=_cparams("parallel", "arbitrary"),
        name="peer_dense",
    )(x2d, xb, u, vt, thr, e1, s2, e2, g, b)


def _peer(h2d, w_q, sub_keys, u_tab, v_tab, g, b, alpha):
    t, d = h2d.shape
    heads, _, nk, dh = sub_keys.shape
    assert nk == LANES and dh == LANES
    ts = min(512, t)
    eb = min(1024, u_tab.shape[0])
    keys = sub_keys.reshape(heads * 2, nk, dh).astype(BF16)
    xb, thr, e1, s2, e2 = _peer_route(h2d, w_q.astype(BF16), keys, heads, ts)
    return _peer_dense(h2d, xb, u_tab.astype(BF16), v_tab.astype(BF16).T, thr, e1, s2, e2,
                       g.reshape(1, d), b.reshape(1, d), alpha, heads, ts, eb)


def _gdn_layer(h2d, batch, seq, w_in, conv_w, a_log, dt_bias, norm_g, w_out, g, b, alpha):
    t, d = h2d.shape
    heads = a_log.shape[0]
    dv = norm_g.shape[0]
    nv = heads * dv
    nqk = (conv_w.shape[1] - nv) // 2
    assert nqk // heads == LANES and dv == LANES and 2 * heads <= LANES
    chunk = min(GDN_KERNEL_CHUNK, seq)
    ts = min(256, seq)
    tc = min(512, seq)
    cw = 2 * nqk + nv
    wqkv = w_in[:, :cw].astype(BF16)
    wz = w_in[:, cw:cw + nv].astype(BF16)
    wbg = jnp.pad(w_in[:, cw + nv:], ((0, 0), (0, LANES - 2 * heads))).astype(BF16)
    alog = jnp.pad(a_log, (heads, LANES - 2 * heads)).reshape(1, LANES)
    dtb = jnp.pad(dt_bias, (heads, LANES - 2 * heads)).reshape(1, LANES)
    q, k, v, z, bg = _gdn_proj(h2d, wqkv, wz, wbg, conv_w, alog, dtb, batch, seq, heads, nqk, nv, ts, chunk)
    o = _gdn_chunk(q, k, v, z, bg, norm_g.reshape(1, dv), batch, seq, heads, tc, chunk)
    return _proj_ln(o, w_out.astype(BF16), h2d, g.reshape(1, d), b.reshape(1, d), alpha, min(512, t))


def _head_pad(w, heads, width):
    r = w.shape[0]
    return jnp.pad(w.reshape(r, heads, width), ((0, 0), (0, 0), (0, LANES - width))).reshape(r, heads * LANES)


def _rope_swap(w, half):
    return jnp.concatenate([-w[..., half:], w[..., :half]], -1)


def _mla_layer(h2d, positions, batch, seq, shared_kv, kv_w_a, kv_norm_g, kv_w_uk, kv_w_uv, w_qa, q_norm_g, w_qb,
               w_out, g, b, alpha):
    t, d = h2d.shape
    rank = kv_norm_g.shape[0]
    rope = kv_w_a.shape[1] - rank
    qrank = w_qa.shape[1]
    heads = (w_qb.shape[1] - kv_w_uk.shape[1]) // rope
    nope = kv_w_uk.shape[1] // heads
    vdim = kv_w_uv.shape[1] // heads
    half = rope // 2
    assert nope + rope <= LANES and LANES % vdim == 0 and rank % LANES == 0

    inv_freq = ROPE_THETA ** (-jnp.arange(half, dtype=F32) / half)
    invf = jnp.zeros((LANES,), F32).at[nope:nope + half].set(inv_freq).at[nope + half:nope + rope].set(inv_freq)
    posf = positions.astype(F32).reshape(t, 1)

    kr = kv_w_a[:, rank:]
    place = lambda w: jnp.pad(w, ((0, 0), (nope, LANES - nope - rope)))
    wkva = jnp.concatenate([kv_w_a[:, :rank], place(kr), place(_rope_swap(kr, half))], -1).astype(BF16)
    wuk = _head_pad(kv_w_uk, heads, nope).astype(BF16)
    wq3 = w_qb.reshape(qrank, heads, nope + rope)
    wqb = _head_pad(w_qb, heads, nope + rope).astype(BF16)
    wq_sw = jnp.concatenate([jnp.zeros((qrank, heads, nope), F32), _rope_swap(wq3[..., nope:], half)], -1)
    wqbs = _head_pad(wq_sw.reshape(qrank, heads * (nope + rope)), heads, nope + rope).astype(BF16)

    ts = min(512, t)
    q, k, v = _mla_prep(h2d, posf, invf.reshape(1, LANES), wkva, kv_norm_g.reshape(1, rank), wuk,
                        kv_w_uv.astype(BF16), w_qa.astype(BF16), q_norm_g.reshape(1, qrank), wqb, wqbs,
                        heads, rank, nope, rope, vdim, ts)
    if shared_kv is None:
        shared_kv = (k, v)
    tq = min(512, seq)
    o = _flash(q, shared_kv[0], shared_kv[1], batch, seq, heads, vdim, tq, tq)
    h_new = _proj_ln(o, w_out.astype(BF16), h2d, g.reshape(1, d), b.reshape(1, d), alpha, min(512, t))
    return h_new, shared_kv


def kernel(x, positions, a_w_in, a_conv_w, a_a_log, a_dt_bias, a_norm_g, a_w_out, kv_w_a, kv_norm_g, kv_w_uk,
           kv_w_uv, b_w_qa, b_q_norm_g, b_w_qb, b_w_out, peer_w_q, peer_sub_keys, peer_u, peer_v, ln_g, ln_b):
    batch, seq, d = x.shape
    depth = ln_g.shape[0]
    n_a = a_w_in.shape[0]
    alpha = (2 * depth) ** 0.25
    h = x.reshape(batch * seq, d)
    shared_kv = None
    for layer in range(depth):
        if layer < n_a:
            i = layer
            h = _gdn_layer(h, batch, seq, a_w_in[i], a_conv_w[i], a_a_log[i], a_dt_bias[i], a_norm_g[i],
                           a_w_out[i], ln_g[layer, 0], ln_b[layer, 0], alpha)
        else:
            j = layer - n_a
            h, shared_kv = _mla_layer(h, positions, batch, seq, shared_kv, kv_w_a, kv_norm_g, kv_w_uk, kv_w_uv,
                                      b_w_qa[j], b_q_norm_g[j], b_w_qb[j], b_w_out[j],
                                      ln_g[layer, 0], ln_b[layer, 0], alpha)
        h = _peer(h, peer_w_q[layer], peer_sub_keys[layer], peer_u[layer], peer_v[layer],
                  ln_g[layer, 1], ln_b[layer, 1], alpha)
    return h.reshape(batch, seq, d)
```

```python
import functools
import math

import jax
import jax.numpy as jnp
from jax import lax
from jax.experimental import pallas as pl
from jax.experimental.pallas import tpu as pltpu

F32 = jnp.float32
BF16 = jnp.bfloat16
HIGHEST = lax.Precision.HIGHEST

LN_EPS = 1e-5
RMS_EPS = 1e-6
ROPE_THETA = 10000.0
GDN_CONV = 4
PEER_TOPK = 16
LANES = 128
SUBLANES = 8
GDN_KERNEL_CHUNK = 128
VMEM_LIMIT_BYTES = 56 * 1024 * 1024


def _cparams(*sem):
    return pltpu.CompilerParams(dimension_semantics=sem, vmem_limit_bytes=VMEM_LIMIT_BYTES)


def _dot(a, b):
    return jnp.dot(a, b, preferred_element_type=F32)


def _dot_nt(a, b):
    return lax.dot_general(a, b, (((1,), (1,)), ((), ())), preferred_element_type=F32)


def _dot_tn(a, b):
    return lax.dot_general(a, b, (((0,), (0,)), ((), ())), preferred_element_type=F32)


def _dot_hi(a, b):
    return jnp.dot(a, b, precision=HIGHEST, preferred_element_type=F32)


def _sigmoid(x):
    return 1.0 / (1.0 + jnp.exp(-x))


def _softplus(x):
    return jnp.maximum(x, 0.0) + jnp.log1p(jnp.exp(-jnp.abs(x)))


def _layer_norm(y, g, b):
    mu = jnp.mean(y, -1, keepdims=True)
    d = y - mu
    var = jnp.mean(d * d, -1, keepdims=True)
    return d * lax.rsqrt(var + LN_EPS) * g + b


def _rms(x, g):
    return x * lax.rsqrt(jnp.mean(x * x, -1, keepdims=True) + RMS_EPS) * g


def _proj_ln_kernel(a_ref, w_ref, res_ref, g_ref, b_ref, o_ref, *, alpha):
    y = alpha * res_ref[...] + _dot(a_ref[...], w_ref[...])
    o_ref[...] = _layer_norm(y, g_ref[...], b_ref[...])


def _proj_ln(a, w, res, g, b, alpha, tm):
    t, k = a.shape
    d = w.shape[1]
    return pl.pallas_call(
        functools.partial(_proj_ln_kernel, alpha=alpha),
        grid=(t // tm,),
        in_specs=[pl.BlockSpec((tm, k), lambda i: (i, 0)),
                  pl.BlockSpec((k, d), lambda i: (0, 0)),
                  pl.BlockSpec((tm, d), lambda i: (i, 0)),
                  pl.BlockSpec((1, d), lambda i: (0, 0)),
                  pl.BlockSpec((1, d), lambda i: (0, 0))],
        out_specs=pl.BlockSpec((tm, d), lambda i: (i, 0)),
        out_shape=jax.ShapeDtypeStruct((t, d), F32),
        compiler_params=_cparams("parallel"),
        name="proj_ln",
    )(a, w, res, g, b)


def _gdn_proj_kernel(x_ref, wqkv_ref, wz_ref, wbg_ref, convw_ref, alog_ref, dtb_ref, tri_ref,
                     q_ref, k_ref, v_ref, z_ref, bg_ref, buf_ref, *, ts, nqk, nv, heads, chunk):
    s = pl.program_id(1)
    halo = SUBLANES
    xb = x_ref[...].astype(BF16)

    @pl.when(s == 0)
    def _():
        buf_ref[0:halo, :] = jnp.zeros((halo, buf_ref.shape[1]), F32)

    buf_ref[halo:halo + ts, :] = _dot(xb, wqkv_ref[...])
    z_ref[...] = _dot(xb, wz_ref[...]).astype(BF16)

    dk = nqk // heads
    for grp in range((2 * nqk + nv) // LANES):
        c0 = grp * LANES
        acc = None
        for j in range(GDN_CONV):
            r0 = halo - (GDN_CONV - 1) + j
            term = buf_ref[r0:r0 + ts, c0:c0 + LANES] * convw_ref[j:j + 1, c0:c0 + LANES]
            acc = term if acc is None else acc + term
        y = acc * _sigmoid(acc)
        if c0 < 2 * nqk:
            y = y * lax.rsqrt(jnp.sum(y * y, -1, keepdims=True) + RMS_EPS)
        if c0 < nqk:
            q_ref[:, c0:c0 + LANES] = (y * dk ** -0.5).astype(BF16)
        elif c0 < 2 * nqk:
            k_ref[:, c0 - nqk:c0 - nqk + LANES] = y.astype(BF16)
        else:
            v_ref[:, c0 - 2 * nqk:c0 - 2 * nqk + LANES] = y.astype(BF16)

    buf_ref[0:halo, :] = buf_ref[ts:ts + halo, :]

    bgp = _dot(xb, wbg_ref[...])
    lane = lax.broadcasted_iota(jnp.int32, (chunk, LANES), 1)
    for c in range(ts // chunk):
        blk = bgp[c * chunk:(c + 1) * chunk, :]
        beta = _sigmoid(blk)
        g = -jnp.exp(alog_ref[...]) * _softplus(blk + dtb_ref[...])
        gc = _dot_hi(tri_ref[...], g)
        bg_ref[c * chunk:(c + 1) * chunk, :] = jnp.where(lane < heads, beta, gc)


def _gdn_proj(x2d, wqkv, wz, wbg, convw, alog, dtb, batch, seq, heads, nqk, nv, ts, chunk):
    t, d = x2d.shape
    ns = seq // ts
    cw = 2 * nqk + nv
    tri = jnp.tril(jnp.ones((chunk, chunk), F32))
    row = lambda b, s: (b * ns + s, 0)
    fixed = lambda b, s: (0, 0)
    return pl.pallas_call(
        functools.partial(_gdn_proj_kernel, ts=ts, nqk=nqk, nv=nv, heads=heads, chunk=chunk),
        grid=(batch, ns),
        in_specs=[pl.BlockSpec((ts, d), row),
                  pl.BlockSpec((d, cw), fixed),
                  pl.BlockSpec((d, nv), fixed),
                  pl.BlockSpec((d, LANES), fixed),
                  pl.BlockSpec((GDN_CONV, cw), fixed),
                  pl.BlockSpec((1, LANES), fixed),
                  pl.BlockSpec((1, LANES), fixed),
                  pl.BlockSpec((chunk, chunk), fixed)],
        out_specs=[pl.BlockSpec((ts, nqk), row),
                   pl.BlockSpec((ts, nqk), row),
                   pl.BlockSpec((ts, nv), row),
                   pl.BlockSpec((ts, nv), row),
                   pl.BlockSpec((ts, LANES), row)],
        out_shape=[jax.ShapeDtypeStruct((t, nqk), BF16),
                   jax.ShapeDtypeStruct((t, nqk), BF16),
                   jax.ShapeDtypeStruct((t, nv), BF16),
                   jax.ShapeDtypeStruct((t, nv), BF16),
                   jax.ShapeDtypeStruct((t, LANES), F32)],
        scratch_shapes=[pltpu.VMEM((ts + 2 * SUBLANES, cw), F32)],
        compiler_params=_cparams("parallel", "arbitrary"),
        name="gdn_proj",
    )(x2d, wqkv, wz, wbg, convw, alog, dtb, tri)


def _gdn_chunk_kernel(q_ref, k_ref, v_ref, z_ref, bg_ref, ng_ref, o_ref, state_ref, *, tc, heads, chunk):
    @pl.when(pl.program_id(1) == 0)
    def _():
        state_ref[...] = jnp.zeros(state_ref.shape, F32)

    rows = lax.broadcasted_iota(jnp.int32, (chunk, chunk), 0)
    cols = lax.broadcasted_iota(jnp.int32, (chunk, chunk), 1)
    tril = rows >= cols
    strict = rows > cols
    eye = jnp.where(rows == cols, 1.0, 0.0).astype(F32)
    hr = range(heads)

    def body(c, carry):
        r0 = pl.multiple_of(c * chunk, chunk)
        bg = bg_ref[pl.ds(r0, chunk), :]
        hs = [slice(h * LANES, (h + 1) * LANES) for h in hr]
        beta = [bg[:, h:h + 1] for h in hr]
        gcol = [bg[:, heads + h:heads + h + 1] for h in hr]
        glast = [g[chunk - 1:chunk, :] for g in gcol]
        eg = [jnp.exp(g) for g in gcol]
        decay = []
        for h in hr:
            gmat = jnp.broadcast_to(gcol[h], (chunk, chunk))
            decay.append(jnp.where(tril, jnp.exp(jnp.where(tril, gmat - gmat.T, 0.0)), 0.0))
        qb = [q_ref[pl.ds(r0, chunk), hs[h]] for h in hr]
        kb = [k_ref[pl.ds(r0, chunk), hs[h]] for h in hr]
        kf = [k.astype(F32) for k in kb]
        kbeta = [kf[h] * beta[h] for h in hr]
        vbeta = [v_ref[pl.ds(r0, chunk), hs[h]].astype(F32) * beta[h] for h in hr]

        kk = [_dot_nt(jnp.concatenate([kbeta[h].astype(BF16), qb[h]], axis=0), kb[h]) for h in hr]
        low = [jnp.where(strict, kk[h][:chunk] * decay[h], 0.0) for h in hr]
        attn = [(kk[h][chunk:] * decay[h]).astype(BF16) for h in hr]

        inv = [eye - l for l in low]
        power = low
        span = 1
        while 2 * span < chunk:
            pb = [p.astype(BF16) for p in power]
            power = [_dot(p, p) for p in pb]
            inv = [inv[h] + _dot(inv[h].astype(BF16), power[h].astype(BF16)) for h in hr]
            span *= 2

        rhs = [jnp.concatenate([vbeta[h], kbeta[h] * eg[h]], axis=1).astype(BF16) for h in hr]
        sol = [_dot(inv[h].astype(BF16), rhs[h]) for h in hr]
        dv = vbeta[0].shape[1]

        st = [state_ref[h] for h in hr]
        lhs = [jnp.concatenate([sol[h][:, dv:], qb[h].astype(F32) * eg[h]], axis=0).astype(BF16) for h in hr]
        ws = [_dot(lhs[h], st[h].astype(BF16)) for h in hr]
        vnb = [(sol[h][:, :dv] - ws[h][:chunk]).astype(BF16) for h in hr]
        o = [ws[h][chunk:] + _dot(attn[h], vnb[h]) for h in hr]
        for h in hr:
            kdec = (kf[h] * jnp.exp(glast[h] - gcol[h])).astype(BF16)
            state_ref[h] = st[h] * jnp.exp(glast[h]) + _dot_tn(kdec, vnb[h])
        for h in hr:
            zf = z_ref[pl.ds(r0, chunk), hs[h]].astype(F32)
            on = _rms(o[h], ng_ref[...]) * (zf * _sigmoid(zf))
            o_ref[pl.ds(r0, chunk), hs[h]] = on.astype(BF16)
        return carry

    lax.fori_loop(0, tc // chunk, body, 0)


def _gdn_chunk(q, k, v, z, bg, norm_g, batch, seq, heads, tc, chunk):
    t, nqk = q.shape
    nv = v.shape[1]
    ns = seq // tc
    row = lambda b, s: (b * ns + s, 0)
    return pl.pallas_call(
        functools.partial(_gdn_chunk_kernel, tc=tc, heads=heads, chunk=chunk),
        grid=(batch, ns),
        in_specs=[pl.BlockSpec((tc, nqk), row),
                  pl.BlockSpec((tc, nqk), row),
                  pl.BlockSpec((tc, nv), row),
                  pl.BlockSpec((tc, nv), row),
                  pl.BlockSpec((tc, LANES), row),
                  pl.BlockSpec((1, LANES), lambda b, s: (0, 0))],
        out_specs=pl.BlockSpec((tc, nv), row),
        out_shape=jax.ShapeDtypeStruct((t, nv), BF16),
        scratch_shapes=[pltpu.VMEM((heads, nqk // heads, nv // heads), F32)],
        compiler_params=_cparams("parallel", "arbitrary"),
        name="gdn_chunk",
    )(q, k, v, z, bg, norm_g)


def _mla_prep_kernel(x_ref, pos_ref, invf_ref, wkva_ref, gkv_ref, wuk_ref, wuv_ref, wqa_ref, gq_ref,
                     wqb_ref, wqbs_ref, q_ref, k_ref, v_ref, *, heads, rank, nope, rope, scale):
    xb = x_ref[...].astype(BF16)
    ts = xb.shape[0]
    lane = lax.broadcasted_iota(jnp.int32, (ts, LANES), 1)
    ang = pos_ref[...] * invf_ref[...]
    is_rope = (lane >= nope) & (lane < nope + rope)
    cosm = jnp.where(lane < nope, 1.0, jnp.where(is_rope, jnp.cos(ang), 0.0))
    sinm = jnp.where(is_rope, jnp.sin(ang), 0.0)

    kv = _dot(xb, wkva_ref[...])
    ckv = _rms(kv[:, :rank], gkv_ref[...]).astype(BF16)
    krope = kv[:, rank:rank + LANES] * cosm + kv[:, rank + LANES:rank + 2 * LANES] * sinm
    v_ref[...] = _dot(ckv, wuv_ref[...]).astype(BF16)
    knope = _dot(ckv, wuk_ref[...])

    qn = _rms(_dot(xb, wqa_ref[...]), gq_ref[...]).astype(BF16)
    qa = _dot(qn, wqb_ref[...])
    qs = _dot(qn, wqbs_ref[...])
    for h in range(heads):
        hs = slice(h * LANES, (h + 1) * LANES)
        k_ref[:, hs] = (knope[:, hs] + krope).astype(BF16)
        q_ref[:, hs] = ((qa[:, hs] * cosm + qs[:, hs] * sinm) * scale).astype(BF16)


def _mla_prep(x2d, posf, invf, wkva, gkv, wuk, wuv, wqa, gq, wqb, wqbs, heads, rank, nope, rope, vdim, ts):
    t, d = x2d.shape
    qrank = wqa.shape[1]
    scale = (nope + rope) ** -0.5
    row = lambda i: (i, 0)
    fixed = lambda i: (0, 0)
    full = lambda a: pl.BlockSpec(a.shape, fixed)
    return pl.pallas_call(
        functools.partial(_mla_prep_kernel, heads=heads, rank=rank, nope=nope, rope=rope, scale=scale),
        grid=(t // ts,),
        in_specs=[pl.BlockSpec((ts, d), row), pl.BlockSpec((ts, 1), row), full(invf), full(wkva), full(gkv),
                  full(wuk), full(wuv), full(wqa), full(gq), full(wqb), full(wqbs)],
        out_specs=[pl.BlockSpec((ts, heads * LANES), row),
                   pl.BlockSpec((ts, heads * LANES), row),
                   pl.BlockSpec((ts, heads * vdim), row)],
        out_shape=[jax.ShapeDtypeStruct((t, heads * LANES), BF16),
                   jax.ShapeDtypeStruct((t, heads * LANES), BF16),
                   jax.ShapeDtypeStruct((t, heads * vdim), BF16)],
        compiler_params=_cparams("parallel"),
        name="mla_prep",
    )(x2d, posf, invf, wkva, gkv, wuk, wuv, wqa, gq, wqb, wqbs)


def _flash_kernel(q_ref, k_ref, v_ref, o_ref, *, tq, tk, vdim):
    qi = pl.program_id(2)
    hpp = LANES // vdim
    lane = lax.broadcasted_iota(jnp.int32, (tq, LANES), 1)
    qs = [q_ref[:, hh * LANES:(hh + 1) * LANES] for hh in range(hpp)]

    def chunk(c, carry, masked):
        k0 = pl.multiple_of(c * tk, tk)
        vc = v_ref[pl.ds(k0, tk), :]
        ss = [_dot_nt(qs[hh], k_ref[pl.ds(k0, tk), hh * LANES:(hh + 1) * LANES]) for hh in range(hpp)]
        if masked:
            keep = lax.broadcasted_iota(jnp.int32, (tq, tk), 1) <= lax.broadcasted_iota(jnp.int32, (tq, tk), 0)
            ss = [jnp.where(keep, s, -jnp.inf) for s in ss]
        out = []
        for hh in range(hpp):
            m, l, acc = carry[hh]
            m_new = jnp.maximum(m, jnp.max(ss[hh], -1, keepdims=True))
            p = jnp.exp(ss[hh] - m_new)
            corr = jnp.exp(m - m_new)
            l = corr * l + jnp.sum(p, -1, keepdims=True)
            acc = corr * acc + _dot(p.astype(BF16), vc)
            out.append((m_new, l, acc))
        return tuple(out)

    init = tuple((jnp.full((tq, 1), -jnp.inf, F32), jnp.zeros((tq, 1), F32), jnp.zeros((tq, LANES), F32))
                 for _ in range(hpp))
    carry = lax.fori_loop(0, qi, functools.partial(chunk, masked=False), init)
    carry = chunk(qi, carry, masked=True)
    out = jnp.zeros((tq, LANES), F32)
    for hh in range(hpp):
        m, l, acc = carry[hh]
        out = jnp.where((lane >= hh * vdim) & (lane < (hh + 1) * vdim), acc / l, out)
    o_ref[...] = out.astype(BF16)


def _flash(q, k, v, batch, seq, heads, vdim, tq, tk):
    t = q.shape[0]
    hpp = LANES // vdim
    nq = seq // tq
    assert tq == tk
    return pl.pallas_call(
        functools.partial(_flash_kernel, tq=tq, tk=tk, vdim=vdim),
        grid=(batch, heads // hpp, nq),
        in_specs=[pl.BlockSpec((tq, hpp * LANES), lambda b, h, i: (b * nq + i, h)),
                  pl.BlockSpec((seq, hpp * LANES), lambda b, h, i: (b, h)),
                  pl.BlockSpec((seq, LANES), lambda b, h, i: (b, h))],
        out_specs=pl.BlockSpec((tq, LANES), lambda b, h, i: (b * nq + i, h)),
        out_shape=jax.ShapeDtypeStruct((t, heads * vdim), BF16),
        compiler_params=_cparams("parallel", "parallel", "arbitrary"),
        name="mla_flash",
    )(q, k, v)


def _top_values(sc, n):
    vals = []
    cur = sc
    for _ in range(n):
        m = jnp.max(cur, axis=0, keepdims=True)
        vals.append(m)
        cur = jnp.where(cur == m, -jnp.inf, cur)
    return vals


def _sorting_network(n):
    pairs = []
    p = 1
    while p < n:
        k = p
        while k >= 1:
            for j in range(k % p, n - k, 2 * k):
                for i in range(min(k, n - j - k)):
                    if (i + j) // (2 * p) == (i + j + k) // (2 * p):
                        pairs.append((i + j, i + j + k))
            k //= 2
        p *= 2
    return pairs


def _top_values_sorted(sc, n):
    nslab = sc.shape[0] // SUBLANES
    cols = [sc[v * SUBLANES:(v + 1) * SUBLANES, :] for v in range(nslab)]
    for i, j in _sorting_network(nslab):
        cols[i], cols[j] = jnp.maximum(cols[i], cols[j]), jnp.minimum(cols[i], cols[j])
    vals = []
    for r in range(n):
        m = jnp.max(cols[0], axis=0, keepdims=True)
        vals.append(m)
        hit = cols[0] == m
        for k in range(min(nslab, n - 1 - r)):
            nxt = cols[k + 1] if k + 1 < nslab else -jnp.inf
            cols[k] = jnp.where(hit, nxt, cols[k])
    return vals


def _peer_route_kernel(x_ref, wq_ref, keys_ref, xb_ref, thr_ref, e1_ref, s2_ref, e2_ref, *, heads, topk):
    xb = x_ref[...].astype(BF16)
    xb_ref[...] = xb
    q = _dot(xb, wq_ref[...])
    nk = keys_ref.shape[1]
    for h in range(heads):
        s1 = _dot_nt(keys_ref[2 * h], q[:, 2 * h * nk:(2 * h + 1) * nk].astype(BF16))
        s2 = _dot_nt(keys_ref[2 * h + 1], q[:, (2 * h + 1) * nk:(2 * h + 2) * nk].astype(BF16))
        a = _top_values_sorted(s1, topk + 1)
        b = _top_values_sorted(s2, topk + 1)
        cands = [a[i] + b[j] for i in range(topk + 1) for j in range(topk + 1) if (i + 1) * (j + 1) <= topk + 1]
        best = _top_values(jnp.concatenate(cands, axis=0), topk + 1)
        zsum = None
        for r in range(topk):
            e = jnp.exp(best[r] - best[0])
            zsum = e if zsum is None else zsum + e
        thr = 0.5 * (best[topk - 1] + best[topk]) - s1
        outs = ((thr_ref, thr), (e1_ref, jnp.exp(s1 - a[0]) / zsum), (s2_ref, s2), (e2_ref, jnp.exp(s2 - b[0])))
        for ref, val in outs:
            for tg in range(val.shape[1] // LANES):
                ref[0, h, tg] = val[:, tg * LANES:(tg + 1) * LANES]


def _peer_route(x2d, wq, keys, heads, ts):
    t, d = x2d.shape
    nk = keys.shape[1]
    nt = t // ts
    shape = (nt, heads, ts // LANES, nk, LANES)
    ospec = pl.BlockSpec((1,) + shape[1:], lambda i: (i, 0, 0, 0, 0))
    return pl.pallas_call(
        functools.partial(_peer_route_kernel, heads=heads, topk=PEER_TOPK),
        grid=(nt,),
        in_specs=[pl.BlockSpec((ts, d), lambda i: (i, 0)),
                  pl.BlockSpec(wq.shape, lambda i: (0, 0)),
                  pl.BlockSpec(keys.shape, lambda i: (0, 0, 0))],
        out_specs=[pl.BlockSpec((ts, d), lambda i: (i, 0)), ospec, ospec, ospec, ospec],
        out_shape=[jax.ShapeDtypeStruct((t, d), BF16)] + [jax.ShapeDtypeStruct(shape, F32)] * 4,
        compiler_params=_cparams("parallel"),
        name="peer_route",
    )(x2d, wq, keys)


def _gelu(x):
    return 0.5 * x * (1.0 + lax.erf(x * (2.0 ** -0.5)))


GATE_I = 2
GATE_ROWS = 64


def _peer_dense_kernel(x_ref, xb_ref, u_ref, vt_ref, thr_ref, e1_ref, s2_ref, e2_ref, g_ref, b_ref, o_ref,
                       acc_ref, act_ref, *, heads, nk, eb, alpha):
    e = pl.program_id(1)
    ts = xb_ref.shape[0]
    ni = eb // nk
    ntg = ts // LANES

    @pl.when(e == 0)
    def _():
        acc_ref[...] = jnp.zeros(acc_ref.shape, F32)

    for ip in range(ni // GATE_I):
        hid = _dot_nt(u_ref[ip * GATE_I * nk:(ip + 1) * GATE_I * nk, :], xb_ref[...])
        for tg in range(ntg):
            ls = slice(tg * LANES, (tg + 1) * LANES)
            for jc in range(nk // GATE_ROWS):
                js = slice(jc * GATE_ROWS, (jc + 1) * GATE_ROWS)
                gates = [None] * GATE_I
                for h in range(heads):
                    s2 = s2_ref[0, h, tg, js, :]
                    e2 = e2_ref[0, h, tg, js, :]
                    for a in range(GATE_I):
                        il = ip * GATE_I + a
                        thr = thr_ref[0, h, tg, il:il + 1, :]
                        e1 = e1_ref[0, h, tg, il:il + 1, :]
                        term = jnp.where(s2 >= thr, e2, 0.0) * e1
                        gates[a] = term if gates[a] is None else gates[a] + term
                for a in range(GATE_I):
                    r0 = a * nk + jc * GATE_ROWS
                    act = gates[a] * _gelu(hid[r0:r0 + GATE_ROWS, ls])
                    r1 = ip * GATE_I * nk + r0
                    act_ref[r1:r1 + GATE_ROWS, ls] = act.astype(BF16)
    acc_ref[...] += _dot(vt_ref[...], act_ref[...])

    @pl.when(e == pl.num_programs(1) - 1)
    def _():
        o_ref[...] = _layer_norm(alpha * x_ref[...] + acc_ref[...].T, g_ref[...], b_ref[...])


def _peer_dense(x2d, xb, u, vt, thr, e1, s2, e2, g, b, alpha, heads, ts, eb):
    t, d = x2d.shape
    n = u.shape[0]
    nk = s2.shape[3]
    ntg = ts // LANES
    assert (eb // nk) % GATE_I == 0 and nk % GATE_ROWS == 0
    jspec = pl.BlockSpec((1, heads, ntg, nk, LANES), lambda i, e: (i, 0, 0, 0, 0))
    ispec = pl.BlockSpec((1, heads, ntg, eb // nk, LANES), lambda i, e: (i, 0, 0, e, 0))
    return pl.pallas_call(
        functools.partial(_peer_dense_kernel, heads=heads, nk=nk, eb=eb, alpha=alpha),
        grid=(t // ts, n // eb),
        in_specs=[pl.BlockSpec((ts, d), lambda i, e: (i, 0)),
                  pl.BlockSpec((ts, d), lambda i, e: (i, 0)),
                  pl.BlockSpec((eb, d), lambda i, e: (e, 0)),
                  pl.BlockSpec((d, eb), lambda i, e: (0, e)),
                  ispec, ispec, jspec, jspec,
                  pl.BlockSpec((1, d), lambda i, e: (0, 0)),
                  pl.BlockSpec((1, d), lambda i, e: (0, 0))],
        out_specs=pl.BlockSpec((ts, d), lambda i, e: (i, 0)),
        out_shape=jax.ShapeDtypeStruct((t, d), F32),
        scratch_shapes=[pltpu.VMEM((d, ts), F32), pltpu.VMEM((eb, ts), BF16)],
        compiler_params=_cparams("parallel", "arbitrary"),
        name="peer_dense",
    )(x2d, xb, u, vt, thr, e1, s2, e2, g, b)


def _peer(h2d, w_q, sub_keys, u_tab, v_tab, g, b, alpha):
    t, d = h2d.shape
    heads, _, nk, dh = sub_keys.shape
    assert nk == LANES and dh == LANES
    ts = min(512, t)
    eb = min(1024, u_tab.shape[0])
    keys = sub_keys.reshape(heads * 2, nk, dh).astype(BF16)
    xb, thr, e1, s2, e2 = _peer_route(h2d, w_q.astype(BF16), keys, heads, ts)
    return _peer_dense(h2d, xb, u_tab.astype(BF16), v_tab.astype(BF16).T, thr, e1, s2, e2,
                       g.reshape(1, d), b.reshape(1, d), alpha, heads, ts, eb)


def _gdn_layer(h2d, batch, seq, w_in, conv_w, a_log, dt_bias, norm_g, w_out, g, b, alpha):
    t, d = h2d.shape
    heads = a_log.shape[0]
    dv = norm_g.shape[0]
    nv = heads * dv
    nqk = (conv_w.shape[1] - nv) // 2
    assert nqk // heads == LANES and dv == LANES and 2 * heads <= LANES
    chunk = min(GDN_KERNEL_CHUNK, seq)
    ts = min(512, seq)
    tc = min(512, seq)
    cw = 2 * nqk + nv
    wqkv = w_in[:, :cw].astype(BF16)
    wz = w_in[:, cw:cw + nv].astype(BF16)
    wbg = jnp.pad(w_in[:, cw + nv:], ((0, 0), (0, LANES - 2 * heads))).astype(BF16)
    alog = jnp.pad(a_log, (heads, LANES - 2 * heads)).reshape(1, LANES)
    dtb = jnp.pad(dt_bias, (heads, LANES - 2 * heads)).reshape(1, LANES)
    q, k, v, z, bg = _gdn_proj(h2d, wqkv, wz, wbg, conv_w, alog, dtb, batch, seq, heads, nqk, nv, ts, chunk)
    o = _gdn_chunk(q, k, v, z, bg, norm_g.reshape(1, dv), batch, seq, heads, tc, chunk)
    return _proj_ln(o, w_out.astype(BF16), h2d, g.reshape(1, d), b.reshape(1, d), alpha, min(512, t))


def _head_pad(w, heads, width):
    r = w.shape[0]
    return jnp.pad(w.reshape(r, heads, width), ((0, 0), (0, 0), (0, LANES - width))).reshape(r, heads * LANES)


def _rope_swap(w, half):
    return jnp.concatenate([-w[..., half:], w[..., :half]], -1)


def _mla_layer(h2d, positions, batch, seq, shared_kv, kv_w_a, kv_norm_g, kv_w_uk, kv_w_uv, w_qa, q_norm_g, w_qb,
               w_out, g, b, alpha):
    t, d = h2d.shape
    rank = kv_norm_g.shape[0]
    rope = kv_w_a.shape[1] - rank
    qrank = w_qa.shape[1]
    heads = (w_qb.shape[1] - kv_w_uk.shape[1]) // rope
    nope = kv_w_uk.shape[1] // heads
    vdim = kv_w_uv.shape[1] // heads
    half = rope // 2
    assert nope + rope <= LANES and LANES % vdim == 0 and rank % LANES == 0

    inv_freq = ROPE_THETA ** (-jnp.arange(half, dtype=F32) / half)
    invf = jnp.zeros((LANES,), F32).at[nope:nope + half].set(inv_freq).at[nope + half:nope + rope].set(inv_freq)
    posf = positions.astype(F32).reshape(t, 1)

    kr = kv_w_a[:, rank:]
    place = lambda w: jnp.pad(w, ((0, 0), (nope, LANES - nope - rope)))
    wkva = jnp.concatenate([kv_w_a[:, :rank], place(kr), place(_rope_swap(kr, half))], -1).astype(BF16)
    wuk = _head_pad(kv_w_uk, heads, nope).astype(BF16)
    wq3 = w_qb.reshape(qrank, heads, nope + rope)
    wqb = _head_pad(w_qb, heads, nope + rope).astype(BF16)
    wq_sw = jnp.concatenate([jnp.zeros((qrank, heads, nope), F32), _rope_swap(wq3[..., nope:], half)], -1)
    wqbs = _head_pad(wq_sw.reshape(qrank, heads * (nope + rope)), heads, nope + rope).astype(BF16)

    ts = min(512, t)
    q, k, v = _mla_prep(h2d, posf, invf.reshape(1, LANES), wkva, kv_norm_g.reshape(1, rank), wuk,
                        kv_w_uv.astype(BF16), w_qa.astype(BF16), q_norm_g.reshape(1, qrank), wqb, wqbs,
                        heads, rank, nope, rope, vdim, ts)
    if shared_kv is None:
        shared_kv = (k, v)
    tq = min(512, seq)
    o = _flash(q, shared_kv[0], shared_kv[1], batch, seq, heads, vdim, tq, tq)
    h_new = _proj_ln(o, w_out.astype(BF16), h2d, g.reshape(1, d), b.reshape(1, d), alpha, min(512, t))
    return h_new, shared_kv


def kernel(x, positions, a_w_in, a_conv_w, a_a_log, a_dt_bias, a_norm_g, a_w_out, kv_w_a, kv_norm_g, kv_w_uk,
           kv_w_uv, b_w_qa, b_q_norm_g, b_w_qb, b_w_out, peer_w_q, peer_sub_keys, peer_u, peer_v, ln_g, ln_b):
    batch, seq, d = x.shape
    depth = ln_g.shape[0]
    n_a = a_w_in.shape[0]
    alpha = (2 * depth) ** 0.25
    h = x.reshape(batch * seq, d)
    shared_kv = None
    for layer in range(depth):
        if layer < n_a:
            i = layer
            h = _gdn_layer(h, batch, seq, a_w_in[i], a_conv_w[i], a_a_log[i], a_dt_bias[i], a_norm_g[i],
                           a_w_out[i], ln_g[layer, 0], ln_b[layer, 0], alpha)
        else:
            j = layer - n_a
            h, shared_kv = _mla_layer(h, positions, batch, seq, shared_kv, kv_w_a, kv_norm_g, kv_w_uk, kv_w_uv,
                                      b_w_qa[j], b_q_norm_g[j], b_w_qb[j], b_w_out[j],
                                      ln_g[layer, 0], ln_b[layer, 0], alpha)
        h = _peer(h, peer_w_q[layer], peer_sub_keys[layer], peer_u[layer], peer_v[layer],
                  ln_g[layer, 1], ln_b[layer, 1], alpha)
    return h.reshape(batch, seq, d)
```

```python
import functools
import math

import jax
import jax.numpy as jnp
from jax import lax
from jax.experimental import pallas as pl
from jax.experimental.pallas import tpu as pltpu

F32 = jnp.float32
BF16 = jnp.bfloat16
HIGHEST = lax.Precision.HIGHEST

LN_EPS = 1e-5
RMS_EPS = 1e-6
ROPE_THETA = 10000.0
GDN_CONV = 4
PEER_TOPK = 16
LANES = 128
SUBLANES = 8
GDN_KERNEL_CHUNK = 128
VMEM_LIMIT_BYTES = 56 * 1024 * 1024


def _cparams(*sem):
    return pltpu.CompilerParams(dimension_semantics=sem, vmem_limit_bytes=VMEM_LIMIT_BYTES)


def _dot(a, b):
    return jnp.dot(a, b, preferred_element_type=F32)


def _dot_nt(a, b):
    return lax.dot_general(a, b, (((1,), (1,)), ((), ())), preferred_element_type=F32)


def _dot_tn(a, b):
    return lax.dot_general(a, b, (((0,), (0,)), ((), ())), preferred_element_type=F32)


def _dot_hi(a, b):
    return jnp.dot(a, b, precision=HIGHEST, preferred_element_type=F32)


def _sigmoid(x):
    return 1.0 / (1.0 + jnp.exp(-x))


def _softplus(x):
    return jnp.maximum(x, 0.0) + jnp.log1p(jnp.exp(-jnp.abs(x)))


def _layer_norm(y, g, b):
    mu = jnp.mean(y, -1, keepdims=True)
    d = y - mu
    var = jnp.mean(d * d, -1, keepdims=True)
    return d * lax.rsqrt(var + LN_EPS) * g + b


def _rms(x, g):
    return x * lax.rsqrt(jnp.mean(x * x, -1, keepdims=True) + RMS_EPS) * g


def _proj_ln_kernel(a_ref, w_ref, res_ref, g_ref, b_ref, o_ref, *, alpha):
    y = alpha * res_ref[...] + _dot(a_ref[...], w_ref[...])
    o_ref[...] = _layer_norm(y, g_ref[...], b_ref[...])


def _proj_ln(a, w, res, g, b, alpha, tm):
    t, k = a.shape
    d = w.shape[1]
    return pl.pallas_call(
        functools.partial(_proj_ln_kernel, alpha=alpha),
        grid=(t // tm,),
        in_specs=[pl.BlockSpec((tm, k), lambda i: (i, 0)),
                  pl.BlockSpec((k, d), lambda i: (0, 0)),
                  pl.BlockSpec((tm, d), lambda i: (i, 0)),
                  pl.BlockSpec((1, d), lambda i: (0, 0)),
                  pl.BlockSpec((1, d), lambda i: (0, 0))],
        out_specs=pl.BlockSpec((tm, d), lambda i: (i, 0)),
        out_shape=jax.ShapeDtypeStruct((t, d), F32),
        compiler_params=_cparams("parallel"),
        name="proj_ln",
    )(a, w, res, g, b)


def _gdn_proj_kernel(x_ref, wqkv_ref, wz_ref, wbg_ref, convw_ref, alog_ref, dtb_ref, tri_ref,
                     q_ref, k_ref, v_ref, z_ref, bg_ref, buf_ref, *, ts, nqk, nv, heads, chunk):
    s = pl.program_id(1)
    halo = SUBLANES
    xb = x_ref[...].astype(BF16)

    @pl.when(s == 0)
    def _():
        buf_ref[0:halo, :] = jnp.zeros((halo, buf_ref.shape[1]), F32)

    buf_ref[halo:halo + ts, :] = _dot(xb, wqkv_ref[...])
    z_ref[...] = _dot(xb, wz_ref[...]).astype(BF16)

    dk = nqk // heads
    for grp in range((2 * nqk + nv) // LANES):
        c0 = grp * LANES
        acc = None
        for j in range(GDN_CONV):
            r0 = halo - (GDN_CONV - 1) + j
            term = buf_ref[r0:r0 + ts, c0:c0 + LANES] * convw_ref[j:j + 1, c0:c0 + LANES]
            acc = term if acc is None else acc + term
        y = acc * _sigmoid(acc)
        if c0 < 2 * nqk:
            y = y * lax.rsqrt(jnp.sum(y * y, -1, keepdims=True) + RMS_EPS)
        if c0 < nqk:
            q_ref[:, c0:c0 + LANES] = (y * dk ** -0.5).astype(BF16)
        elif c0 < 2 * nqk:
            k_ref[:, c0 - nqk:c0 - nqk + LANES] = y.astype(BF16)
        else:
            v_ref[:, c0 - 2 * nqk:c0 - 2 * nqk + LANES] = y.astype(BF16)

    buf_ref[0:halo, :] = buf_ref[ts:ts + halo, :]

    bgp = _dot(xb, wbg_ref[...])
    lane = lax.broadcasted_iota(jnp.int32, (chunk, LANES), 1)
    for c in range(ts // chunk):
        blk = bgp[c * chunk:(c + 1) * chunk, :]
        beta = _sigmoid(blk)
        g = -jnp.exp(alog_ref[...]) * _softplus(blk + dtb_ref[...])
        gc = _dot_hi(tri_ref[...], g)
        bg_ref[c * chunk:(c + 1) * chunk, :] = jnp.where(lane < heads, beta, gc)


def _gdn_proj(x2d, wqkv, wz, wbg, convw, alog, dtb, batch, seq, heads, nqk, nv, ts, chunk):
    t, d = x2d.shape
    ns = seq // ts
    cw = 2 * nqk + nv
    tri = jnp.tril(jnp.ones((chunk, chunk), F32))
    row = lambda b, s: (b * ns + s, 0)
    fixed = lambda b, s: (0, 0)
    return pl.pallas_call(
        functools.partial(_gdn_proj_kernel, ts=ts, nqk=nqk, nv=nv, heads=heads, chunk=chunk),
        grid=(batch, ns),
        in_specs=[pl.BlockSpec((ts, d), row),
                  pl.BlockSpec((d, cw), fixed),
                  pl.BlockSpec((d, nv), fixed),
                  pl.BlockSpec((d, LANES), fixed),
                  pl.BlockSpec((GDN_CONV, cw), fixed),
                  pl.BlockSpec((1, LANES), fixed),
                  pl.BlockSpec((1, LANES), fixed),
                  pl.BlockSpec((chunk, chunk), fixed)],
        out_specs=[pl.BlockSpec((ts, nqk), row),
                   pl.BlockSpec((ts, nqk), row),
                   pl.BlockSpec((ts, nv), row),
                   pl.BlockSpec((ts, nv), row),
                   pl.BlockSpec((ts, LANES), row)],
        out_shape=[jax.ShapeDtypeStruct((t, nqk), BF16),
                   jax.ShapeDtypeStruct((t, nqk), BF16),
                   jax.ShapeDtypeStruct((t, nv), BF16),
                   jax.ShapeDtypeStruct((t, nv), BF16),
                   jax.ShapeDtypeStruct((t, LANES), F32)],
        scratch_shapes=[pltpu.VMEM((ts + 2 * SUBLANES, cw), F32)],
        compiler_params=_cparams("parallel", "arbitrary"),
        name="gdn_proj",
    )(x2d, wqkv, wz, wbg, convw, alog, dtb, tri)


def _gdn_chunk_kernel(q_ref, k_ref, v_ref, z_ref, bg_ref, ng_ref, o_ref, state_ref, *, tc, heads, chunk):
    @pl.when(pl.program_id(1) == 0)
    def _():
        state_ref[...] = jnp.zeros(state_ref.shape, F32)

    rows = lax.broadcasted_iota(jnp.int32, (chunk, chunk), 0)
    cols = lax.broadcasted_iota(jnp.int32, (chunk, chunk), 1)
    tril = rows >= cols
    strict = rows > cols
    eye = jnp.where(rows == cols, 1.0, 0.0).astype(F32)
    hr = range(heads)

    def body(c, carry):
        r0 = pl.multiple_of(c * chunk, chunk)
        bg = bg_ref[pl.ds(r0, chunk), :]
        hs = [slice(h * LANES, (h + 1) * LANES) for h in hr]
        beta = [bg[:, h:h + 1] for h in hr]
        gcol = [bg[:, heads + h:heads + h + 1] for h in hr]
        glast = [g[chunk - 1:chunk, :] for g in gcol]
        eg = [jnp.exp(g) for g in gcol]
        decay = []
        for h in hr:
            gmat = jnp.broadcast_to(gcol[h], (chunk, chunk))
            decay.append(jnp.where(tril, jnp.exp(jnp.where(tril, gmat - gmat.T, 0.0)), 0.0))
        qb = [q_ref[pl.ds(r0, chunk), hs[h]] for h in hr]
        kb = [k_ref[pl.ds(r0, chunk), hs[h]] for h in hr]
        kf = [k.astype(F32) for k in kb]
        kbeta = [kf[h] * beta[h] for h in hr]
        vbeta = [v_ref[pl.ds(r0, chunk), hs[h]].astype(F32) * beta[h] for h in hr]

        kk = [_dot_nt(jnp.concatenate([kbeta[h].astype(BF16), qb[h]], axis=0), kb[h]) for h in hr]
        low = [jnp.where(strict, kk[h][:chunk] * decay[h], 0.0) for h in hr]
        attn = [(kk[h][chunk:] * decay[h]).astype(BF16) for h in hr]

        inv = [eye - l for l in low]
        power = low
        span = 1
        while 2 * span < chunk:
            pb = [p.astype(BF16) for p in power]
            power = [_dot(p, p) for p in pb]
            inv = [inv[h] + _dot(inv[h].astype(BF16), power[h].astype(BF16)) for h in hr]
            span *= 2

        rhs = [jnp.concatenate([vbeta[h], kbeta[h] * eg[h]], axis=1).astype(BF16) for h in hr]
        sol = [_dot(inv[h].astype(BF16), rhs[h]) for h in hr]
        dv = vbeta[0].shape[1]

        st = [state_ref[h] for h in hr]
        lhs = [jnp.concatenate([sol[h][:, dv:], qb[h].astype(F32) * eg[h]], axis=0).astype(BF16) for h in hr]
        ws = [_dot(lhs[h], st[h].astype(BF16)) for h in hr]
        vnb = [(sol[h][:, :dv] - ws[h][:chunk]).astype(BF16) for h in hr]
        o = [ws[h][chunk:] + _dot(attn[h], vnb[h]) for h in hr]
        for h in hr:
            kdec = (kf[h] * jnp.exp(glast[h] - gcol[h])).astype(BF16)
            state_ref[h] = st[h] * jnp.exp(glast[h]) + _dot_tn(kdec, vnb[h])
        for h in hr:
            zf = z_ref[pl.ds(r0, chunk), hs[h]].astype(F32)
            on = _rms(o[h], ng_ref[...]) * (zf * _sigmoid(zf))
            o_ref[pl.ds(r0, chunk), hs[h]] = on.astype(BF16)
        return carry

    lax.fori_loop(0, tc // chunk, body, 0)


def _gdn_chunk(q, k, v, z, bg, norm_g, batch, seq, heads, tc, chunk):
    t, nqk = q.shape
    nv = v.shape[1]
    ns = seq // tc
    row = lambda b, s: (b * ns + s, 0)
    return pl.pallas_call(
        functools.partial(_gdn_chunk_kernel, tc=tc, heads=heads, chunk=chunk),
        grid=(batch, ns),
        in_specs=[pl.BlockSpec((tc, nqk), row),
                  pl.BlockSpec((tc, nqk), row),
                  pl.BlockSpec((tc, nv), row),
                  pl.BlockSpec((tc, nv), row),
                  pl.BlockSpec((tc, LANES), row),
                  pl.BlockSpec((1, LANES), lambda b, s: (0, 0))],
        out_specs=pl.BlockSpec((tc, nv), row),
        out_shape=jax.ShapeDtypeStruct((t, nv), BF16),
        scratch_shapes=[pltpu.VMEM((heads, nqk // heads, nv // heads), F32)],
        compiler_params=_cparams("parallel", "arbitrary"),
        name="gdn_chunk",
    )(q, k, v, z, bg, norm_g)


def _mla_prep_kernel(x_ref, pos_ref, invf_ref, wkva_ref, gkv_ref, wuk_ref, wuv_ref, wqa_ref, gq_ref,
                     wqb_ref, wqbs_ref, q_ref, k_ref, v_ref, *, heads, rank, nope, rope, scale):
    xb = x_ref[...].astype(BF16)
    ts = xb.shape[0]
    lane = lax.broadcasted_iota(jnp.int32, (ts, LANES), 1)
    ang = pos_ref[...] * invf_ref[...]
    is_rope = (lane >= nope) & (lane < nope + rope)
    cosm = jnp.where(lane < nope, 1.0, jnp.where(is_rope, jnp.cos(ang), 0.0))
    sinm = jnp.where(is_rope, jnp.sin(ang), 0.0)

    kv = _dot(xb, wkva_ref[...])
    ckv = _rms(kv[:, :rank], gkv_ref[...]).astype(BF16)
    krope = kv[:, rank:rank + LANES] * cosm + kv[:, rank + LANES:rank + 2 * LANES] * sinm
    v_ref[...] = _dot(ckv, wuv_ref[...]).astype(BF16)
    knope = _dot(ckv, wuk_ref[...])

    qn = _rms(_dot(xb, wqa_ref[...]), gq_ref[...]).astype(BF16)
    qa = _dot(qn, wqb_ref[...])
    qs = _dot(qn, wqbs_ref[...])
    for h in range(heads):
        hs = slice(h * LANES, (h + 1) * LANES)
        k_ref[:, hs] = (knope[:, hs] + krope).astype(BF16)
        q_ref[:, hs] = ((qa[:, hs] * cosm + qs[:, hs] * sinm) * scale).astype(BF16)


def _mla_prep(x2d, posf, invf, wkva, gkv, wuk, wuv, wqa, gq, wqb, wqbs, heads, rank, nope, rope, vdim, ts):
    t, d = x2d.shape
    qrank = wqa.shape[1]
    scale = (nope + rope) ** -0.5
    row = lambda i: (i, 0)
    fixed = lambda i: (0, 0)
    full = lambda a: pl.BlockSpec(a.shape, fixed)
    return pl.pallas_call(
        functools.partial(_mla_prep_kernel, heads=heads, rank=rank, nope=nope, rope=rope, scale=scale),
        grid=(t // ts,),
        in_specs=[pl.BlockSpec((ts, d), row), pl.BlockSpec((ts, 1), row), full(invf), full(wkva), full(gkv),
                  full(wuk), full(wuv), full(wqa), full(gq), full(wqb), full(wqbs)],
        out_specs=[pl.BlockSpec((ts, heads * LANES), row),
                   pl.BlockSpec((ts, heads * LANES), row),
                   pl.BlockSpec((ts, heads * vdim), row)],
        out_shape=[jax.ShapeDtypeStruct((t, heads * LANES), BF16),
                   jax.ShapeDtypeStruct((t, heads * LANES), BF16),
                   jax.ShapeDtypeStruct((t, heads * vdim), BF16)],
        compiler_params=_cparams("parallel"),
        name="mla_prep",
    )(x2d, posf, invf, wkva, gkv, wuk, wuv, wqa, gq, wqb, wqbs)


def _flash_kernel(q_ref, k_ref, v_ref, o_ref, *, tq, tk, vdim):
    qi = pl.program_id(2)
    hpp = LANES // vdim
    lane = lax.broadcasted_iota(jnp.int32, (tq, LANES), 1)
    qs = [q_ref[:, hh * LANES:(hh + 1) * LANES] for hh in range(hpp)]

    def chunk(c, carry, masked):
        k0 = pl.multiple_of(c * tk, tk)
        vc = v_ref[pl.ds(k0, tk), :]
        ss = [_dot_nt(qs[hh], k_ref[pl.ds(k0, tk), hh * LANES:(hh + 1) * LANES]) for hh in range(hpp)]
        if masked:
            keep = lax.broadcasted_iota(jnp.int32, (tq, tk), 1) <= lax.broadcasted_iota(jnp.int32, (tq, tk), 0)
            ss = [jnp.where(keep, s, -jnp.inf) for s in ss]
        out = []
        for hh in range(hpp):
            m, l, acc = carry[hh]
            m_new = jnp.maximum(m, jnp.max(ss[hh], -1, keepdims=True))
            p = jnp.exp(ss[hh] - m_new)
            corr = jnp.exp(m - m_new)
            l = corr * l + jnp.sum(p, -1, keepdims=True)
            acc = corr * acc + _dot(p.astype(BF16), vc)
            out.append((m_new, l, acc))
        return tuple(out)

    init = tuple((jnp.full((tq, 1), -jnp.inf, F32), jnp.zeros((tq, 1), F32), jnp.zeros((tq, LANES), F32))
                 for _ in range(hpp))
    carry = lax.fori_loop(0, qi, functools.partial(chunk, masked=False), init)
    carry = chunk(qi, carry, masked=True)
    out = jnp.zeros((tq, LANES), F32)
    for hh in range(hpp):
        m, l, acc = carry[hh]
        out = jnp.where((lane >= hh * vdim) & (lane < (hh + 1) * vdim), acc / l, out)
    o_ref[...] = out.astype(BF16)


def _flash(q, k, v, batch, seq, heads, vdim, tq, tk):
    t = q.shape[0]
    hpp = LANES // vdim
    nq = seq // tq
    assert tq == tk
    return pl.pallas_call(
        functools.partial(_flash_kernel, tq=tq, tk=tk, vdim=vdim),
        grid=(batch, heads // hpp, nq),
        in_specs=[pl.BlockSpec((tq, hpp * LANES), lambda b, h, i: (b * nq + i, h)),
                  pl.BlockSpec((seq, hpp * LANES), lambda b, h, i: (b, h)),
                  pl.BlockSpec((seq, LANES), lambda b, h, i: (b, h))],
        out_specs=pl.BlockSpec((tq, LANES), lambda b, h, i: (b * nq + i, h)),
        out_shape=jax.ShapeDtypeStruct((t, heads * vdim), BF16),
        compiler_params=_cparams("parallel", "parallel", "arbitrary"),
        name="mla_flash",
    )(q, k, v)


def _top_values(sc, n):
    vals = []
    cur = sc
    for _ in range(n):
        m = jnp.max(cur, axis=0, keepdims=True)
        vals.append(m)
        cur = jnp.where(cur == m, -jnp.inf, cur)
    return vals


def _sorting_network(n):
    pairs = []
    p = 1
    while p < n:
        k = p
        while k >= 1:
            for j in range(k % p, n - k, 2 * k):
                for i in range(min(k, n - j - k)):
                    if (i + j) // (2 * p) == (i + j + k) // (2 * p):
                        pairs.append((i + j, i + j + k))
            k //= 2
        p *= 2
    return pairs


def _top_values_sorted(sc, n):
    nslab = sc.shape[0] // SUBLANES
    cols = [sc[v * SUBLANES:(v + 1) * SUBLANES, :] for v in range(nslab)]
    for i, j in _sorting_network(nslab):
        cols[i], cols[j] = jnp.maximum(cols[i], cols[j]), jnp.minimum(cols[i], cols[j])
    vals = []
    for r in range(n):
        m = jnp.max(cols[0], axis=0, keepdims=True)
        vals.append(m)
        hit = cols[0] == m
        for k in range(min(nslab, n - 1 - r)):
            nxt = cols[k + 1] if k + 1 < nslab else -jnp.inf
            cols[k] = jnp.where(hit, nxt, cols[k])
    return vals


def _peer_route_kernel(x_ref, wq_ref, keys_ref, xb_ref, thr_ref, e1_ref, s2_ref, e2_ref, *, heads, topk):
    xb = x_ref[...].astype(BF16)
    xb_ref[...] = xb
    q = _dot(xb, wq_ref[...])
    nk = keys_ref.shape[1]
    for h in range(heads):
        s1 = _dot_nt(keys_ref[2 * h], q[:, 2 * h * nk:(2 * h + 1) * nk].astype(BF16))
        s2 = _dot_nt(keys_ref[2 * h + 1], q[:, (2 * h + 1) * nk:(2 * h + 2) * nk].astype(BF16))
        a = _top_values_sorted(s1, topk + 1)
        b = _top_values_sorted(s2, topk + 1)
        cands = [a[i] + b[j] for i in range(topk + 1) for j in range(topk + 1) if (i + 1) * (j + 1) <= topk + 1]
        best = _top_values(jnp.concatenate(cands, axis=0), topk + 1)
        zsum = None
        for r in range(topk):
            e = jnp.exp(best[r] - best[0])
            zsum = e if zsum is None else zsum + e
        thr = 0.5 * (best[topk - 1] + best[topk]) - s1
        outs = ((thr_ref, thr), (e1_ref, jnp.exp(s1 - a[0]) / zsum), (s2_ref, s2), (e2_ref, jnp.exp(s2 - b[0])))
        for ref, val in outs:
            for tg in range(val.shape[1] // LANES):
                ref[0, h, tg] = val[:, tg * LANES:(tg + 1) * LANES]


def _peer_route(x2d, wq, keys, heads, ts):
    t, d = x2d.shape
    nk = keys.shape[1]
    nt = t // ts
    shape = (nt, heads, ts // LANES, nk, LANES)
    ospec = pl.BlockSpec((1,) + shape[1:], lambda i: (i, 0, 0, 0, 0))
    return pl.pallas_call(
        functools.partial(_peer_route_kernel, heads=heads, topk=PEER_TOPK),
        grid=(nt,),
        in_specs=[pl.BlockSpec((ts, d), lambda i: (i, 0)),
                  pl.BlockSpec(wq.shape, lambda i: (0, 0)),
                  pl.BlockSpec(keys.shape, lambda i: (0, 0, 0))],
        out_specs=[pl.BlockSpec((ts, d), lambda i: (i, 0)), ospec, ospec, ospec, ospec],
        out_shape=[jax.ShapeDtypeStruct((t, d), BF16)] + [jax.ShapeDtypeStruct(shape, F32)] * 4,
        compiler_params=_cparams("parallel"),
        name="peer_route",
    )(x2d, wq, keys)


def _gelu(x):
    return 0.5 * x * (1.0 + lax.erf(x * (2.0 ** -0.5)))


GATE_I = 2
GATE_ROWS = 64


def _peer_dense_kernel(x_ref, xb_ref, u_ref, vt_ref, thr_ref, e1_ref, s2_ref, e2_ref, g_ref, b_ref, o_ref,
                       acc_ref, act_ref, hid_ref, *, heads, nk, eb, alpha):
    e = pl.program_id(1)
    ts = xb_ref.shape[0]
    ni = eb // nk
    ntg = ts // LANES

    @pl.when(e == 0)
    def _():
        acc_ref[...] = jnp.zeros(acc_ref.shape, F32)

    hid = _dot_nt(u_ref[...], xb_ref[...])
    for il in range(ni):
        for tg in range(ntg):
            hid_ref[il, tg] = hid[il * nk:(il + 1) * nk, tg * LANES:(tg + 1) * LANES]

    def token_group(tg, carry):
        for ip in range(ni // GATE_I):
            for jc in range(nk // GATE_ROWS):
                js = slice(jc * GATE_ROWS, (jc + 1) * GATE_ROWS)
                gates = [None] * GATE_I
                for h in range(heads):
                    s2 = s2_ref[0, h, tg, js, :]
                    e2 = e2_ref[0, h, tg, js, :]
                    for a in range(GATE_I):
                        il = ip * GATE_I + a
                        thr = thr_ref[0, h, tg, il:il + 1, :]
                        e1 = e1_ref[0, h, tg, il:il + 1, :]
                        term = jnp.where(s2 >= thr, e2, 0.0) * e1
                        gates[a] = term if gates[a] is None else gates[a] + term
                for a in range(GATE_I):
                    il = ip * GATE_I + a
                    act = gates[a] * _gelu(hid_ref[il, tg, js, :])
                    act_ref[tg, il * nk + jc * GATE_ROWS:il * nk + (jc + 1) * GATE_ROWS, :] = act.astype(BF16)
        return carry

    lax.fori_loop(0, ntg, token_group, 0)
    act = jnp.concatenate([act_ref[tg] for tg in range(ntg)], axis=1)
    acc_ref[...] += _dot(vt_ref[...], act)

    @pl.when(e == pl.num_programs(1) - 1)
    def _():
        o_ref[...] = _layer_norm(alpha * x_ref[...] + acc_ref[...].T, g_ref[...], b_ref[...])


def _peer_dense(x2d, xb, u, vt, thr, e1, s2, e2, g, b, alpha, heads, ts, eb):
    t, d = x2d.shape
    n = u.shape[0]
    nk = s2.shape[3]
    ntg = ts // LANES
    assert (eb // nk) % GATE_I == 0 and nk % GATE_ROWS == 0
    jspec = pl.BlockSpec((1, heads, ntg, nk, LANES), lambda i, e: (i, 0, 0, 0, 0))
    ispec = pl.BlockSpec((1, heads, ntg, eb // nk, LANES), lambda i, e: (i, 0, 0, e, 0))
    return pl.pallas_call(
        functools.partial(_peer_dense_kernel, heads=heads, nk=nk, eb=eb, alpha=alpha),
        grid=(t // ts, n // eb),
        in_specs=[pl.BlockSpec((ts, d), lambda i, e: (i, 0)),
                  pl.BlockSpec((ts, d), lambda i, e: (i, 0)),
                  pl.BlockSpec((eb, d), lambda i, e: (e, 0)),
                  pl.BlockSpec((d, eb), lambda i, e: (0, e)),
                  ispec, ispec, jspec, jspec,
                  pl.BlockSpec((1, d), lambda i, e: (0, 0)),
                  pl.BlockSpec((1, d), lambda i, e: (0, 0))],
        out_specs=pl.BlockSpec((ts, d), lambda i, e: (i, 0)),
        out_shape=jax.ShapeDtypeStruct((t, d), F32),
        scratch_shapes=[pltpu.VMEM((d, ts), F32), pltpu.VMEM((ntg, eb, LANES), BF16),
                        pltpu.VMEM((eb // nk, ntg, nk, LANES), F32)],
        compiler_params=_cparams("parallel", "arbitrary"),
        name="peer_dense",
    )(x2d, xb, u, vt, thr, e1, s2, e2, g, b)


def _peer(h2d, w_q, sub_keys, u_tab, v_tab, g, b, alpha):
    t, d = h2d.shape
    heads, _, nk, dh = sub_keys.shape
    assert nk == LANES and dh == LANES
    ts = min(512, t)
    eb = min(2048, u_tab.shape[0])
    keys = sub_keys.reshape(heads * 2, nk, dh).astype(BF16)
    xb, thr, e1, s2, e2 = _peer_route(h2d, w_q.astype(BF16), keys, heads, ts)
    return _peer_dense(h2d, xb, u_tab.astype(BF16), v_tab.astype(BF16).T, thr, e1, s2, e2,
                       g.reshape(1, d), b.reshape(1, d), alpha, heads, ts, eb)


def _gdn_layer(h2d, batch, seq, w_in, conv_w, a_log, dt_bias, norm_g, w_out, g, b, alpha):
    t, d = h2d.shape
    heads = a_log.shape[0]
    dv = norm_g.shape[0]
    nv = heads * dv
    nqk = (conv_w.shape[1] - nv) // 2
    assert nqk // heads == LANES and dv == LANES and 2 * heads <= LANES
    chunk = min(GDN_KERNEL_CHUNK, seq)
    ts = min(512, seq)
    tc = min(512, seq)
    cw = 2 * nqk + nv
    wqkv = w_in[:, :cw].astype(BF16)
    wz = w_in[:, cw:cw + nv].astype(BF16)
    wbg = jnp.pad(w_in[:, cw + nv:], ((0, 0), (0, LANES - 2 * heads))).astype(BF16)
    alog = jnp.pad(a_log, (heads, LANES - 2 * heads)).reshape(1, LANES)
    dtb = jnp.pad(dt_bias, (heads, LANES - 2 * heads)).reshape(1, LANES)
    q, k, v, z, bg = _gdn_proj(h2d, wqkv, wz, wbg, conv_w, alog, dtb, batch, seq, heads, nqk, nv, ts, chunk)
    o = _gdn_chunk(q, k, v, z, bg, norm_g.reshape(1, dv), batch, seq, heads, tc, chunk)
    return _proj_ln(o, w_out.astype(BF16), h2d, g.reshape(1, d), b.reshape(1, d), alpha, min(512, t))


def _head_pad(w, heads, width):
    r = w.shape[0]
    return jnp.pad(w.reshape(r, heads, width), ((0, 0), (0, 0), (0, LANES - width))).reshape(r, heads * LANES)


def _rope_swap(w, half):
    return jnp.concatenate([-w[..., half:], w[..., :half]], -1)


def _mla_layer(h2d, positions, batch, seq, shared_kv, kv_w_a, kv_norm_g, kv_w_uk, kv_w_uv, w_qa, q_norm_g, w_qb,
               w_out, g, b, alpha):
    t, d = h2d.shape
    rank = kv_norm_g.shape[0]
    rope = kv_w_a.shape[1] - rank
    qrank = w_qa.shape[1]
    heads = (w_qb.shape[1] - kv_w_uk.shape[1]) // rope
    nope = kv_w_uk.shape[1] // heads
    vdim = kv_w_uv.shape[1] // heads
    half = rope // 2
    assert nope + rope <= LANES and LANES % vdim == 0 and rank % LANES == 0

    inv_freq = ROPE_THETA ** (-jnp.arange(half, dtype=F32) / half)
    invf = jnp.zeros((LANES,), F32).at[nope:nope + half].set(inv_freq).at[nope + half:nope + rope].set(inv_freq)
    posf = positions.astype(F32).reshape(t, 1)

    kr = kv_w_a[:, rank:]
    place = lambda w: jnp.pad(w, ((0, 0), (nope, LANES - nope - rope)))
    wkva = jnp.concatenate([kv_w_a[:, :rank], place(kr), place(_rope_swap(kr, half))], -1).astype(BF16)
    wuk = _head_pad(kv_w_uk, heads, nope).astype(BF16)
    wq3 = w_qb.reshape(qrank, heads, nope + rope)
    wqb = _head_pad(w_qb, heads, nope + rope).astype(BF16)
    wq_sw = jnp.concatenate([jnp.zeros((qrank, heads, nope), F32), _rope_swap(wq3[..., nope:], half)], -1)
    wqbs = _head_pad(wq_sw.reshape(qrank, heads * (nope + rope)), heads, nope + rope).astype(BF16)

    ts = min(512, t)
    q, k, v = _mla_prep(h2d, posf, invf.reshape(1, LANES), wkva, kv_norm_g.reshape(1, rank), wuk,
                        kv_w_uv.astype(BF16), w_qa.astype(BF16), q_norm_g.reshape(1, qrank), wqb, wqbs,
                        heads, rank, nope, rope, vdim, ts)
    if shared_kv is None:
        shared_kv = (k, v)
    tq = min(512, seq)
    o = _flash(q, shared_kv[0], shared_kv[1], batch, seq, heads, vdim, tq, tq)
    h_new = _proj_ln(o, w_out.astype(BF16), h2d, g.reshape(1, d), b.reshape(1, d), alpha, min(512, t))
    return h_new, shared_kv


def kernel(x, positions, a_w_in, a_conv_w, a_a_log, a_dt_bias, a_norm_g, a_w_out, kv_w_a, kv_norm_g, kv_w_uk,
           kv_w_uv, b_w_qa, b_q_norm_g, b_w_qb, b_w_out, peer_w_q, peer_sub_keys, peer_u, peer_v, ln_g, ln_b):
    batch, seq, d = x.shape
    depth = ln_g.shape[0]
    n_a = a_w_in.shape[0]
    alpha = (2 * depth) ** 0.25
    h = x.reshape(batch * seq, d)
    shared_kv = None
    for layer in range(depth):
        if layer < n_a:
            i = layer
            h = _gdn_layer(h, batch, seq, a_w_in[i], a_conv_w[i], a_a_log[i], a_dt_bias[i], a_norm_g[i],
                           a_w_out[i], ln_g[layer, 0], ln_b[layer, 0], alpha)
        else:
            j = layer - n_a
            h, shared_kv = _mla_layer(h, positions, batch, seq, shared_kv, kv_w_a, kv_norm_g, kv_w_uk, kv_w_uv,
                                      b_w_qa[j], b_q_norm_g[j], b_w_qb[j], b_w_out[j],
                                      ln_g[layer, 0], ln_b[layer, 0], alpha)
        h = _peer(h, peer_w_q[layer], peer_sub_keys[layer], peer_u[layer], peer_v[layer],
                  ln_g[layer, 1], ln_b[layer, 1], alpha)
    return h.reshape(batch, seq, d)
```

```python
import functools
import math

import jax
import jax.numpy as jnp
from jax import lax
from jax.experimental import pallas as pl
from jax.experimental.pallas import tpu as pltpu

F32 = jnp.float32
BF16 = jnp.bfloat16
HIGHEST = lax.Precision.HIGHEST

LN_EPS = 1e-5
RMS_EPS = 1e-6
ROPE_THETA = 10000.0
GDN_CONV = 4
PEER_TOPK = 16
LANES = 128
SUBLANES = 8
GDN_KERNEL_CHUNK = 128
VMEM_LIMIT_BYTES = 56 * 1024 * 1024


def _cparams(*sem):
    return pltpu.CompilerParams(dimension_semantics=sem, vmem_limit_bytes=VMEM_LIMIT_BYTES)


def _dot(a, b):
    return jnp.dot(a, b, preferred_element_type=F32)


def _dot_nt(a, b):
    return lax.dot_general(a, b, (((1,), (1,)), ((), ())), preferred_element_type=F32)


def _dot_tn(a, b):
    return lax.dot_general(a, b, (((0,), (0,)), ((), ())), preferred_element_type=F32)


def _dot_hi(a, b):
    return jnp.dot(a, b, precision=HIGHEST, preferred_element_type=F32)


def _sigmoid(x):
    return 1.0 / (1.0 + jnp.exp(-x))


def _softplus(x):
    return jnp.maximum(x, 0.0) + jnp.log1p(jnp.exp(-jnp.abs(x)))


def _layer_norm(y, g, b):
    mu = jnp.mean(y, -1, keepdims=True)
    d = y - mu
    var = jnp.mean(d * d, -1, keepdims=True)
    return d * lax.rsqrt(var + LN_EPS) * g + b


def _rms(x, g):
    return x * lax.rsqrt(jnp.mean(x * x, -1, keepdims=True) + RMS_EPS) * g


def _proj_ln_kernel(a_ref, w_ref, res_ref, g_ref, b_ref, o_ref, *, alpha):
    y = alpha * res_ref[...] + _dot(a_ref[...], w_ref[...])
    o_ref[...] = _layer_norm(y, g_ref[...], b_ref[...])


def _proj_ln(a, w, res, g, b, alpha, tm):
    t, k = a.shape
    d = w.shape[1]
    return pl.pallas_call(
        functools.partial(_proj_ln_kernel, alpha=alpha),
        grid=(t // tm,),
        in_specs=[pl.BlockSpec((tm, k), lambda i: (i, 0)),
                  pl.BlockSpec((k, d), lambda i: (0, 0)),
                  pl.BlockSpec((tm, d), lambda i: (i, 0)),
                  pl.BlockSpec((1, d), lambda i: (0, 0)),
                  pl.BlockSpec((1, d), lambda i: (0, 0))],
        out_specs=pl.BlockSpec((tm, d), lambda i: (i, 0)),
        out_shape=jax.ShapeDtypeStruct((t, d), F32),
        compiler_params=_cparams("parallel"),
        name="proj_ln",
    )(a, w, res, g, b)


def _gdn_proj_kernel(x_ref, wqkv_ref, wz_ref, wbg_ref, convw_ref, alog_ref, dtb_ref, tri_ref,
                     q_ref, k_ref, v_ref, z_ref, bg_ref, buf_ref, *, ts, nqk, nv, heads, chunk):
    s = pl.program_id(1)
    halo = SUBLANES
    xb = x_ref[...].astype(BF16)

    @pl.when(s == 0)
    def _():
        buf_ref[0:halo, :] = jnp.zeros((halo, buf_ref.shape[1]), F32)

    buf_ref[halo:halo + ts, :] = _dot(xb, wqkv_ref[...])
    z_ref[...] = _dot(xb, wz_ref[...]).astype(BF16)

    dk = nqk // heads
    for grp in range((2 * nqk + nv) // LANES):
        c0 = grp * LANES
        acc = None
        for j in range(GDN_CONV):
            r0 = halo - (GDN_CONV - 1) + j
            term = buf_ref[r0:r0 + ts, c0:c0 + LANES] * convw_ref[j:j + 1, c0:c0 + LANES]
            acc = term if acc is None else acc + term
        y = acc * _sigmoid(acc)
        if c0 < 2 * nqk:
            y = y * lax.rsqrt(jnp.sum(y * y, -1, keepdims=True) + RMS_EPS)
        if c0 < nqk:
            q_ref[:, c0:c0 + LANES] = (y * dk ** -0.5).astype(BF16)
        elif c0 < 2 * nqk:
            k_ref[:, c0 - nqk:c0 - nqk + LANES] = y.astype(BF16)
        else:
            v_ref[:, c0 - 2 * nqk:c0 - 2 * nqk + LANES] = y.astype(BF16)

    buf_ref[0:halo, :] = buf_ref[ts:ts + halo, :]

    bgp = _dot(xb, wbg_ref[...])
    lane = lax.broadcasted_iota(jnp.int32, (chunk, LANES), 1)
    for c in range(ts // chunk):
        blk = bgp[c * chunk:(c + 1) * chunk, :]
        beta = _sigmoid(blk)
        g = -jnp.exp(alog_ref[...]) * _softplus(blk + dtb_ref[...])
        gc = _dot_hi(tri_ref[...], g)
        bg_ref[c * chunk:(c + 1) * chunk, :] = jnp.where(lane < heads, beta, gc)


def _gdn_proj(x2d, wqkv, wz, wbg, convw, alog, dtb, batch, seq, heads, nqk, nv, ts, chunk):
    t, d = x2d.shape
    ns = seq // ts
    cw = 2 * nqk + nv
    tri = jnp.tril(jnp.ones((chunk, chunk), F32))
    row = lambda b, s: (b * ns + s, 0)
    fixed = lambda b, s: (0, 0)
    return pl.pallas_call(
        functools.partial(_gdn_proj_kernel, ts=ts, nqk=nqk, nv=nv, heads=heads, chunk=chunk),
        grid=(batch, ns),
        in_specs=[pl.BlockSpec((ts, d), row),
                  pl.BlockSpec((d, cw), fixed),
                  pl.BlockSpec((d, nv), fixed),
                  pl.BlockSpec((d, LANES), fixed),
                  pl.BlockSpec((GDN_CONV, cw), fixed),
                  pl.BlockSpec((1, LANES), fixed),
                  pl.BlockSpec((1, LANES), fixed),
                  pl.BlockSpec((chunk, chunk), fixed)],
        out_specs=[pl.BlockSpec((ts, nqk), row),
                   pl.BlockSpec((ts, nqk), row),
                   pl.BlockSpec((ts, nv), row),
                   pl.BlockSpec((ts, nv), row),
                   pl.BlockSpec((ts, LANES), row)],
        out_shape=[jax.ShapeDtypeStruct((t, nqk), BF16),
                   jax.ShapeDtypeStruct((t, nqk), BF16),
                   jax.ShapeDtypeStruct((t, nv), BF16),
                   jax.ShapeDtypeStruct((t, nv), BF16),
                   jax.ShapeDtypeStruct((t, LANES), F32)],
        scratch_shapes=[pltpu.VMEM((ts + 2 * SUBLANES, cw), F32)],
        compiler_params=_cparams("parallel", "arbitrary"),
        name="gdn_proj",
    )(x2d, wqkv, wz, wbg, convw, alog, dtb, tri)


def _gdn_chunk_kernel(q_ref, k_ref, v_ref, z_ref, bg_ref, ng_ref, o_ref, state_ref, *, tc, heads, chunk):
    @pl.when(pl.program_id(1) == 0)
    def _():
        state_ref[...] = jnp.zeros(state_ref.shape, F32)

    rows = lax.broadcasted_iota(jnp.int32, (chunk, chunk), 0)
    cols = lax.broadcasted_iota(jnp.int32, (chunk, chunk), 1)
    tril = rows >= cols
    strict = rows > cols
    eye = jnp.where(rows == cols, 1.0, 0.0).astype(F32)
    hr = range(heads)

    def body(c, carry):
        r0 = pl.multiple_of(c * chunk, chunk)
        bg = bg_ref[pl.ds(r0, chunk), :]
        hs = [slice(h * LANES, (h + 1) * LANES) for h in hr]
        beta = [bg[:, h:h + 1] for h in hr]
        gcol = [bg[:, heads + h:heads + h + 1] for h in hr]
        glast = [g[chunk - 1:chunk, :] for g in gcol]
        eg = [jnp.exp(g) for g in gcol]
        decay = []
        for h in hr:
            gmat = jnp.broadcast_to(gcol[h], (chunk, chunk))
            decay.append(jnp.where(tril, jnp.exp(jnp.where(tril, gmat - gmat.T, 0.0)), 0.0))
        qb = [q_ref[pl.ds(r0, chunk), hs[h]] for h in hr]
        kb = [k_ref[pl.ds(r0, chunk), hs[h]] for h in hr]
        kf = [k.astype(F32) for k in kb]
        kbeta = [kf[h] * beta[h] for h in hr]
        vbeta = [v_ref[pl.ds(r0, chunk), hs[h]].astype(F32) * beta[h] for h in hr]

        kk = [_dot_nt(jnp.concatenate([kbeta[h].astype(BF16), qb[h]], axis=0), kb[h]) for h in hr]
        low = [jnp.where(strict, kk[h][:chunk] * decay[h], 0.0) for h in hr]
        attn = [(kk[h][chunk:] * decay[h]).astype(BF16) for h in hr]

        inv = [eye - l for l in low]
        power = low
        span = 1
        while 2 * span < chunk:
            pb = [p.astype(BF16) for p in power]
            power = [_dot(p, p) for p in pb]
            inv = [inv[h] + _dot(inv[h].astype(BF16), power[h].astype(BF16)) for h in hr]
            span *= 2

        rhs = [jnp.concatenate([vbeta[h], kbeta[h] * eg[h]], axis=1).astype(BF16) for h in hr]
        sol = [_dot(inv[h].astype(BF16), rhs[h]) for h in hr]
        dv = vbeta[0].shape[1]

        st = [state_ref[h] for h in hr]
        lhs = [jnp.concatenate([sol[h][:, dv:], qb[h].astype(F32) * eg[h]], axis=0).astype(BF16) for h in hr]
        ws = [_dot(lhs[h], st[h].astype(BF16)) for h in hr]
        vnb = [(sol[h][:, :dv] - ws[h][:chunk]).astype(BF16) for h in hr]
        o = [ws[h][chunk:] + _dot(attn[h], vnb[h]) for h in hr]
        for h in hr:
            kdec = (kf[h] * jnp.exp(glast[h] - gcol[h])).astype(BF16)
            state_ref[h] = st[h] * jnp.exp(glast[h]) + _dot_tn(kdec, vnb[h])
        for h in hr:
            zf = z_ref[pl.ds(r0, chunk), hs[h]].astype(F32)
            on = _rms(o[h], ng_ref[...]) * (zf * _sigmoid(zf))
            o_ref[pl.ds(r0, chunk), hs[h]] = on.astype(BF16)
        return carry

    lax.fori_loop(0, tc // chunk, body, 0)


def _gdn_chunk(q, k, v, z, bg, norm_g, batch, seq, heads, tc, chunk):
    t, nqk = q.shape
    nv = v.shape[1]
    ns = seq // tc
    row = lambda b, s: (b * ns + s, 0)
    return pl.pallas_call(
        functools.partial(_gdn_chunk_kernel, tc=tc, heads=heads, chunk=chunk),
        grid=(batch, ns),
        in_specs=[pl.BlockSpec((tc, nqk), row),
                  pl.BlockSpec((tc, nqk), row),
                  pl.BlockSpec((tc, nv), row),
                  pl.BlockSpec((tc, nv), row),
                  pl.BlockSpec((tc, LANES), row),
                  pl.BlockSpec((1, LANES), lambda b, s: (0, 0))],
        out_specs=pl.BlockSpec((tc, nv), row),
        out_shape=jax.ShapeDtypeStruct((t, nv), BF16),
        scratch_shapes=[pltpu.VMEM((heads, nqk // heads, nv // heads), F32)],
        compiler_params=_cparams("parallel", "arbitrary"),
        name="gdn_chunk",
    )(q, k, v, z, bg, norm_g)


def _mla_prep_kernel(x_ref, pos_ref, invf_ref, wkva_ref, gkv_ref, wuk_ref, wuv_ref, wqa_ref, gq_ref,
                     wqb_ref, wqbs_ref, q_ref, k_ref, v_ref, *, heads, rank, nope, rope, scale):
    xb = x_ref[...].astype(BF16)
    ts = xb.shape[0]
    lane = lax.broadcasted_iota(jnp.int32, (ts, LANES), 1)
    ang = pos_ref[...] * invf_ref[...]
    is_rope = (lane >= nope) & (lane < nope + rope)
    cosm = jnp.where(lane < nope, 1.0, jnp.where(is_rope, jnp.cos(ang), 0.0))
    sinm = jnp.where(is_rope, jnp.sin(ang), 0.0)

    kv = _dot(xb, wkva_ref[...])
    ckv = _rms(kv[:, :rank], gkv_ref[...]).astype(BF16)
    krope = kv[:, rank:rank + LANES] * cosm + kv[:, rank + LANES:rank + 2 * LANES] * sinm
    v_ref[...] = _dot(ckv, wuv_ref[...]).astype(BF16)
    knope = _dot(ckv, wuk_ref[...])

    qn = _rms(_dot(xb, wqa_ref[...]), gq_ref[...]).astype(BF16)
    qa = _dot(qn, wqb_ref[...])
    qs = _dot(qn, wqbs_ref[...])
    for h in range(heads):
        hs = slice(h * LANES, (h + 1) * LANES)
        k_ref[:, hs] = (knope[:, hs] + krope).astype(BF16)
        q_ref[:, hs] = ((qa[:, hs] * cosm + qs[:, hs] * sinm) * scale).astype(BF16)


def _mla_prep(x2d, posf, invf, wkva, gkv, wuk, wuv, wqa, gq, wqb, wqbs, heads, rank, nope, rope, vdim, ts):
    t, d = x2d.shape
    qrank = wqa.shape[1]
    scale = (nope + rope) ** -0.5
    row = lambda i: (i, 0)
    fixed = lambda i: (0, 0)
    full = lambda a: pl.BlockSpec(a.shape, fixed)
    return pl.pallas_call(
        functools.partial(_mla_prep_kernel, heads=heads, rank=rank, nope=nope, rope=rope, scale=scale),
        grid=(t // ts,),
        in_specs=[pl.BlockSpec((ts, d), row), pl.BlockSpec((ts, 1), row), full(invf), full(wkva), full(gkv),
                  full(wuk), full(wuv), full(wqa), full(gq), full(wqb), full(wqbs)],
        out_specs=[pl.BlockSpec((ts, heads * LANES), row),
                   pl.BlockSpec((ts, heads * LANES), row),
                   pl.BlockSpec((ts, heads * vdim), row)],
        out_shape=[jax.ShapeDtypeStruct((t, heads * LANES), BF16),
                   jax.ShapeDtypeStruct((t, heads * LANES), BF16),
                   jax.ShapeDtypeStruct((t, heads * vdim), BF16)],
        compiler_params=_cparams("parallel"),
        name="mla_prep",
    )(x2d, posf, invf, wkva, gkv, wuk, wuv, wqa, gq, wqb, wqbs)


def _flash_kernel(q_ref, k_ref, v_ref, o_ref, *, tq, tk, vdim):
    qi = pl.program_id(2)
    hpp = LANES // vdim
    lane = lax.broadcasted_iota(jnp.int32, (tq, LANES), 1)
    qs = [q_ref[:, hh * LANES:(hh + 1) * LANES] for hh in range(hpp)]

    def chunk(c, carry, masked):
        k0 = pl.multiple_of(c * tk, tk)
        vc = v_ref[pl.ds(k0, tk), :]
        ss = [_dot_nt(qs[hh], k_ref[pl.ds(k0, tk), hh * LANES:(hh + 1) * LANES]) for hh in range(hpp)]
        if masked:
            keep = lax.broadcasted_iota(jnp.int32, (tq, tk), 1) <= lax.broadcasted_iota(jnp.int32, (tq, tk), 0)
            ss = [jnp.where(keep, s, -jnp.inf) for s in ss]
        out = []
        for hh in range(hpp):
            m, l, acc = carry[hh]
            m_new = jnp.maximum(m, jnp.max(ss[hh], -1, keepdims=True))
            p = jnp.exp(ss[hh] - m_new)
            corr = jnp.exp(m - m_new)
            l = corr * l + jnp.sum(p, -1, keepdims=True)
            acc = corr * acc + _dot(p.astype(BF16), vc)
            out.append((m_new, l, acc))
        return tuple(out)

    init = tuple((jnp.full((tq, 1), -jnp.inf, F32), jnp.zeros((tq, 1), F32), jnp.zeros((tq, LANES), F32))
                 for _ in range(hpp))
    carry = lax.fori_loop(0, qi, functools.partial(chunk, masked=False), init)
    carry = chunk(qi, carry, masked=True)
    out = jnp.zeros((tq, LANES), F32)
    for hh in range(hpp):
        m, l, acc = carry[hh]
        out = jnp.where((lane >= hh * vdim) & (lane < (hh + 1) * vdim), acc / l, out)
    o_ref[...] = out.astype(BF16)


def _flash(q, k, v, batch, seq, heads, vdim, tq, tk):
    t = q.shape[0]
    hpp = LANES // vdim
    nq = seq // tq
    assert tq == tk
    return pl.pallas_call(
        functools.partial(_flash_kernel, tq=tq, tk=tk, vdim=vdim),
        grid=(batch, heads // hpp, nq),
        in_specs=[pl.BlockSpec((tq, hpp * LANES), lambda b, h, i: (b * nq + i, h)),
                  pl.BlockSpec((seq, hpp * LANES), lambda b, h, i: (b, h)),
                  pl.BlockSpec((seq, LANES), lambda b, h, i: (b, h))],
        out_specs=pl.BlockSpec((tq, LANES), lambda b, h, i: (b * nq + i, h)),
        out_shape=jax.ShapeDtypeStruct((t, heads * vdim), BF16),
        compiler_params=_cparams("parallel", "parallel", "arbitrary"),
        name="mla_flash",
    )(q, k, v)


def _top_values(sc, n):
    vals = []
    cur = sc
    for _ in range(n):
        m = jnp.max(cur, axis=0, keepdims=True)
        vals.append(m)
        cur = jnp.where(cur == m, -jnp.inf, cur)
    return vals


def _sorting_network(n):
    pairs = []
    p = 1
    while p < n:
        k = p
        while k >= 1:
            for j in range(k % p, n - k, 2 * k):
                for i in range(min(k, n - j - k)):
                    if (i + j) // (2 * p) == (i + j + k) // (2 * p):
                        pairs.append((i + j, i + j + k))
            k //= 2
        p *= 2
    return pairs


def _top_values_sorted(sc, n):
    nslab = sc.shape[0] // SUBLANES
    cols = [sc[v * SUBLANES:(v + 1) * SUBLANES, :] for v in range(nslab)]
    for i, j in _sorting_network(nslab):
        cols[i], cols[j] = jnp.maximum(cols[i], cols[j]), jnp.minimum(cols[i], cols[j])
    vals = []
    for r in range(n):
        m = jnp.max(cols[0], axis=0, keepdims=True)
        vals.append(m)
        hit = cols[0] == m
        for k in range(min(nslab, n - 1 - r)):
            nxt = cols[k + 1] if k + 1 < nslab else -jnp.inf
            cols[k] = jnp.where(hit, nxt, cols[k])
    return vals


def _peer_route_kernel(x_ref, wq_ref, keys_ref, xb_ref, thr_ref, e1_ref, s2_ref, e2_ref, *, heads, topk):
    xb = x_ref[...].astype(BF16)
    xb_ref[...] = xb
    q = _dot(xb, wq_ref[...])
    nk = keys_ref.shape[1]
    for h in range(heads):
        s1 = _dot_nt(keys_ref[2 * h], q[:, 2 * h * nk:(2 * h + 1) * nk].astype(BF16))
        s2 = _dot_nt(keys_ref[2 * h + 1], q[:, (2 * h + 1) * nk:(2 * h + 2) * nk].astype(BF16))
        a = _top_values_sorted(s1, topk + 1)
        b = _top_values_sorted(s2, topk + 1)
        cands = [a[i] + b[j] for i in range(topk + 1) for j in range(topk + 1) if (i + 1) * (j + 1) <= topk + 1]
        best = _top_values(jnp.concatenate(cands, axis=0), topk + 1)
        zsum = None
        for r in range(topk):
            e = jnp.exp(best[r] - best[0])
            zsum = e if zsum is None else zsum + e
        thr = 0.5 * (best[topk - 1] + best[topk]) - s1
        outs = ((thr_ref, thr), (e1_ref, jnp.exp(s1 - a[0]) / zsum), (s2_ref, s2), (e2_ref, jnp.exp(s2 - b[0])))
        for ref, val in outs:
            for tg in range(val.shape[1] // LANES):
                ref[0, h, tg] = val[:, tg * LANES:(tg + 1) * LANES]


def _peer_route(x2d, wq, keys, heads, ts):
    t, d = x2d.shape
    nk = keys.shape[1]
    nt = t // ts
    shape = (nt, heads, ts // LANES, nk, LANES)
    ospec = pl.BlockSpec((1,) + shape[1:], lambda i: (i, 0, 0, 0, 0))
    return pl.pallas_call(
        functools.partial(_peer_route_kernel, heads=heads, topk=PEER_TOPK),
        grid=(nt,),
        in_specs=[pl.BlockSpec((ts, d), lambda i: (i, 0)),
                  pl.BlockSpec(wq.shape, lambda i: (0, 0)),
                  pl.BlockSpec(keys.shape, lambda i: (0, 0, 0))],
        out_specs=[pl.BlockSpec((ts, d), lambda i: (i, 0)), ospec, ospec, ospec, ospec],
        out_shape=[jax.ShapeDtypeStruct((t, d), BF16)] + [jax.ShapeDtypeStruct(shape, F32)] * 4,
        compiler_params=_cparams("parallel"),
        name="peer_route",
    )(x2d, wq, keys)


def _gelu(x):
    return 0.5 * x * (1.0 + lax.erf(x * (2.0 ** -0.5)))


GATE_I = 2
GATE_ROWS = 64


def _peer_dense_kernel(x_ref, xb_ref, u_ref, vt_ref, thr_ref, e1_ref, s2_ref, e2_ref, g_ref, b_ref, o_ref,
                       acc_ref, act_ref, hid_ref, *, heads, nk, eb, alpha):
    e = pl.program_id(1)
    ts = xb_ref.shape[0]
    ni = eb // nk
    ntg = ts // LANES

    @pl.when(e == 0)
    def _():
        acc_ref[...] = jnp.zeros(acc_ref.shape, F32)

    hid = _gelu(_dot_nt(u_ref[...], xb_ref[...]))
    for il in range(ni):
        for tg in range(ntg):
            hid_ref[il, tg] = hid[il * nk:(il + 1) * nk, tg * LANES:(tg + 1) * LANES]

    def token_group(tg, carry):
        for ip in range(ni // GATE_I):
            for jc in range(nk // GATE_ROWS):
                js = slice(jc * GATE_ROWS, (jc + 1) * GATE_ROWS)
                gates = [None] * GATE_I
                for h in range(heads):
                    s2 = s2_ref[0, h, tg, js, :]
                    e2 = e2_ref[0, h, tg, js, :]
                    for a in range(GATE_I):
                        il = ip * GATE_I + a
                        thr = thr_ref[0, h, tg, il:il + 1, :]
                        e1 = e1_ref[0, h, tg, il:il + 1, :]
                        term = jnp.where(s2 >= thr, e2, 0.0) * e1
                        gates[a] = term if gates[a] is None else gates[a] + term
                for a in range(GATE_I):
                    il = ip * GATE_I + a
                    act = gates[a] * hid_ref[il, tg, js, :]
                    act_ref[tg, il * nk + jc * GATE_ROWS:il * nk + (jc + 1) * GATE_ROWS, :] = act.astype(BF16)
        return carry

    lax.fori_loop(0, ntg, token_group, 0)
    act = jnp.concatenate([act_ref[tg] for tg in range(ntg)], axis=1)
    acc_ref[...] += _dot(vt_ref[...], act)

    @pl.when(e == pl.num_programs(1) - 1)
    def _():
        o_ref[...] = _layer_norm(alpha * x_ref[...] + acc_ref[...].T, g_ref[...], b_ref[...])


def _peer_dense(x2d, xb, u, vt, thr, e1, s2, e2, g, b, alpha, heads, ts, eb):
    t, d = x2d.shape
    n = u.shape[0]
    nk = s2.shape[3]
    ntg = ts // LANES
    assert (eb // nk) % GATE_I == 0 and nk % GATE_ROWS == 0
    jspec = pl.BlockSpec((1, heads, ntg, nk, LANES), lambda i, e: (i, 0, 0, 0, 0))
    ispec = pl.BlockSpec((1, heads, ntg, eb // nk, LANES), lambda i, e: (i, 0, 0, e, 0))
    return pl.pallas_call(
        functools.partial(_peer_dense_kernel, heads=heads, nk=nk, eb=eb, alpha=alpha),
        grid=(t // ts, n // eb),
        in_specs=[pl.BlockSpec((ts, d), lambda i, e: (i, 0)),
                  pl.BlockSpec((ts, d), lambda i, e: (i, 0)),
                  pl.BlockSpec((eb, d), lambda i, e: (e, 0)),
                  pl.BlockSpec((d, eb), lambda i, e: (0, e)),
                  ispec, ispec, jspec, jspec,
                  pl.BlockSpec((1, d), lambda i, e: (0, 0)),
                  pl.BlockSpec((1, d), lambda i, e: (0, 0))],
        out_specs=pl.BlockSpec((ts, d), lambda i, e: (i, 0)),
        out_shape=jax.ShapeDtypeStruct((t, d), F32),
        scratch_shapes=[pltpu.VMEM((d, ts), F32), pltpu.VMEM((ntg, eb, LANES), BF16),
                        pltpu.VMEM((eb // nk, ntg, nk, LANES), F32)],
        compiler_params=_cparams("parallel", "arbitrary"),
        name="peer_dense",
    )(x2d, xb, u, vt, thr, e1, s2, e2, g, b)


def _peer(h2d, w_q, sub_keys, u_tab, v_tab, g, b, alpha):
    t, d = h2d.shape
    heads, _, nk, dh = sub_keys.shape
    assert nk == LANES and dh == LANES
    ts = min(512, t)
    eb = min(2048, u_tab.shape[0])
    keys = sub_keys.reshape(heads * 2, nk, dh).astype(BF16)
    xb, thr, e1, s2, e2 = _peer_route(h2d, w_q.astype(BF16), keys, heads, ts)
    return _peer_dense(h2d, xb, u_tab.astype(BF16), v_tab.astype(BF16).T, thr, e1, s2, e2,
                       g.reshape(1, d), b.reshape(1, d), alpha, heads, ts, eb)


def _gdn_layer(h2d, batch, seq, w_in, conv_w, a_log, dt_bias, norm_g, w_out, g, b, alpha):
    t, d = h2d.shape
    heads = a_log.shape[0]
    dv = norm_g.shape[0]
    nv = heads * dv
    nqk = (conv_w.shape[1] - nv) // 2
    assert nqk // heads == LANES and dv == LANES and 2 * heads <= LANES
    chunk = min(GDN_KERNEL_CHUNK, seq)
    ts = min(512, seq)
    tc = min(512, seq)
    cw = 2 * nqk + nv
    wqkv = w_in[:, :cw].astype(BF16)
    wz = w_in[:, cw:cw + nv].astype(BF16)
    wbg = jnp.pad(w_in[:, cw + nv:], ((0, 0), (0, LANES - 2 * heads))).astype(BF16)
    alog = jnp.pad(a_log, (heads, LANES - 2 * heads)).reshape(1, LANES)
    dtb = jnp.pad(dt_bias, (heads, LANES - 2 * heads)).reshape(1, LANES)
    q, k, v, z, bg = _gdn_proj(h2d, wqkv, wz, wbg, conv_w, alog, dtb, batch, seq, heads, nqk, nv, ts, chunk)
    o = _gdn_chunk(q, k, v, z, bg, norm_g.reshape(1, dv), batch, seq, heads, tc, chunk)
    return _proj_ln(o, w_out.astype(BF16), h2d, g.reshape(1, d), b.reshape(1, d), alpha, min(512, t))


def _head_pad(w, heads, width):
    r = w.shape[0]
    return jnp.pad(w.reshape(r, heads, width), ((0, 0), (0, 0), (0, LANES - width))).reshape(r, heads * LANES)


def _rope_swap(w, half):
    return jnp.concatenate([-w[..., half:], w[..., :half]], -1)


def _mla_layer(h2d, positions, batch, seq, shared_kv, kv_w_a, kv_norm_g, kv_w_uk, kv_w_uv, w_qa, q_norm_g, w_qb,
               w_out, g, b, alpha):
    t, d = h2d.shape
    rank = kv_norm_g.shape[0]
    rope = kv_w_a.shape[1] - rank
    qrank = w_qa.shape[1]
    heads = (w_qb.shape[1] - kv_w_uk.shape[1]) // rope
    nope = kv_w_uk.shape[1] // heads
    vdim = kv_w_uv.shape[1] // heads
    half = rope // 2
    assert nope + rope <= LANES and LANES % vdim == 0 and rank % LANES == 0

    inv_freq = ROPE_THETA ** (-jnp.arange(half, dtype=F32) / half)
    invf = jnp.zeros((LANES,), F32).at[nope:nope + half].set(inv_freq).at[nope + half:nope + rope].set(inv_freq)
    posf = positions.astype(F32).reshape(t, 1)

    kr = kv_w_a[:, rank:]
    place = lambda w: jnp.pad(w, ((0, 0), (nope, LANES - nope - rope)))
    wkva = jnp.concatenate([kv_w_a[:, :rank], place(kr), place(_rope_swap(kr, half))], -1).astype(BF16)
    wuk = _head_pad(kv_w_uk, heads, nope).astype(BF16)
    wq3 = w_qb.reshape(qrank, heads, nope + rope)
    wqb = _head_pad(w_qb, heads, nope + rope).astype(BF16)
    wq_sw = jnp.concatenate([jnp.zeros((qrank, heads, nope), F32), _rope_swap(wq3[..., nope:], half)], -1)
    wqbs = _head_pad(wq_sw.reshape(qrank, heads * (nope + rope)), heads, nope + rope).astype(BF16)

    ts = min(512, t)
    q, k, v = _mla_prep(h2d, posf, invf.reshape(1, LANES), wkva, kv_norm_g.reshape(1, rank), wuk,
                        kv_w_uv.astype(BF16), w_qa.astype(BF16), q_norm_g.reshape(1, qrank), wqb, wqbs,
                        heads, rank, nope, rope, vdim, ts)
    if shared_kv is None:
        shared_kv = (k, v)
    tq = min(512, seq)
    o = _flash(q, shared_kv[0], shared_kv[1], batch, seq, heads, vdim, tq, tq)
    h_new = _proj_ln(o, w_out.astype(BF16), h2d, g.reshape(1, d), b.reshape(1, d), alpha, min(512, t))
    return h_new, shared_kv


def kernel(x, positions, a_w_in, a_conv_w, a_a_log, a_dt_bias, a_norm_g, a_w_out, kv_w_a, kv_norm_g, kv_w_uk,
           kv_w_uv, b_w_qa, b_q_norm_g, b_w_qb, b_w_out, peer_w_q, peer_sub_keys, peer_u, peer_v, ln_g, ln_b):
    batch, seq, d = x.shape
    depth = ln_g.shape[0]
    n_a = a_w_in.shape[0]
    alpha = (2 * depth) ** 0.25
    h = x.reshape(batch * seq, d)
    shared_kv = None
    for layer in range(depth):
        if layer < n_a:
            i = layer
            h = _gdn_layer(h, batch, seq, a_w_in[i], a_conv_w[i], a_a_log[i], a_dt_bias[i], a_norm_g[i],
                           a_w_out[i], ln_g[layer, 0], ln_b[layer, 0], alpha)
        else:
            j = layer - n_a
            h, shared_kv = _mla_layer(h, positions, batch, seq, shared_kv, kv_w_a, kv_norm_g, kv_w_uk, kv_w_uv,
                                      b_w_qa[j], b_q_norm_g[j], b_w_qb[j], b_w_out[j],
                                      ln_g[layer, 0], ln_b[layer, 0], alpha)
        h = _peer(h, peer_w_q[layer], peer_sub_keys[layer], peer_u[layer], peer_v[layer],
                  ln_g[layer, 1], ln_b[layer, 1], alpha)
    return h.reshape(batch, seq, d)
```

```python
import functools
import math

import jax
import jax.numpy as jnp
from jax import lax
from jax.experimental import pallas as pl
from jax.experimental.pallas import tpu as pltpu

F32 = jnp.float32
BF16 = jnp.bfloat16
HIGHEST = lax.Precision.HIGHEST

LN_EPS = 1e-5
RMS_EPS = 1e-6
ROPE_THETA = 10000.0
GDN_CONV = 4
PEER_TOPK = 16
LANES = 128
SUBLANES = 8
GDN_KERNEL_CHUNK = 128
VMEM_LIMIT_BYTES = 56 * 1024 * 1024


def _cparams(*sem):
    return pltpu.CompilerParams(dimension_semantics=sem, vmem_limit_bytes=VMEM_LIMIT_BYTES)


def _dot(a, b):
    return jnp.dot(a, b, preferred_element_type=F32)


def _dot_nt(a, b):
    return lax.dot_general(a, b, (((1,), (1,)), ((), ())), preferred_element_type=F32)


def _dot_tn(a, b):
    return lax.dot_general(a, b, (((0,), (0,)), ((), ())), preferred_element_type=F32)


def _dot_hi(a, b):
    return jnp.dot(a, b, precision=HIGHEST, preferred_element_type=F32)


def _sigmoid(x):
    return 1.0 / (1.0 + jnp.exp(-x))


def _softplus(x):
    return jnp.maximum(x, 0.0) + jnp.log1p(jnp.exp(-jnp.abs(x)))


def _layer_norm(y, g, b):
    mu = jnp.mean(y, -1, keepdims=True)
    d = y - mu
    var = jnp.mean(d * d, -1, keepdims=True)
    return d * lax.rsqrt(var + LN_EPS) * g + b


def _rms(x, g):
    return x * lax.rsqrt(jnp.mean(x * x, -1, keepdims=True) + RMS_EPS) * g


def _proj_ln_kernel(a_ref, w_ref, res_ref, g_ref, b_ref, o_ref, *, alpha):
    y = alpha * res_ref[...] + _dot(a_ref[...], w_ref[...])
    o_ref[...] = _layer_norm(y, g_ref[...], b_ref[...])


def _proj_ln(a, w, res, g, b, alpha, tm):
    t, k = a.shape
    d = w.shape[1]
    return pl.pallas_call(
        functools.partial(_proj_ln_kernel, alpha=alpha),
        grid=(t // tm,),
        in_specs=[pl.BlockSpec((tm, k), lambda i: (i, 0)),
                  pl.BlockSpec((k, d), lambda i: (0, 0)),
                  pl.BlockSpec((tm, d), lambda i: (i, 0)),
                  pl.BlockSpec((1, d), lambda i: (0, 0)),
                  pl.BlockSpec((1, d), lambda i: (0, 0))],
        out_specs=pl.BlockSpec((tm, d), lambda i: (i, 0)),
        out_shape=jax.ShapeDtypeStruct((t, d), F32),
        compiler_params=_cparams("parallel"),
        name="proj_ln",
    )(a, w, res, g, b)


def _gdn_proj_kernel(x_ref, wqkv_ref, wz_ref, wbg_ref, convw_ref, alog_ref, dtb_ref, tri_ref,
                     q_ref, k_ref, v_ref, z_ref, bg_ref, buf_ref, *, ts, nqk, nv, heads, chunk):
    s = pl.program_id(1)
    halo = SUBLANES
    xb = x_ref[...].astype(BF16)

    @pl.when(s == 0)
    def _():
        buf_ref[0:halo, :] = jnp.zeros((halo, buf_ref.shape[1]), F32)

    buf_ref[halo:halo + ts, :] = _dot(xb, wqkv_ref[...])
    z_ref[...] = _dot(xb, wz_ref[...]).astype(BF16)

    dk = nqk // heads
    for grp in range((2 * nqk + nv) // LANES):
        c0 = grp * LANES
        acc = None
        for j in range(GDN_CONV):
            r0 = halo - (GDN_CONV - 1) + j
            term = buf_ref[r0:r0 + ts, c0:c0 + LANES] * convw_ref[j:j + 1, c0:c0 + LANES]
            acc = term if acc is None else acc + term
        y = acc * _sigmoid(acc)
        if c0 < 2 * nqk:
            y = y * lax.rsqrt(jnp.sum(y * y, -1, keepdims=True) + RMS_EPS)
        if c0 < nqk:
            q_ref[:, c0:c0 + LANES] = (y * dk ** -0.5).astype(BF16)
        elif c0 < 2 * nqk:
            k_ref[:, c0 - nqk:c0 - nqk + LANES] = y.astype(BF16)
        else:
            v_ref[:, c0 - 2 * nqk:c0 - 2 * nqk + LANES] = y.astype(BF16)

    buf_ref[0:halo, :] = buf_ref[ts:ts + halo, :]

    bgp = _dot(xb, wbg_ref[...])
    lane = lax.broadcasted_iota(jnp.int32, (chunk, LANES), 1)
    for c in range(ts // chunk):
        blk = bgp[c * chunk:(c + 1) * chunk, :]
        beta = _sigmoid(blk)
        g = -jnp.exp(alog_ref[...]) * _softplus(blk + dtb_ref[...])
        gc = _dot_hi(tri_ref[...], g)
        bg_ref[c * chunk:(c + 1) * chunk, :] = jnp.where(lane < heads, beta, gc)


def _gdn_proj(x2d, wqkv, wz, wbg, convw, alog, dtb, batch, seq, heads, nqk, nv, ts, chunk):
    t, d = x2d.shape
    ns = seq // ts
    cw = 2 * nqk + nv
    tri = jnp.tril(jnp.ones((chunk, chunk), F32))
    row = lambda b, s: (b * ns + s, 0)
    fixed = lambda b, s: (0, 0)
    return pl.pallas_call(
        functools.partial(_gdn_proj_kernel, ts=ts, nqk=nqk, nv=nv, heads=heads, chunk=chunk),
        grid=(batch, ns),
        in_specs=[pl.BlockSpec((ts, d), row),
                  pl.BlockSpec((d, cw), fixed),
                  pl.BlockSpec((d, nv), fixed),
                  pl.BlockSpec((d, LANES), fixed),
                  pl.BlockSpec((GDN_CONV, cw), fixed),
                  pl.BlockSpec((1, LANES), fixed),
                  pl.BlockSpec((1, LANES), fixed),
                  pl.BlockSpec((chunk, chunk), fixed)],
        out_specs=[pl.BlockSpec((ts, nqk), row),
                   pl.BlockSpec((ts, nqk), row),
                   pl.BlockSpec((ts, nv), row),
                   pl.BlockSpec((ts, nv), row),
                   pl.BlockSpec((ts, LANES), row)],
        out_shape=[jax.ShapeDtypeStruct((t, nqk), BF16),
                   jax.ShapeDtypeStruct((t, nqk), BF16),
                   jax.ShapeDtypeStruct((t, nv), BF16),
                   jax.ShapeDtypeStruct((t, nv), BF16),
                   jax.ShapeDtypeStruct((t, LANES), F32)],
        scratch_shapes=[pltpu.VMEM((ts + 2 * SUBLANES, cw), F32)],
        compiler_params=_cparams("parallel", "arbitrary"),
        name="gdn_proj",
    )(x2d, wqkv, wz, wbg, convw, alog, dtb, tri)


def _gdn_chunk_kernel(q_ref, k_ref, v_ref, z_ref, bg_ref, ng_ref, o_ref, state_ref, *, tc, heads, chunk):
    @pl.when(pl.program_id(1) == 0)
    def _():
        state_ref[...] = jnp.zeros(state_ref.shape, F32)

    rows = lax.broadcasted_iota(jnp.int32, (chunk, chunk), 0)
    cols = lax.broadcasted_iota(jnp.int32, (chunk, chunk), 1)
    tril = rows >= cols
    strict = rows > cols
    eye = jnp.where(rows == cols, 1.0, 0.0).astype(F32)
    hr = range(heads)

    def body(c, carry):
        r0 = pl.multiple_of(c * chunk, chunk)
        bg = bg_ref[pl.ds(r0, chunk), :]
        hs = [slice(h * LANES, (h + 1) * LANES) for h in hr]
        beta = [bg[:, h:h + 1] for h in hr]
        gcol = [bg[:, heads + h:heads + h + 1] for h in hr]
        glast = [g[chunk - 1:chunk, :] for g in gcol]
        eg = [jnp.exp(g) for g in gcol]
        decay = []
        for h in hr:
            gmat = jnp.broadcast_to(gcol[h], (chunk, chunk))
            decay.append(jnp.where(tril, jnp.exp(jnp.where(tril, gmat - gmat.T, 0.0)), 0.0))
        qb = [q_ref[pl.ds(r0, chunk), hs[h]] for h in hr]
        kb = [k_ref[pl.ds(r0, chunk), hs[h]] for h in hr]
        kf = [k.astype(F32) for k in kb]
        kbeta = [kf[h] * beta[h] for h in hr]
        vbeta = [v_ref[pl.ds(r0, chunk), hs[h]].astype(F32) * beta[h] for h in hr]

        kk = [_dot_nt(jnp.concatenate([kbeta[h].astype(BF16), qb[h]], axis=0), kb[h]) for h in hr]
        low = [jnp.where(strict, kk[h][:chunk] * decay[h], 0.0) for h in hr]
        attn = [(kk[h][chunk:] * decay[h]).astype(BF16) for h in hr]

        inv = [eye - l for l in low]
        power = low
        span = 1
        while 2 * span < chunk:
            pb = [p.astype(BF16) for p in power]
            power = [_dot(p, p) for p in pb]
            inv = [inv[h] + _dot(inv[h].astype(BF16), power[h].astype(BF16)) for h in hr]
            span *= 2

        rhs = [jnp.concatenate([vbeta[h], kbeta[h] * eg[h]], axis=1).astype(BF16) for h in hr]
        sol = [_dot(inv[h].astype(BF16), rhs[h]) for h in hr]
        dv = vbeta[0].shape[1]

        st = [state_ref[h] for h in hr]
        lhs = [jnp.concatenate([sol[h][:, dv:], qb[h].astype(F32) * eg[h]], axis=0).astype(BF16) for h in hr]
        ws = [_dot(lhs[h], st[h].astype(BF16)) for h in hr]
        vnb = [(sol[h][:, :dv] - ws[h][:chunk]).astype(BF16) for h in hr]
        o = [ws[h][chunk:] + _dot(attn[h], vnb[h]) for h in hr]
        for h in hr:
            kdec = (kf[h] * jnp.exp(glast[h] - gcol[h])).astype(BF16)
            state_ref[h] = st[h] * jnp.exp(glast[h]) + _dot_tn(kdec, vnb[h])
        for h in hr:
            zf = z_ref[pl.ds(r0, chunk), hs[h]].astype(F32)
            on = _rms(o[h], ng_ref[...]) * (zf * _sigmoid(zf))
            o_ref[pl.ds(r0, chunk), hs[h]] = on.astype(BF16)
        return carry

    lax.fori_loop(0, tc // chunk, body, 0)


def _gdn_chunk(q, k, v, z, bg, norm_g, batch, seq, heads, tc, chunk):
    t, nqk = q.shape
    nv = v.shape[1]
    ns = seq // tc
    row = lambda b, s: (b * ns + s, 0)
    return pl.pallas_call(
        functools.partial(_gdn_chunk_kernel, tc=tc, heads=heads, chunk=chunk),
        grid=(batch, ns),
        in_specs=[pl.BlockSpec((tc, nqk), row),
                  pl.BlockSpec((tc, nqk), row),
                  pl.BlockSpec((tc, nv), row),
                  pl.BlockSpec((tc, nv), row),
                  pl.BlockSpec((tc, LANES), row),
                  pl.BlockSpec((1, LANES), lambda b, s: (0, 0))],
        out_specs=pl.BlockSpec((tc, nv), row),
        out_shape=jax.ShapeDtypeStruct((t, nv), BF16),
        scratch_shapes=[pltpu.VMEM((heads, nqk // heads, nv // heads), F32)],
        compiler_params=_cparams("parallel", "arbitrary"),
        name="gdn_chunk",
    )(q, k, v, z, bg, norm_g)


def _mla_prep_kernel(x_ref, pos_ref, invf_ref, wkva_ref, gkv_ref, wuk_ref, wuv_ref, wqa_ref, gq_ref,
                     wqb_ref, wqbs_ref, q_ref, k_ref, v_ref, *, heads, rank, nope, rope, scale):
    xb = x_ref[...].astype(BF16)
    ts = xb.shape[0]
    lane = lax.broadcasted_iota(jnp.int32, (ts, LANES), 1)
    ang = pos_ref[...] * invf_ref[...]
    is_rope = (lane >= nope) & (lane < nope + rope)
    cosm = jnp.where(lane < nope, 1.0, jnp.where(is_rope, jnp.cos(ang), 0.0))
    sinm = jnp.where(is_rope, jnp.sin(ang), 0.0)

    kv = _dot(xb, wkva_ref[...])
    ckv = _rms(kv[:, :rank], gkv_ref[...]).astype(BF16)
    krope = kv[:, rank:rank + LANES] * cosm + kv[:, rank + LANES:rank + 2 * LANES] * sinm
    v_ref[...] = _dot(ckv, wuv_ref[...]).astype(BF16)
    knope = _dot(ckv, wuk_ref[...])

    qn = _rms(_dot(xb, wqa_ref[...]), gq_ref[...]).astype(BF16)
    qa = _dot(qn, wqb_ref[...])
    qs = _dot(qn, wqbs_ref[...])
    for h in range(heads):
        hs = slice(h * LANES, (h + 1) * LANES)
        k_ref[:, hs] = (knope[:, hs] + krope).astype(BF16)
        q_ref[:, hs] = ((qa[:, hs] * cosm + qs[:, hs] * sinm) * scale).astype(BF16)


def _mla_prep(x2d, posf, invf, wkva, gkv, wuk, wuv, wqa, gq, wqb, wqbs, heads, rank, nope, rope, vdim, ts):
    t, d = x2d.shape
    qrank = wqa.shape[1]
    scale = (nope + rope) ** -0.5
    row = lambda i: (i, 0)
    fixed = lambda i: (0, 0)
    full = lambda a: pl.BlockSpec(a.shape, fixed)
    return pl.pallas_call(
        functools.partial(_mla_prep_kernel, heads=heads, rank=rank, nope=nope, rope=rope, scale=scale),
        grid=(t // ts,),
        in_specs=[pl.BlockSpec((ts, d), row), pl.BlockSpec((ts, 1), row), full(invf), full(wkva), full(gkv),
                  full(wuk), full(wuv), full(wqa), full(gq), full(wqb), full(wqbs)],
        out_specs=[pl.BlockSpec((ts, heads * LANES), row),
                   pl.BlockSpec((ts, heads * LANES), row),
                   pl.BlockSpec((ts, heads * vdim), row)],
        out_shape=[jax.ShapeDtypeStruct((t, heads * LANES), BF16),
                   jax.ShapeDtypeStruct((t, heads * LANES), BF16),
                   jax.ShapeDtypeStruct((t, heads * vdim), BF16)],
        compiler_params=_cparams("parallel"),
        name="mla_prep",
    )(x2d, posf, invf, wkva, gkv, wuk, wuv, wqa, gq, wqb, wqbs)


def _flash_kernel(q_ref, k_ref, v_ref, o_ref, *, tq, tk, vdim):
    qi = pl.program_id(2)
    hpp = LANES // vdim
    lane = lax.broadcasted_iota(jnp.int32, (tq, LANES), 1)
    qs = [q_ref[:, hh * LANES:(hh + 1) * LANES] for hh in range(hpp)]

    def chunk(c, carry, masked):
        k0 = pl.multiple_of(c * tk, tk)
        vc = v_ref[pl.ds(k0, tk), :]
        ss = [_dot_nt(qs[hh], k_ref[pl.ds(k0, tk), hh * LANES:(hh + 1) * LANES]) for hh in range(hpp)]
        if masked:
            keep = lax.broadcasted_iota(jnp.int32, (tq, tk), 1) <= lax.broadcasted_iota(jnp.int32, (tq, tk), 0)
            ss = [jnp.where(keep, s, -jnp.inf) for s in ss]
        out = []
        for hh in range(hpp):
            m, l, acc = carry[hh]
            m_new = jnp.maximum(m, jnp.max(ss[hh], -1, keepdims=True))
            p = jnp.exp(ss[hh] - m_new)
            corr = jnp.exp(m - m_new)
            l = corr * l + jnp.sum(p, -1, keepdims=True)
            acc = corr * acc + _dot(p.astype(BF16), vc)
            out.append((m_new, l, acc))
        return tuple(out)

    init = tuple((jnp.full((tq, 1), -jnp.inf, F32), jnp.zeros((tq, 1), F32), jnp.zeros((tq, LANES), F32))
                 for _ in range(hpp))
    carry = lax.fori_loop(0, qi, functools.partial(chunk, masked=False), init)
    carry = chunk(qi, carry, masked=True)
    out = jnp.zeros((tq, LANES), F32)
    for hh in range(hpp):
        m, l, acc = carry[hh]
        out = jnp.where((lane >= hh * vdim) & (lane < (hh + 1) * vdim), acc / l, out)
    o_ref[...] = out.astype(BF16)


def _flash(q, k, v, batch, seq, heads, vdim, tq, tk):
    t = q.shape[0]
    hpp = LANES // vdim
    nq = seq // tq
    assert tq == tk
    return pl.pallas_call(
        functools.partial(_flash_kernel, tq=tq, tk=tk, vdim=vdim),
        grid=(batch, heads // hpp, nq),
        in_specs=[pl.BlockSpec((tq, hpp * LANES), lambda b, h, i: (b * nq + i, h)),
                  pl.BlockSpec((seq, hpp * LANES), lambda b, h, i: (b, h)),
                  pl.BlockSpec((seq, LANES), lambda b, h, i: (b, h))],
        out_specs=pl.BlockSpec((tq, LANES), lambda b, h, i: (b * nq + i, h)),
        out_shape=jax.ShapeDtypeStruct((t, heads * vdim), BF16),
        compiler_params=_cparams("parallel", "parallel", "arbitrary"),
        name="mla_flash",
    )(q, k, v)


def _top_values(sc, n):
    vals = []
    cur = sc
    for _ in range(n):
        m = jnp.max(cur, axis=0, keepdims=True)
        vals.append(m)
        cur = jnp.where(cur == m, -jnp.inf, cur)
    return vals


def _sorting_network(n):
    pairs = []
    p = 1
    while p < n:
        k = p
        while k >= 1:
            for j in range(k % p, n - k, 2 * k):
                for i in range(min(k, n - j - k)):
                    if (i + j) // (2 * p) == (i + j + k) // (2 * p):
                        pairs.append((i + j, i + j + k))
            k //= 2
        p *= 2
    return pairs


def _top_values_sorted(sc, n):
    nslab = sc.shape[0] // SUBLANES
    cols = [sc[v * SUBLANES:(v + 1) * SUBLANES, :] for v in range(nslab)]
    for i, j in _sorting_network(nslab):
        cols[i], cols[j] = jnp.maximum(cols[i], cols[j]), jnp.minimum(cols[i], cols[j])
    vals = []
    for r in range(n):
        m = jnp.max(cols[0], axis=0, keepdims=True)
        vals.append(m)
        hit = cols[0] == m
        for k in range(min(nslab, n - 1 - r)):
            nxt = cols[k + 1] if k + 1 < nslab else -jnp.inf
            cols[k] = jnp.where(hit, nxt, cols[k])
    return vals


def _peer_route_kernel(x_ref, wq_ref, keys_ref, xb_ref, thr_ref, e1_ref, s2_ref, e2_ref, *, heads, topk):
    xb = x_ref[...].astype(BF16)
    xb_ref[...] = xb
    q = _dot(xb, wq_ref[...])
    nk = keys_ref.shape[1]
    for h in range(heads):
        s1 = _dot_nt(keys_ref[2 * h], q[:, 2 * h * nk:(2 * h + 1) * nk].astype(BF16))
        s2 = _dot_nt(keys_ref[2 * h + 1], q[:, (2 * h + 1) * nk:(2 * h + 2) * nk].astype(BF16))
        a = _top_values_sorted(s1, topk + 1)
        b = _top_values_sorted(s2, topk + 1)
        cands = [a[i] + b[j] for i in range(topk + 1) for j in range(topk + 1) if (i + 1) * (j + 1) <= topk + 1]
        best = _top_values(jnp.concatenate(cands, axis=0), topk + 1)
        zsum = None
        for r in range(topk):
            e = jnp.exp(best[r] - best[0])
            zsum = e if zsum is None else zsum + e
        thr = 0.5 * (best[topk - 1] + best[topk]) - s1
        outs = ((thr_ref, thr), (e1_ref, jnp.exp(s1 - a[0]) / zsum), (s2_ref, s2), (e2_ref, jnp.exp(s2 - b[0])))
        for ref, val in outs:
            for tg in range(val.shape[1] // LANES):
                ref[0, h, tg] = val[:, tg * LANES:(tg + 1) * LANES]


def _peer_route(x2d, wq, keys, heads, ts):
    t, d = x2d.shape
    nk = keys.shape[1]
    nt = t // ts
    shape = (nt, heads, ts // LANES, nk, LANES)
    ospec = pl.BlockSpec((1,) + shape[1:], lambda i: (i, 0, 0, 0, 0))
    return pl.pallas_call(
        functools.partial(_peer_route_kernel, heads=heads, topk=PEER_TOPK),
        grid=(nt,),
        in_specs=[pl.BlockSpec((ts, d), lambda i: (i, 0)),
                  pl.BlockSpec(wq.shape, lambda i: (0, 0)),
                  pl.BlockSpec(keys.shape, lambda i: (0, 0, 0))],
        out_specs=[pl.BlockSpec((ts, d), lambda i: (i, 0)), ospec, ospec, ospec, ospec],
        out_shape=[jax.ShapeDtypeStruct((t, d), BF16)] + [jax.ShapeDtypeStruct(shape, F32)] * 4,
        compiler_params=_cparams("parallel"),
        name="peer_route",
    )(x2d, wq, keys)


def _gelu(x):
    return 0.5 * x * (1.0 + lax.erf(x * (2.0 ** -0.5)))


GATE_I = 4
GATE_ROWS = 32


def _peer_dense_kernel(x_ref, xb_ref, u_ref, vt_ref, thr_ref, e1_ref, s2_ref, e2_ref, g_ref, b_ref, o_ref,
                       acc_ref, act_ref, hid_ref, *, heads, nk, eb, alpha):
    e = pl.program_id(1)
    ts = xb_ref.shape[0]
    ni = eb // nk
    ntg = ts // LANES

    @pl.when(e == 0)
    def _():
        acc_ref[...] = jnp.zeros(acc_ref.shape, F32)

    hid = _gelu(_dot_nt(u_ref[...], xb_ref[...]))
    for il in range(ni):
        for tg in range(ntg):
            hid_ref[il, tg] = hid[il * nk:(il + 1) * nk, tg * LANES:(tg + 1) * LANES]

    def token_group(tg, carry):
        for ip in range(ni // GATE_I):
            for jc in range(nk // GATE_ROWS):
                js = slice(jc * GATE_ROWS, (jc + 1) * GATE_ROWS)
                gates = [None] * GATE_I
                for h in range(heads):
                    s2 = s2_ref[0, h, tg, js, :]
                    e2 = e2_ref[0, h, tg, js, :]
                    for a in range(GATE_I):
                        il = ip * GATE_I + a
                        thr = thr_ref[0, h, tg, il:il + 1, :]
                        e1 = e1_ref[0, h, tg, il:il + 1, :]
                        term = jnp.where(s2 >= thr, e2, 0.0) * e1
                        gates[a] = term if gates[a] is None else gates[a] + term
                for a in range(GATE_I):
                    il = ip * GATE_I + a
                    act = gates[a] * hid_ref[il, tg, js, :]
                    act_ref[tg, il * nk + jc * GATE_ROWS:il * nk + (jc + 1) * GATE_ROWS, :] = act.astype(BF16)
        return carry

    lax.fori_loop(0, ntg, token_group, 0)
    act = jnp.concatenate([act_ref[tg] for tg in range(ntg)], axis=1)
    acc_ref[...] += _dot(vt_ref[...], act)

    @pl.when(e == pl.num_programs(1) - 1)
    def _():
        o_ref[...] = _layer_norm(alpha * x_ref[...] + acc_ref[...].T, g_ref[...], b_ref[...])


def _peer_dense(x2d, xb, u, vt, thr, e1, s2, e2, g, b, alpha, heads, ts, eb):
    t, d = x2d.shape
    n = u.shape[0]
    nk = s2.shape[3]
    ntg = ts // LANES
    assert (eb // nk) % GATE_I == 0 and nk % GATE_ROWS == 0
    jspec = pl.BlockSpec((1, heads, ntg, nk, LANES), lambda i, e: (i, 0, 0, 0, 0))
    ispec = pl.BlockSpec((1, heads, ntg, eb // nk, LANES), lambda i, e: (i, 0, 0, e, 0))
    return pl.pallas_call(
        functools.partial(_peer_dense_kernel, heads=heads, nk=nk, eb=eb, alpha=alpha),
        grid=(t // ts, n // eb),
        in_specs=[pl.BlockSpec((ts, d), lambda i, e: (i, 0)),
                  pl.BlockSpec((ts, d), lambda i, e: (i, 0)),
                  pl.BlockSpec((eb, d), lambda i, e: (e, 0)),
                  pl.BlockSpec((d, eb), lambda i, e: (0, e)),
                  ispec, ispec, jspec, jspec,
                  pl.BlockSpec((1, d), lambda i, e: (0, 0)),
                  pl.BlockSpec((1, d), lambda i, e: (0, 0))],
        out_specs=pl.BlockSpec((ts, d), lambda i, e: (i, 0)),
        out_shape=jax.ShapeDtypeStruct((t, d), F32),
        scratch_shapes=[pltpu.VMEM((d, ts), F32), pltpu.VMEM((ntg, eb, LANES), BF16),
                        pltpu.VMEM((eb // nk, ntg, nk, LANES), F32)],
        compiler_params=_cparams("parallel", "arbitrary"),
        name="peer_dense",
    )(x2d, xb, u, vt, thr, e1, s2, e2, g, b)


def _peer(h2d, w_q, sub_keys, u_tab, v_tab, g, b, alpha):
    t, d = h2d.shape
    heads, _, nk, dh = sub_keys.shape
    assert nk == LANES and dh == LANES
    ts = min(512, t)
    eb = min(2048, u_tab.shape[0])
    keys = sub_keys.reshape(heads * 2, nk, dh).astype(BF16)
    xb, thr, e1, s2, e2 = _peer_route(h2d, w_q.astype(BF16), keys, heads, ts)
    return _peer_dense(h2d, xb, u_tab.astype(BF16), v_tab.astype(BF16).T, thr, e1, s2, e2,
                       g.reshape(1, d), b.reshape(1, d), alpha, heads, ts, eb)


def _gdn_layer(h2d, batch, seq, w_in, conv_w, a_log, dt_bias, norm_g, w_out, g, b, alpha):
    t, d = h2d.shape
    heads = a_log.shape[0]
    dv = norm_g.shape[0]
    nv = heads * dv
    nqk = (conv_w.shape[1] - nv) // 2
    assert nqk // heads == LANES and dv == LANES and 2 * heads <= LANES
    chunk = min(GDN_KERNEL_CHUNK, seq)
    ts = min(512, seq)
    tc = min(512, seq)
    cw = 2 * nqk + nv
    wqkv = w_in[:, :cw].astype(BF16)
    wz = w_in[:, cw:cw + nv].astype(BF16)
    wbg = jnp.pad(w_in[:, cw + nv:], ((0, 0), (0, LANES - 2 * heads))).astype(BF16)
    alog = jnp.pad(a_log, (heads, LANES - 2 * heads)).reshape(1, LANES)
    dtb = jnp.pad(dt_bias, (heads, LANES - 2 * heads)).reshape(1, LANES)
    q, k, v, z, bg = _gdn_proj(h2d, wqkv, wz, wbg, conv_w, alog, dtb, batch, seq, heads, nqk, nv, ts, chunk)
    o = _gdn_chunk(q, k, v, z, bg, norm_g.reshape(1, dv), batch, seq, heads, tc, chunk)
    return _proj_ln(o, w_out.astype(BF16), h2d, g.reshape(1, d), b.reshape(1, d), alpha, min(512, t))


def _head_pad(w, heads, width):
    r = w.shape[0]
    return jnp.pad(w.reshape(r, heads, width), ((0, 0), (0, 0), (0, LANES - width))).reshape(r, heads * LANES)


def _rope_swap(w, half):
    return jnp.concatenate([-w[..., half:], w[..., :half]], -1)


def _mla_layer(h2d, positions, batch, seq, shared_kv, kv_w_a, kv_norm_g, kv_w_uk, kv_w_uv, w_qa, q_norm_g, w_qb,
               w_out, g, b, alpha):
    t, d = h2d.shape
    rank = kv_norm_g.shape[0]
    rope = kv_w_a.shape[1] - rank
    qrank = w_qa.shape[1]
    heads = (w_qb.shape[1] - kv_w_uk.shape[1]) // rope
    nope = kv_w_uk.shape[1] // heads
    vdim = kv_w_uv.shape[1] // heads
    half = rope // 2
    assert nope + rope <= LANES and LANES % vdim == 0 and rank % LANES == 0

    inv_freq = ROPE_THETA ** (-jnp.arange(half, dtype=F32) / half)
    invf = jnp.zeros((LANES,), F32).at[nope:nope + half].set(inv_freq).at[nope + half:nope + rope].set(inv_freq)
    posf = positions.astype(F32).reshape(t, 1)

    kr = kv_w_a[:, rank:]
    place = lambda w: jnp.pad(w, ((0, 0), (nope, LANES - nope - rope)))
    wkva = jnp.concatenate([kv_w_a[:, :rank], place(kr), place(_rope_swap(kr, half))], -1).astype(BF16)
    wuk = _head_pad(kv_w_uk, heads, nope).astype(BF16)
    wq3 = w_qb.reshape(qrank, heads, nope + rope)
    wqb = _head_pad(w_qb, heads, nope + rope).astype(BF16)
    wq_sw = jnp.concatenate([jnp.zeros((qrank, heads, nope), F32), _rope_swap(wq3[..., nope:], half)], -1)
    wqbs = _head_pad(wq_sw.reshape(qrank, heads * (nope + rope)), heads, nope + rope).astype(BF16)

    ts = min(512, t)
    q, k, v = _mla_prep(h2d, posf, invf.reshape(1, LANES), wkva, kv_norm_g.reshape(1, rank), wuk,
                        kv_w_uv.astype(BF16), w_qa.astype(BF16), q_norm_g.reshape(1, qrank), wqb, wqbs,
                        heads, rank, nope, rope, vdim, ts)
    if shared_kv is None:
        shared_kv = (k, v)
    tq = min(512, seq)
    o = _flash(q, shared_kv[0], shared_kv[1], batch, seq, heads, vdim, tq, tq)
    h_new = _proj_ln(o, w_out.astype(BF16), h2d, g.reshape(1, d), b.reshape(1, d), alpha, min(512, t))
    return h_new, shared_kv


def kernel(x, positions, a_w_in, a_conv_w, a_a_log, a_dt_bias, a_norm_g, a_w_out, kv_w_a, kv_norm_g, kv_w_uk,
           kv_w_uv, b_w_qa, b_q_norm_g, b_w_qb, b_w_out, peer_w_q, peer_sub_keys, peer_u, peer_v, ln_g, ln_b):
    batch, seq, d = x.shape
    depth = ln_g.shape[0]
    n_a = a_w_in.shape[0]
    alpha = (2 * depth) ** 0.25
    h = x.reshape(batch * seq, d)
    shared_kv = None
    for layer in range(depth):
        if layer < n_a:
            i = layer
            h = _gdn_layer(h, batch, seq, a_w_in[i], a_conv_w[i], a_a_log[i], a_dt_bias[i], a_norm_g[i],
                           a_w_out[i], ln_g[layer, 0], ln_b[layer, 0], alpha)
        else:
            j = layer - n_a
            h, shared_kv = _mla_layer(h, positions, batch, seq, shared_kv, kv_w_a, kv_norm_g, kv_w_uk, kv_w_uv,
                                      b_w_qa[j], b_q_norm_g[j], b_w_qb[j], b_w_out[j],
                                      ln_g[layer, 0], ln_b[layer, 0], alpha)
        h = _peer(h, peer_w_q[layer], peer_sub_keys[layer], peer_u[layer], peer_v[layer],
                  ln_g[layer, 1], ln_b[layer, 1], alpha)
    return h.reshape(batch, seq, d)
```

```python
import functools
import math

import jax
import jax.numpy as jnp
from jax import lax
from jax.experimental import pallas as pl
from jax.experimental.pallas import tpu as pltpu

F32 = jnp.float32
BF16 = jnp.bfloat16
HIGHEST = lax.Precision.HIGHEST

LN_EPS = 1e-5
RMS_EPS = 1e-6
ROPE_THETA = 10000.0
GDN_CONV = 4
PEER_TOPK = 16
LANES = 128
SUBLANES = 8
GDN_KERNEL_CHUNK = 128
VMEM_LIMIT_BYTES = 56 * 1024 * 1024


def _cparams(*sem):
    return pltpu.CompilerParams(dimension_semantics=sem, vmem_limit_bytes=VMEM_LIMIT_BYTES)


def _dot(a, b):
    return jnp.dot(a, b, preferred_element_type=F32)


def _dot_nt(a, b):
    return lax.dot_general(a, b, (((1,), (1,)), ((), ())), preferred_element_type=F32)


def _dot_tn(a, b):
    return lax.dot_general(a, b, (((0,), (0,)), ((), ())), preferred_element_type=F32)


def _dot_hi(a, b):
    return jnp.dot(a, b, precision=HIGHEST, preferred_element_type=F32)


def _sigmoid(x):
    return 1.0 / (1.0 + jnp.exp(-x))


def _softplus(x):
    return jnp.maximum(x, 0.0) + jnp.log1p(jnp.exp(-jnp.abs(x)))


def _layer_norm(y, g, b):
    mu = jnp.mean(y, -1, keepdims=True)
    d = y - mu
    var = jnp.mean(d * d, -1, keepdims=True)
    return d * lax.rsqrt(var + LN_EPS) * g + b


def _rms(x, g):
    return x * lax.rsqrt(jnp.mean(x * x, -1, keepdims=True) + RMS_EPS) * g


def _proj_ln_kernel(a_ref, w_ref, res_ref, g_ref, b_ref, o_ref, *, alpha):
    y = alpha * res_ref[...] + _dot(a_ref[...], w_ref[...])
    o_ref[...] = _layer_norm(y, g_ref[...], b_ref[...])


def _proj_ln(a, w, res, g, b, alpha, tm):
    t, k = a.shape
    d = w.shape[1]
    return pl.pallas_call(
        functools.partial(_proj_ln_kernel, alpha=alpha),
        grid=(t // tm,),
        in_specs=[pl.BlockSpec((tm, k), lambda i: (i, 0)),
                  pl.BlockSpec((k, d), lambda i: (0, 0)),
                  pl.BlockSpec((tm, d), lambda i: (i, 0)),
                  pl.BlockSpec((1, d), lambda i: (0, 0)),
                  pl.BlockSpec((1, d), lambda i: (0, 0))],
        out_specs=pl.BlockSpec((tm, d), lambda i: (i, 0)),
        out_shape=jax.ShapeDtypeStruct((t, d), F32),
        compiler_params=_cparams("parallel"),
        name="proj_ln",
    )(a, w, res, g, b)


def _gdn_proj_kernel(x_ref, wqkv_ref, wz_ref, wbg_ref, convw_ref, alog_ref, dtb_ref, tri_ref,
                     q_ref, k_ref, v_ref, z_ref, bg_ref, buf_ref, *, ts, nqk, nv, heads, chunk):
    s = pl.program_id(1)
    halo = SUBLANES
    xb = x_ref[...].astype(BF16)

    @pl.when(s == 0)
    def _():
        buf_ref[0:halo, :] = jnp.zeros((halo, buf_ref.shape[1]), F32)

    buf_ref[halo:halo + ts, :] = _dot(xb, wqkv_ref[...])
    z_ref[...] = _dot(xb, wz_ref[...]).astype(BF16)

    dk = nqk // heads
    for grp in range((2 * nqk + nv) // LANES):
        c0 = grp * LANES
        acc = None
        for j in range(GDN_CONV):
            r0 = halo - (GDN_CONV - 1) + j
            term = buf_ref[r0:r0 + ts, c0:c0 + LANES] * convw_ref[j:j + 1, c0:c0 + LANES]
            acc = term if acc is None else acc + term
        y = acc * _sigmoid(acc)
        if c0 < 2 * nqk:
            y = y * lax.rsqrt(jnp.sum(y * y, -1, keepdims=True) + RMS_EPS)
        if c0 < nqk:
            q_ref[:, c0:c0 + LANES] = (y * dk ** -0.5).astype(BF16)
        elif c0 < 2 * nqk:
            k_ref[:, c0 - nqk:c0 - nqk + LANES] = y.astype(BF16)
        else:
            v_ref[:, c0 - 2 * nqk:c0 - 2 * nqk + LANES] = y.astype(BF16)

    buf_ref[0:halo, :] = buf_ref[ts:ts + halo, :]

    bgp = _dot(xb, wbg_ref[...])
    lane = lax.broadcasted_iota(jnp.int32, (chunk, LANES), 1)
    for c in range(ts // chunk):
        blk = bgp[c * chunk:(c + 1) * chunk, :]
        beta = _sigmoid(blk)
        g = -jnp.exp(alog_ref[...]) * _softplus(blk + dtb_ref[...])
        gc = _dot_hi(tri_ref[...], g)
        bg_ref[c * chunk:(c + 1) * chunk, :] = jnp.where(lane < heads, beta, gc)


def _gdn_proj(x2d, wqkv, wz, wbg, convw, alog, dtb, batch, seq, heads, nqk, nv, ts, chunk):
    t, d = x2d.shape
    ns = seq // ts
    cw = 2 * nqk + nv
    tri = jnp.tril(jnp.ones((chunk, chunk), F32))
    row = lambda b, s: (b * ns + s, 0)
    fixed = lambda b, s: (0, 0)
    return pl.pallas_call(
        functools.partial(_gdn_proj_kernel, ts=ts, nqk=nqk, nv=nv, heads=heads, chunk=chunk),
        grid=(batch, ns),
        in_specs=[pl.BlockSpec((ts, d), row),
                  pl.BlockSpec((d, cw), fixed),
                  pl.BlockSpec((d, nv), fixed),
                  pl.BlockSpec((d, LANES), fixed),
                  pl.BlockSpec((GDN_CONV, cw), fixed),
                  pl.BlockSpec((1, LANES), fixed),
                  pl.BlockSpec((1, LANES), fixed),
                  pl.BlockSpec((chunk, chunk), fixed)],
        out_specs=[pl.BlockSpec((ts, nqk), row),
                   pl.BlockSpec((ts, nqk), row),
                   pl.BlockSpec((ts, nv), row),
                   pl.BlockSpec((ts, nv), row),
                   pl.BlockSpec((ts, LANES), row)],
        out_shape=[jax.ShapeDtypeStruct((t, nqk), BF16),
                   jax.ShapeDtypeStruct((t, nqk), BF16),
                   jax.ShapeDtypeStruct((t, nv), BF16),
                   jax.ShapeDtypeStruct((t, nv), BF16),
                   jax.ShapeDtypeStruct((t, LANES), F32)],
        scratch_shapes=[pltpu.VMEM((ts + 2 * SUBLANES, cw), F32)],
        compiler_params=_cparams("parallel", "arbitrary"),
        name="gdn_proj",
    )(x2d, wqkv, wz, wbg, convw, alog, dtb, tri)


def _gdn_chunk_kernel(q_ref, k_ref, v_ref, z_ref, bg_ref, ng_ref, o_ref, state_ref, *, tc, heads, chunk):
    @pl.when(pl.program_id(1) == 0)
    def _():
        state_ref[...] = jnp.zeros(state_ref.shape, F32)

    rows = lax.broadcasted_iota(jnp.int32, (chunk, chunk), 0)
    cols = lax.broadcasted_iota(jnp.int32, (chunk, chunk), 1)
    tril = rows >= cols
    strict = rows > cols
    eye = jnp.where(rows == cols, 1.0, 0.0).astype(F32)
    hr = range(heads)

    def body(c, carry):
        r0 = pl.multiple_of(c * chunk, chunk)
        bg = bg_ref[pl.ds(r0, chunk), :]
        hs = [slice(h * LANES, (h + 1) * LANES) for h in hr]
        beta = [bg[:, h:h + 1] for h in hr]
        gcol = [bg[:, heads + h:heads + h + 1] for h in hr]
        glast = [g[chunk - 1:chunk, :] for g in gcol]
        eg = [jnp.exp(g) for g in gcol]
        decay = []
        for h in hr:
            gmat = jnp.broadcast_to(gcol[h], (chunk, chunk))
            decay.append(jnp.where(tril, jnp.exp(jnp.where(tril, gmat - gmat.T, 0.0)), 0.0))
        qb = [q_ref[pl.ds(r0, chunk), hs[h]] for h in hr]
        kb = [k_ref[pl.ds(r0, chunk), hs[h]] for h in hr]
        kf = [k.astype(F32) for k in kb]
        kbeta = [kf[h] * beta[h] for h in hr]
        vbeta = [v_ref[pl.ds(r0, chunk), hs[h]].astype(F32) * beta[h] for h in hr]

        kk = [_dot_nt(jnp.concatenate([kbeta[h].astype(BF16), qb[h]], axis=0), kb[h]) for h in hr]
        low = [jnp.where(strict, kk[h][:chunk] * decay[h], 0.0) for h in hr]
        attn = [(kk[h][chunk:] * decay[h]).astype(BF16) for h in hr]

        inv = [eye - l for l in low]
        power = low
        span = 1
        while 2 * span < chunk:
            pb = [p.astype(BF16) for p in power]
            power = [_dot(p, p) for p in pb]
            inv = [inv[h] + _dot(inv[h].astype(BF16), power[h].astype(BF16)) for h in hr]
            span *= 2

        rhs = [jnp.concatenate([vbeta[h], kbeta[h] * eg[h]], axis=1).astype(BF16) for h in hr]
        sol = [_dot(inv[h].astype(BF16), rhs[h]) for h in hr]
        dv = vbeta[0].shape[1]

        st = [state_ref[h] for h in hr]
        lhs = [jnp.concatenate([sol[h][:, dv:], qb[h].astype(F32) * eg[h]], axis=0).astype(BF16) for h in hr]
        ws = [_dot(lhs[h], st[h].astype(BF16)) for h in hr]
        vnb = [(sol[h][:, :dv] - ws[h][:chunk]).astype(BF16) for h in hr]
        o = [ws[h][chunk:] + _dot(attn[h], vnb[h]) for h in hr]
        for h in hr:
            kdec = (kf[h] * jnp.exp(glast[h] - gcol[h])).astype(BF16)
            state_ref[h] = st[h] * jnp.exp(glast[h]) + _dot_tn(kdec, vnb[h])
        for h in hr:
            zf = z_ref[pl.ds(r0, chunk), hs[h]].astype(F32)
            on = _rms(o[h], ng_ref[...]) * (zf * _sigmoid(zf))
            o_ref[pl.ds(r0, chunk), hs[h]] = on.astype(BF16)
        return carry

    lax.fori_loop(0, tc // chunk, body, 0)


def _gdn_chunk(q, k, v, z, bg, norm_g, batch, seq, heads, tc, chunk):
    t, nqk = q.shape
    nv = v.shape[1]
    ns = seq // tc
    row = lambda b, s: (b * ns + s, 0)
    return pl.pallas_call(
        functools.partial(_gdn_chunk_kernel, tc=tc, heads=heads, chunk=chunk),
        grid=(batch, ns),
        in_specs=[pl.BlockSpec((tc, nqk), row),
                  pl.BlockSpec((tc, nqk), row),
                  pl.BlockSpec((tc, nv), row),
                  pl.BlockSpec((tc, nv), row),
                  pl.BlockSpec((tc, LANES), row),
                  pl.BlockSpec((1, LANES), lambda b, s: (0, 0))],
        out_specs=pl.BlockSpec((tc, nv), row),
        out_shape=jax.ShapeDtypeStruct((t, nv), BF16),
        scratch_shapes=[pltpu.VMEM((heads, nqk // heads, nv // heads), F32)],
        compiler_params=_cparams("parallel", "arbitrary"),
        name="gdn_chunk",
    )(q, k, v, z, bg, norm_g)


def _mla_prep_kernel(x_ref, pos_ref, invf_ref, wkva_ref, gkv_ref, wuk_ref, wuv_ref, wqa_ref, gq_ref,
                     wqb_ref, wqbs_ref, q_ref, k_ref, v_ref, *, heads, rank, nope, rope, scale):
    xb = x_ref[...].astype(BF16)
    ts = xb.shape[0]
    lane = lax.broadcasted_iota(jnp.int32, (ts, LANES), 1)
    ang = pos_ref[...] * invf_ref[...]
    is_rope = (lane >= nope) & (lane < nope + rope)
    cosm = jnp.where(lane < nope, 1.0, jnp.where(is_rope, jnp.cos(ang), 0.0))
    sinm = jnp.where(is_rope, jnp.sin(ang), 0.0)

    kv = _dot(xb, wkva_ref[...])
    ckv = _rms(kv[:, :rank], gkv_ref[...]).astype(BF16)
    krope = kv[:, rank:rank + LANES] * cosm + kv[:, rank + LANES:rank + 2 * LANES] * sinm
    v_ref[...] = _dot(ckv, wuv_ref[...]).astype(BF16)
    knope = _dot(ckv, wuk_ref[...])

    qn = _rms(_dot(xb, wqa_ref[...]), gq_ref[...]).astype(BF16)
    qa = _dot(qn, wqb_ref[...])
    qs = _dot(qn, wqbs_ref[...])
    for h in range(heads):
        hs = slice(h * LANES, (h + 1) * LANES)
        k_ref[:, hs] = (knope[:, hs] + krope).astype(BF16)
        q_ref[:, hs] = ((qa[:, hs] * cosm + qs[:, hs] * sinm) * scale).astype(BF16)


def _mla_prep(x2d, posf, invf, wkva, gkv, wuk, wuv, wqa, gq, wqb, wqbs, heads, rank, nope, rope, vdim, ts):
    t, d = x2d.shape
    qrank = wqa.shape[1]
    scale = (nope + rope) ** -0.5
    row = lambda i: (i, 0)
    fixed = lambda i: (0, 0)
    full = lambda a: pl.BlockSpec(a.shape, fixed)
    return pl.pallas_call(
        functools.partial(_mla_prep_kernel, heads=heads, rank=rank, nope=nope, rope=rope, scale=scale),
        grid=(t // ts,),
        in_specs=[pl.BlockSpec((ts, d), row), pl.BlockSpec((ts, 1), row), full(invf), full(wkva), full(gkv),
                  full(wuk), full(wuv), full(wqa), full(gq), full(wqb), full(wqbs)],
        out_specs=[pl.BlockSpec((ts, heads * LANES), row),
                   pl.BlockSpec((ts, heads * LANES), row),
                   pl.BlockSpec((ts, heads * vdim), row)],
        out_shape=[jax.ShapeDtypeStruct((t, heads * LANES), BF16),
                   jax.ShapeDtypeStruct((t, heads * LANES), BF16),
                   jax.ShapeDtypeStruct((t, heads * vdim), BF16)],
        compiler_params=_cparams("parallel"),
        name="mla_prep",
    )(x2d, posf, invf, wkva, gkv, wuk, wuv, wqa, gq, wqb, wqbs)


def _flash_kernel(q_ref, k_ref, v_ref, o_ref, *, tq, tk, vdim):
    qi = pl.program_id(2)
    hpp = LANES // vdim
    lane = lax.broadcasted_iota(jnp.int32, (tq, LANES), 1)
    qs = [q_ref[:, hh * LANES:(hh + 1) * LANES] for hh in range(hpp)]

    def chunk(c, carry, masked):
        k0 = pl.multiple_of(c * tk, tk)
        vc = v_ref[pl.ds(k0, tk), :]
        ss = [_dot_nt(qs[hh], k_ref[pl.ds(k0, tk), hh * LANES:(hh + 1) * LANES]) for hh in range(hpp)]
        if masked:
            keep = lax.broadcasted_iota(jnp.int32, (tq, tk), 1) <= lax.broadcasted_iota(jnp.int32, (tq, tk), 0)
            ss = [jnp.where(keep, s, -jnp.inf) for s in ss]
        out = []
        for hh in range(hpp):
            m, l, acc = carry[hh]
            m_new = jnp.maximum(m, jnp.max(ss[hh], -1, keepdims=True))
            p = jnp.exp(ss[hh] - m_new)
            corr = jnp.exp(m - m_new)
            l = corr * l + jnp.sum(p, -1, keepdims=True)
            acc = corr * acc + _dot(p.astype(BF16), vc)
            out.append((m_new, l, acc))
        return tuple(out)

    init = tuple((jnp.full((tq, 1), -jnp.inf, F32), jnp.zeros((tq, 1), F32), jnp.zeros((tq, LANES), F32))
                 for _ in range(hpp))
    carry = lax.fori_loop(0, qi, functools.partial(chunk, masked=False), init)
    carry = chunk(qi, carry, masked=True)
    out = jnp.zeros((tq, LANES), F32)
    for hh in range(hpp):
        m, l, acc = carry[hh]
        out = jnp.where((lane >= hh * vdim) & (lane < (hh + 1) * vdim), acc / l, out)
    o_ref[...] = out.astype(BF16)


def _flash(q, k, v, batch, seq, heads, vdim, tq, tk):
    t = q.shape[0]
    hpp = LANES // vdim
    nq = seq // tq
    assert tq == tk
    return pl.pallas_call(
        functools.partial(_flash_kernel, tq=tq, tk=tk, vdim=vdim),
        grid=(batch, heads // hpp, nq),
        in_specs=[pl.BlockSpec((tq, hpp * LANES), lambda b, h, i: (b * nq + i, h)),
                  pl.BlockSpec((seq, hpp * LANES), lambda b, h, i: (b, h)),
                  pl.BlockSpec((seq, LANES), lambda b, h, i: (b, h))],
        out_specs=pl.BlockSpec((tq, LANES), lambda b, h, i: (b * nq + i, h)),
        out_shape=jax.ShapeDtypeStruct((t, heads * vdim), BF16),
        compiler_params=_cparams("parallel", "parallel", "arbitrary"),
        name="mla_flash",
    )(q, k, v)


def _top_values(sc, n):
    vals = []
    cur = sc
    for _ in range(n):
        m = jnp.max(cur, axis=0, keepdims=True)
        vals.append(m)
        cur = jnp.where(cur == m, -jnp.inf, cur)
    return vals


def _sorting_network(n):
    pairs = []
    p = 1
    while p < n:
        k = p
        while k >= 1:
            for j in range(k % p, n - k, 2 * k):
                for i in range(min(k, n - j - k)):
                    if (i + j) // (2 * p) == (i + j + k) // (2 * p):
                        pairs.append((i + j, i + j + k))
            k //= 2
        p *= 2
    return pairs


def _top_values_sorted(sc, n):
    nslab = sc.shape[0] // SUBLANES
    cols = [sc[v * SUBLANES:(v + 1) * SUBLANES, :] for v in range(nslab)]
    for i, j in _sorting_network(nslab):
        cols[i], cols[j] = jnp.maximum(cols[i], cols[j]), jnp.minimum(cols[i], cols[j])
    vals = []
    for r in range(n):
        m = jnp.max(cols[0], axis=0, keepdims=True)
        vals.append(m)
        hit = cols[0] == m
        for k in range(min(nslab, n - 1 - r)):
            nxt = cols[k + 1] if k + 1 < nslab else -jnp.inf
            cols[k] = jnp.where(hit, nxt, cols[k])
    return vals


def _peer_route_kernel(x_ref, wq_ref, keys_ref, xb_ref, thr_ref, e1_ref, s2_ref, e2_ref, *, heads, topk):
    xb = x_ref[...].astype(BF16)
    xb_ref[...] = xb
    q = _dot(xb, wq_ref[...])
    nk = keys_ref.shape[1]
    for h in range(heads):
        s1 = _dot_nt(keys_ref[2 * h], q[:, 2 * h * nk:(2 * h + 1) * nk].astype(BF16))
        s2 = _dot_nt(keys_ref[2 * h + 1], q[:, (2 * h + 1) * nk:(2 * h + 2) * nk].astype(BF16))
        a = _top_values_sorted(s1, topk + 1)
        b = _top_values_sorted(s2, topk + 1)
        cands = [a[i] + b[j] for i in range(topk + 1) for j in range(topk + 1) if (i + 1) * (j + 1) <= topk + 1]
        best = _top_values(jnp.concatenate(cands, axis=0), topk + 1)
        zsum = None
        for r in range(topk):
            e = jnp.exp(best[r] - best[0])
            zsum = e if zsum is None else zsum + e
        thr = 0.5 * (best[topk - 1] + best[topk]) - s1
        outs = ((thr_ref, thr), (e1_ref, jnp.exp(s1 - a[0]) / zsum), (s2_ref, s2), (e2_ref, jnp.exp(s2 - b[0])))
        for ref, val in outs:
            for tg in range(val.shape[1] // LANES):
                ref[0, h, tg] = val[:, tg * LANES:(tg + 1) * LANES]


def _peer_route(x2d, wq, keys, heads, ts):
    t, d = x2d.shape
    nk = keys.shape[1]
    nt = t // ts
    shape = (nt, heads, ts // LANES, nk, LANES)
    ospec = pl.BlockSpec((1,) + shape[1:], lambda i: (i, 0, 0, 0, 0))
    return pl.pallas_call(
        functools.partial(_peer_route_kernel, heads=heads, topk=PEER_TOPK),
        grid=(nt,),
        in_specs=[pl.BlockSpec((ts, d), lambda i: (i, 0)),
                  pl.BlockSpec(wq.shape, lambda i: (0, 0)),
                  pl.BlockSpec(keys.shape, lambda i: (0, 0, 0))],
        out_specs=[pl.BlockSpec((ts, d), lambda i: (i, 0)), ospec, ospec, ospec, ospec],
        out_shape=[jax.ShapeDtypeStruct((t, d), BF16)] + [jax.ShapeDtypeStruct(shape, F32)] * 4,
        compiler_params=_cparams("parallel"),
        name="peer_route",
    )(x2d, wq, keys)


def _gelu(x):
    return 0.5 * x * (1.0 + lax.erf(x * (2.0 ** -0.5)))


GATE_I = 2
GATE_ROWS = 64
HID_ROWS = 256


def _peer_dense_kernel(x_ref, xb_ref, u_ref, vt_ref, thr_ref, e1_ref, s2_ref, e2_ref, g_ref, b_ref, o_ref,
                       acc_ref, act_ref, hid_ref, *, heads, nk, eb, alpha):
    e = pl.program_id(1)
    ts = xb_ref.shape[0]
    ni = eb // nk
    ntg = ts // LANES

    @pl.when(e == 0)
    def _():
        acc_ref[...] = jnp.zeros(acc_ref.shape, F32)

    ipc = HID_ROWS // nk
    for c in range(ni // ipc):
        hid = _gelu(_dot_nt(u_ref[c * HID_ROWS:(c + 1) * HID_ROWS, :], xb_ref[...]))
        for a in range(ipc):
            for tg in range(ntg):
                hid_ref[c * ipc + a, tg] = hid[a * nk:(a + 1) * nk, tg * LANES:(tg + 1) * LANES]

    def token_group(tg, carry):
        for ip in range(ni // GATE_I):
            for jc in range(nk // GATE_ROWS):
                js = slice(jc * GATE_ROWS, (jc + 1) * GATE_ROWS)
                gates = [None] * GATE_I
                for h in range(heads):
                    s2 = s2_ref[0, h, tg, js, :]
                    e2 = e2_ref[0, h, tg, js, :]
                    for a in range(GATE_I):
                        il = ip * GATE_I + a
                        thr = thr_ref[0, h, tg, il:il + 1, :]
                        e1 = e1_ref[0, h, tg, il:il + 1, :]
                        term = jnp.where(s2 >= thr, e2, 0.0) * e1
                        gates[a] = term if gates[a] is None else gates[a] + term
                for a in range(GATE_I):
                    il = ip * GATE_I + a
                    act = gates[a] * hid_ref[il, tg, js, :]
                    act_ref[tg, il * nk + jc * GATE_ROWS:il * nk + (jc + 1) * GATE_ROWS, :] = act.astype(BF16)
        return carry

    lax.fori_loop(0, ntg, token_group, 0)
    act = jnp.concatenate([act_ref[tg] for tg in range(ntg)], axis=1)
    acc_ref[...] += _dot(vt_ref[...], act)

    @pl.when(e == pl.num_programs(1) - 1)
    def _():
        o_ref[...] = _layer_norm(alpha * x_ref[...] + acc_ref[...].T, g_ref[...], b_ref[...])


def _peer_dense(x2d, xb, u, vt, thr, e1, s2, e2, g, b, alpha, heads, ts, eb):
    t, d = x2d.shape
    n = u.shape[0]
    nk = s2.shape[3]
    ntg = ts // LANES
    assert (eb // nk) % GATE_I == 0 and nk % GATE_ROWS == 0
    jspec = pl.BlockSpec((1, heads, ntg, nk, LANES), lambda i, e: (i, 0, 0, 0, 0))
    ispec = pl.BlockSpec((1, heads, ntg, eb // nk, LANES), lambda i, e: (i, 0, 0, e, 0))
    return pl.pallas_call(
        functools.partial(_peer_dense_kernel, heads=heads, nk=nk, eb=eb, alpha=alpha),
        grid=(t // ts, n // eb),
        in_specs=[pl.BlockSpec((ts, d), lambda i, e: (i, 0)),
                  pl.BlockSpec((ts, d), lambda i, e: (i, 0)),
                  pl.BlockSpec((eb, d), lambda i, e: (e, 0)),
                  pl.BlockSpec((d, eb), lambda i, e: (0, e)),
                  ispec, ispec, jspec, jspec,
                  pl.BlockSpec((1, d), lambda i, e: (0, 0)),
                  pl.BlockSpec((1, d), lambda i, e: (0, 0))],
        out_specs=pl.BlockSpec((ts, d), lambda i, e: (i, 0)),
        out_shape=jax.ShapeDtypeStruct((t, d), F32),
        scratch_shapes=[pltpu.VMEM((d, ts), F32), pltpu.VMEM((ntg, eb, LANES), BF16),
                        pltpu.VMEM((eb // nk, ntg, nk, LANES), F32)],
        compiler_params=_cparams("parallel", "arbitrary"),
        name="peer_dense",
    )(x2d, xb, u, vt, thr, e1, s2, e2, g, b)


def _peer(h2d, w_q, sub_keys, u_tab, v_tab, g, b, alpha):
    t, d = h2d.shape
    heads, _, nk, dh = sub_keys.shape
    assert nk == LANES and dh == LANES
    ts = min(512, t)
    eb = min(2048, u_tab.shape[0])
    keys = sub_keys.reshape(heads * 2, nk, dh).astype(BF16)
    xb, thr, e1, s2, e2 = _peer_route(h2d, w_q.astype(BF16), keys, heads, ts)
    return _peer_dense(h2d, xb, u_tab.astype(BF16), v_tab.astype(BF16).T, thr, e1, s2, e2,
                       g.reshape(1, d), b.reshape(1, d), alpha, heads, ts, eb)


def _gdn_layer(h2d, batch, seq, w_in, conv_w, a_log, dt_bias, norm_g, w_out, g, b, alpha):
    t, d = h2d.shape
    heads = a_log.shape[0]
    dv = norm_g.shape[0]
    nv = heads * dv
    nqk = (conv_w.shape[1] - nv) // 2
    assert nqk // heads == LANES and dv == LANES and 2 * heads <= LANES
    chunk = min(GDN_KERNEL_CHUNK, seq)
    ts = min(512, seq)
    tc = min(512, seq)
    cw = 2 * nqk + nv
    wqkv = w_in[:, :cw].astype(BF16)
    wz = w_in[:, cw:cw + nv].astype(BF16)
    wbg = jnp.pad(w_in[:, cw + nv:], ((0, 0), (0, LANES - 2 * heads))).astype(BF16)
    alog = jnp.pad(a_log, (heads, LANES - 2 * heads)).reshape(1, LANES)
    dtb = jnp.pad(dt_bias, (heads, LANES - 2 * heads)).reshape(1, LANES)
    q, k, v, z, bg = _gdn_proj(h2d, wqkv, wz, wbg, conv_w, alog, dtb, batch, seq, heads, nqk, nv, ts, chunk)
    o = _gdn_chunk(q, k, v, z, bg, norm_g.reshape(1, dv), batch, seq, heads, tc, chunk)
    return _proj_ln(o, w_out.astype(BF16), h2d, g.reshape(1, d), b.reshape(1, d), alpha, min(512, t))


def _head_pad(w, heads, width):
    r = w.shape[0]
    return jnp.pad(w.reshape(r, heads, width), ((0, 0), (0, 0), (0, LANES - width))).reshape(r, heads * LANES)


def _rope_swap(w, half):
    return jnp.concatenate([-w[..., half:], w[..., :half]], -1)


def _mla_layer(h2d, positions, batch, seq, shared_kv, kv_w_a, kv_norm_g, kv_w_uk, kv_w_uv, w_qa, q_norm_g, w_qb,
               w_out, g, b, alpha):
    t, d = h2d.shape
    rank = kv_norm_g.shape[0]
    rope = kv_w_a.shape[1] - rank
    qrank = w_qa.shape[1]
    heads = (w_qb.shape[1] - kv_w_uk.shape[1]) // rope
    nope = kv_w_uk.shape[1] // heads
    vdim = kv_w_uv.shape[1] // heads
    half = rope // 2
    assert nope + rope <= LANES and LANES % vdim == 0 and rank % LANES == 0

    inv_freq = ROPE_THETA ** (-jnp.arange(half, dtype=F32) / half)
    invf = jnp.zeros((LANES,), F32).at[nope:nope + half].set(inv_freq).at[nope + half:nope + rope].set(inv_freq)
    posf = positions.astype(F32).reshape(t, 1)

    kr = kv_w_a[:, rank:]
    place = lambda w: jnp.pad(w, ((0, 0), (nope, LANES - nope - rope)))
    wkva = jnp.concatenate([kv_w_a[:, :rank], place(kr), place(_rope_swap(kr, half))], -1).astype(BF16)
    wuk = _head_pad(kv_w_uk, heads, nope).astype(BF16)
    wq3 = w_qb.reshape(qrank, heads, nope + rope)
    wqb = _head_pad(w_qb, heads, nope + rope).astype(BF16)
    wq_sw = jnp.concatenate([jnp.zeros((qrank, heads, nope), F32), _rope_swap(wq3[..., nope:], half)], -1)
    wqbs = _head_pad(wq_sw.reshape(qrank, heads * (nope + rope)), heads, nope + rope).astype(BF16)

    ts = min(512, t)
    q, k, v = _mla_prep(h2d, posf, invf.reshape(1, LANES), wkva, kv_norm_g.reshape(1, rank), wuk,
                        kv_w_uv.astype(BF16), w_qa.astype(BF16), q_norm_g.reshape(1, qrank), wqb, wqbs,
                        heads, rank, nope, rope, vdim, ts)
    if shared_kv is None:
        shared_kv = (k, v)
    tq = min(512, seq)
    o = _flash(q, shared_kv[0], shared_kv[1], batch, seq, heads, vdim, tq, tq)
    h_new = _proj_ln(o, w_out.astype(BF16), h2d, g.reshape(1, d), b.reshape(1, d), alpha, min(512, t))
    return h_new, shared_kv


def kernel(x, positions, a_w_in, a_conv_w, a_a_log, a_dt_bias, a_norm_g, a_w_out, kv_w_a, kv_norm_g, kv_w_uk,
           kv_w_uv, b_w_qa, b_q_norm_g, b_w_qb, b_w_out, peer_w_q, peer_sub_keys, peer_u, peer_v, ln_g, ln_b):
    batch, seq, d = x.shape
    depth = ln_g.shape[0]
    n_a = a_w_in.shape[0]
    alpha = (2 * depth) ** 0.25
    h = x.reshape(batch * seq, d)
    shared_kv = None
    for layer in range(depth):
        if layer < n_a:
            i = layer
            h = _gdn_layer(h, batch, seq, a_w_in[i], a_conv_w[i], a_a_log[i], a_dt_bias[i], a_norm_g[i],
                           a_w_out[i], ln_g[layer, 0], ln_b[layer, 0], alpha)
        else:
            j = layer - n_a
            h, shared_kv = _mla_layer(h, positions, batch, seq, shared_kv, kv_w_a, kv_norm_g, kv_w_uk, kv_w_uv,
                                      b_w_qa[j], b_q_norm_g[j], b_w_qb[j], b_w_out[j],
                                      ln_g[layer, 0], ln_b[layer, 0], alpha)
        h = _peer(h, peer_w_q[layer], peer_sub_keys[layer], peer_u[layer], peer_v[layer],
                  ln_g[layer, 1], ln_b[layer, 1], alpha)
    return h.reshape(batch, seq, d)
```

```python
import functools
import math

import jax
import jax.numpy as jnp
from jax import lax
from jax.experimental import pallas as pl
from jax.experimental.pallas import tpu as pltpu

F32 = jnp.float32
BF16 = jnp.bfloat16
HIGHEST = lax.Precision.HIGHEST

LN_EPS = 1e-5
RMS_EPS = 1e-6
ROPE_THETA = 10000.0
GDN_CONV = 4
PEER_TOPK = 16
LANES = 128
SUBLANES = 8
GDN_KERNEL_CHUNK = 128
VMEM_LIMIT_BYTES = 56 * 1024 * 1024


def _cparams(*sem):
    return pltpu.CompilerParams(dimension_semantics=sem, vmem_limit_bytes=VMEM_LIMIT_BYTES)


def _dot(a, b):
    return jnp.dot(a, b, preferred_element_type=F32)


def _dot_nt(a, b):
    return lax.dot_general(a, b, (((1,), (1,)), ((), ())), preferred_element_type=F32)


def _dot_tn(a, b):
    return lax.dot_general(a, b, (((0,), (0,)), ((), ())), preferred_element_type=F32)


def _dot_hi(a, b):
    return jnp.dot(a, b, precision=HIGHEST, preferred_element_type=F32)


def _sigmoid(x):
    return 1.0 / (1.0 + jnp.exp(-x))


def _softplus(x):
    return jnp.maximum(x, 0.0) + jnp.log1p(jnp.exp(-jnp.abs(x)))


def _layer_norm(y, g, b):
    mu = jnp.mean(y, -1, keepdims=True)
    d = y - mu
    var = jnp.mean(d * d, -1, keepdims=True)
    return d * lax.rsqrt(var + LN_EPS) * g + b


def _rms(x, g):
    return x * lax.rsqrt(jnp.mean(x * x, -1, keepdims=True) + RMS_EPS) * g


def _proj_ln_kernel(a_ref, w_ref, res_ref, g_ref, b_ref, o_ref, *, alpha):
    y = alpha * res_ref[...] + _dot(a_ref[...], w_ref[...])
    o_ref[...] = _layer_norm(y, g_ref[...], b_ref[...])


def _proj_ln(a, w, res, g, b, alpha, tm):
    t, k = a.shape
    d = w.shape[1]
    return pl.pallas_call(
        functools.partial(_proj_ln_kernel, alpha=alpha),
        grid=(t // tm,),
        in_specs=[pl.BlockSpec((tm, k), lambda i: (i, 0)),
                  pl.BlockSpec((k, d), lambda i: (0, 0)),
                  pl.BlockSpec((tm, d), lambda i: (i, 0)),
                  pl.BlockSpec((1, d), lambda i: (0, 0)),
                  pl.BlockSpec((1, d), lambda i: (0, 0))],
        out_specs=pl.BlockSpec((tm, d), lambda i: (i, 0)),
        out_shape=jax.ShapeDtypeStruct((t, d), F32),
        compiler_params=_cparams("parallel"),
        name="proj_ln",
    )(a, w, res, g, b)


def _gdn_proj_kernel(x_ref, wqkv_ref, wz_ref, wbg_ref, convw_ref, alog_ref, dtb_ref, tri_ref,
                     q_ref, k_ref, v_ref, z_ref, bg_ref, buf_ref, *, ts, nqk, nv, heads, chunk):
    s = pl.program_id(1)
    halo = SUBLANES
    xb = x_ref[...].astype(BF16)

    @pl.when(s == 0)
    def _():
        buf_ref[0:halo, :] = jnp.zeros((halo, buf_ref.shape[1]), F32)

    buf_ref[halo:halo + ts, :] = _dot(xb, wqkv_ref[...])
    z_ref[...] = _dot(xb, wz_ref[...]).astype(BF16)

    dk = nqk // heads
    for grp in range((2 * nqk + nv) // LANES):
        c0 = grp * LANES
        acc = None
        for j in range(GDN_CONV):
            r0 = halo - (GDN_CONV - 1) + j
            term = buf_ref[r0:r0 + ts, c0:c0 + LANES] * convw_ref[j:j + 1, c0:c0 + LANES]
            acc = term if acc is None else acc + term
        y = acc * _sigmoid(acc)
        if c0 < 2 * nqk:
            y = y * lax.rsqrt(jnp.sum(y * y, -1, keepdims=True) + RMS_EPS)
        if c0 < nqk:
            q_ref[:, c0:c0 + LANES] = (y * dk ** -0.5).astype(BF16)
        elif c0 < 2 * nqk:
            k_ref[:, c0 - nqk:c0 - nqk + LANES] = y.astype(BF16)
        else:
            v_ref[:, c0 - 2 * nqk:c0 - 2 * nqk + LANES] = y.astype(BF16)

    buf_ref[0:halo, :] = buf_ref[ts:ts + halo, :]

    bgp = _dot(xb, wbg_ref[...])
    lane = lax.broadcasted_iota(jnp.int32, (chunk, LANES), 1)
    for c in range(ts // chunk):
        blk = bgp[c * chunk:(c + 1) * chunk, :]
        beta = _sigmoid(blk)
        g = -jnp.exp(alog_ref[...]) * _softplus(blk + dtb_ref[...])
        gc = _dot_hi(tri_ref[...], g)
        bg_ref[c * chunk:(c + 1) * chunk, :] = jnp.where(lane < heads, beta, gc)


def _gdn_proj(x2d, wqkv, wz, wbg, convw, alog, dtb, batch, seq, heads, nqk, nv, ts, chunk):
    t, d = x2d.shape
    ns = seq // ts
    cw = 2 * nqk + nv
    tri = jnp.tril(jnp.ones((chunk, chunk), F32))
    row = lambda b, s: (b * ns + s, 0)
    fixed = lambda b, s: (0, 0)
    return pl.pallas_call(
        functools.partial(_gdn_proj_kernel, ts=ts, nqk=nqk, nv=nv, heads=heads, chunk=chunk),
        grid=(batch, ns),
        in_specs=[pl.BlockSpec((ts, d), row),
                  pl.BlockSpec((d, cw), fixed),
                  pl.BlockSpec((d, nv), fixed),
                  pl.BlockSpec((d, LANES), fixed),
                  pl.BlockSpec((GDN_CONV, cw), fixed),
                  pl.BlockSpec((1, LANES), fixed),
                  pl.BlockSpec((1, LANES), fixed),
                  pl.BlockSpec((chunk, chunk), fixed)],
        out_specs=[pl.BlockSpec((ts, nqk), row),
                   pl.BlockSpec((ts, nqk), row),
                   pl.BlockSpec((ts, nv), row),
                   pl.BlockSpec((ts, nv), row),
                   pl.BlockSpec((ts, LANES), row)],
        out_shape=[jax.ShapeDtypeStruct((t, nqk), BF16),
                   jax.ShapeDtypeStruct((t, nqk), BF16),
                   jax.ShapeDtypeStruct((t, nv), BF16),
                   jax.ShapeDtypeStruct((t, nv), BF16),
                   jax.ShapeDtypeStruct((t, LANES), F32)],
        scratch_shapes=[pltpu.VMEM((ts + 2 * SUBLANES, cw), F32)],
        compiler_params=_cparams("parallel", "arbitrary"),
        name="gdn_proj",
    )(x2d, wqkv, wz, wbg, convw, alog, dtb, tri)


def _gdn_chunk_kernel(q_ref, k_ref, v_ref, z_ref, bg_ref, ng_ref, o_ref, state_ref, *, tc, heads, chunk):
    @pl.when(pl.program_id(1) == 0)
    def _():
        state_ref[...] = jnp.zeros(state_ref.shape, F32)

    rows = lax.broadcasted_iota(jnp.int32, (chunk, chunk), 0)
    cols = lax.broadcasted_iota(jnp.int32, (chunk, chunk), 1)
    tril = rows >= cols
    strict = rows > cols
    eye = jnp.where(rows == cols, 1.0, 0.0).astype(F32)
    hr = range(heads)

    def body(c, carry):
        r0 = pl.multiple_of(c * chunk, chunk)
        bg = bg_ref[pl.ds(r0, chunk), :]
        hs = [slice(h * LANES, (h + 1) * LANES) for h in hr]
        beta = [bg[:, h:h + 1] for h in hr]
        gcol = [bg[:, heads + h:heads + h + 1] for h in hr]
        glast = [g[chunk - 1:chunk, :] for g in gcol]
        eg = [jnp.exp(g) for g in gcol]
        decay = []
        for h in hr:
            gmat = jnp.broadcast_to(gcol[h], (chunk, chunk))
            decay.append(jnp.where(tril, jnp.exp(jnp.where(tril, gmat - gmat.T, 0.0)), 0.0))
        qb = [q_ref[pl.ds(r0, chunk), hs[h]] for h in hr]
        kb = [k_ref[pl.ds(r0, chunk), hs[h]] for h in hr]
        kf = [k.astype(F32) for k in kb]
        kbeta = [kf[h] * beta[h] for h in hr]
        vbeta = [v_ref[pl.ds(r0, chunk), hs[h]].astype(F32) * beta[h] for h in hr]

        kk = [_dot_nt(jnp.concatenate([kbeta[h].astype(BF16), qb[h]], axis=0), kb[h]) for h in hr]
        low = [jnp.where(strict, kk[h][:chunk] * decay[h], 0.0) for h in hr]
        attn = [(kk[h][chunk:] * decay[h]).astype(BF16) for h in hr]

        inv = [eye - l for l in low]
        power = low
        span = 1
        while 2 * span < chunk:
            pb = [p.astype(BF16) for p in power]
            power = [_dot(p, p) for p in pb]
            inv = [inv[h] + _dot(inv[h].astype(BF16), power[h].astype(BF16)) for h in hr]
            span *= 2

        rhs = [jnp.concatenate([vbeta[h], kbeta[h] * eg[h]], axis=1).astype(BF16) for h in hr]
        sol = [_dot(inv[h].astype(BF16), rhs[h]) for h in hr]
        dv = vbeta[0].shape[1]

        st = [state_ref[h] for h in hr]
        lhs = [jnp.concatenate([sol[h][:, dv:], qb[h].astype(F32) * eg[h]], axis=0).astype(BF16) for h in hr]
        ws = [_dot(lhs[h], st[h].astype(BF16)) for h in hr]
        vnb = [(sol[h][:, :dv] - ws[h][:chunk]).astype(BF16) for h in hr]
        o = [ws[h][chunk:] + _dot(attn[h], vnb[h]) for h in hr]
        for h in hr:
            kdec = (kf[h] * jnp.exp(glast[h] - gcol[h])).astype(BF16)
            state_ref[h] = st[h] * jnp.exp(glast[h]) + _dot_tn(kdec, vnb[h])
        for h in hr:
            zf = z_ref[pl.ds(r0, chunk), hs[h]].astype(F32)
            on = _rms(o[h], ng_ref[...]) * (zf * _sigmoid(zf))
            o_ref[pl.ds(r0, chunk), hs[h]] = on.astype(BF16)
        return carry

    lax.fori_loop(0, tc // chunk, body, 0)


def _gdn_chunk(q, k, v, z, bg, norm_g, batch, seq, heads, tc, chunk):
    t, nqk = q.shape
    nv = v.shape[1]
    ns = seq // tc
    row = lambda b, s: (b * ns + s, 0)
    return pl.pallas_call(
        functools.partial(_gdn_chunk_kernel, tc=tc, heads=heads, chunk=chunk),
        grid=(batch, ns),
        in_specs=[pl.BlockSpec((tc, nqk), row),
                  pl.BlockSpec((tc, nqk), row),
                  pl.BlockSpec((tc, nv), row),
                  pl.BlockSpec((tc, nv), row),
                  pl.BlockSpec((tc, LANES), row),
                  pl.BlockSpec((1, LANES), lambda b, s: (0, 0))],
        out_specs=pl.BlockSpec((tc, nv), row),
        out_shape=jax.ShapeDtypeStruct((t, nv), BF16),
        scratch_shapes=[pltpu.VMEM((heads, nqk // heads, nv // heads), F32)],
        compiler_params=_cparams("parallel", "arbitrary"),
        name="gdn_chunk",
    )(q, k, v, z, bg, norm_g)


def _mla_prep_kernel(x_ref, pos_ref, invf_ref, wkva_ref, gkv_ref, wuk_ref, wuv_ref, vones_ref, wqa_ref, gq_ref,
                     wqb_ref, wqbs_ref, q_ref, k_ref, v_ref, *, heads, rank, nope, rope, scale):
    xb = x_ref[...].astype(BF16)
    ts = xb.shape[0]
    lane = lax.broadcasted_iota(jnp.int32, (ts, LANES), 1)
    ang = pos_ref[...] * invf_ref[...]
    is_rope = (lane >= nope) & (lane < nope + rope)
    cosm = jnp.where(lane < nope, 1.0, jnp.where(is_rope, jnp.cos(ang), 0.0))
    sinm = jnp.where(is_rope, jnp.sin(ang), 0.0)

    kv = _dot(xb, wkva_ref[...])
    ckv = _rms(kv[:, :rank], gkv_ref[...]).astype(BF16)
    krope = kv[:, rank:rank + LANES] * cosm + kv[:, rank + LANES:rank + 2 * LANES] * sinm
    v_ref[...] = (_dot(ckv, wuv_ref[...]) + vones_ref[...]).astype(BF16)
    knope = _dot(ckv, wuk_ref[...])

    qn = _rms(_dot(xb, wqa_ref[...]), gq_ref[...]).astype(BF16)
    qa = _dot(qn, wqb_ref[...])
    qs = _dot(qn, wqbs_ref[...])
    for h in range(heads):
        hs = slice(h * LANES, (h + 1) * LANES)
        k_ref[:, hs] = (knope[:, hs] + krope).astype(BF16)
        q_ref[:, hs] = ((qa[:, hs] * cosm + qs[:, hs] * sinm) * scale).astype(BF16)


def _mla_prep(x2d, posf, invf, wkva, gkv, wuk, wuv, vones, wqa, gq, wqb, wqbs, heads, rank, nope, rope, ts):
    t, d = x2d.shape
    scale = (nope + rope) ** -0.5
    row = lambda i: (i, 0)
    fixed = lambda i: (0, 0)
    full = lambda a: pl.BlockSpec(a.shape, fixed)
    ospec = pl.BlockSpec((ts, heads * LANES), row)
    return pl.pallas_call(
        functools.partial(_mla_prep_kernel, heads=heads, rank=rank, nope=nope, rope=rope, scale=scale),
        grid=(t // ts,),
        in_specs=[pl.BlockSpec((ts, d), row), pl.BlockSpec((ts, 1), row), full(invf), full(wkva), full(gkv),
                  full(wuk), full(wuv), full(vones), full(wqa), full(gq), full(wqb), full(wqbs)],
        out_specs=[ospec, ospec, ospec],
        out_shape=[jax.ShapeDtypeStruct((t, heads * LANES), BF16)] * 3,
        compiler_params=_cparams("parallel"),
        name="mla_prep",
    )(x2d, posf, invf, wkva, gkv, wuk, wuv, vones, wqa, gq, wqb, wqbs)


def _flash_kernel(q_ref, k_ref, v_ref, o_ref, *, tq, tk, vdim):
    qi = pl.program_id(2)
    hpp = LANES // vdim
    lane = lax.broadcasted_iota(jnp.int32, (tq, LANES), 1)
    qs = [q_ref[:, hh * LANES:(hh + 1) * LANES] for hh in range(hpp)]

    def chunk(c, carry, masked):
        k0 = pl.multiple_of(c * tk, tk)
        ss = [_dot_nt(qs[hh], k_ref[pl.ds(k0, tk), hh * LANES:(hh + 1) * LANES]) for hh in range(hpp)]
        if masked:
            keep = lax.broadcasted_iota(jnp.int32, (tq, tk), 1) <= lax.broadcasted_iota(jnp.int32, (tq, tk), 0)
            ss = [jnp.where(keep, s, -jnp.inf) for s in ss]
        out = []
        for hh in range(hpp):
            m, acc = carry[hh]
            m_new = jnp.maximum(m, jnp.max(ss[hh], -1, keepdims=True))
            p = jnp.exp((ss[hh] - m_new).astype(BF16))
            acc = jnp.exp(m - m_new) * acc + _dot(p, v_ref[pl.ds(k0, tk), hh * LANES:(hh + 1) * LANES])
            out.append((m_new, acc))
        return tuple(out)

    init = tuple((jnp.full((tq, 1), -jnp.inf, F32), jnp.zeros((tq, LANES), F32)) for _ in range(hpp))
    carry = lax.fori_loop(0, qi, functools.partial(chunk, masked=False), init)
    carry = chunk(qi, carry, masked=True)
    out = jnp.zeros((tq, LANES), F32)
    for hh in range(hpp):
        acc = carry[hh][1]
        ones_lane = ((hh + 1) * vdim) % LANES
        out = jnp.where((lane >= hh * vdim) & (lane < (hh + 1) * vdim), acc / acc[:, ones_lane:ones_lane + 1], out)
    o_ref[...] = out.astype(BF16)


def _flash(q, k, v, batch, seq, heads, vdim, tq, tk):
    t = q.shape[0]
    hpp = LANES // vdim
    nq = seq // tq
    assert tq == tk
    return pl.pallas_call(
        functools.partial(_flash_kernel, tq=tq, tk=tk, vdim=vdim),
        grid=(batch, heads // hpp, nq),
        in_specs=[pl.BlockSpec((tq, hpp * LANES), lambda b, h, i: (b * nq + i, h)),
                  pl.BlockSpec((seq, hpp * LANES), lambda b, h, i: (b, h)),
                  pl.BlockSpec((seq, hpp * LANES), lambda b, h, i: (b, h))],
        out_specs=pl.BlockSpec((tq, LANES), lambda b, h, i: (b * nq + i, h)),
        out_shape=jax.ShapeDtypeStruct((t, heads * vdim), BF16),
        compiler_params=_cparams("parallel", "parallel", "arbitrary"),
        name="mla_flash",
    )(q, k, v)


def _top_values(sc, n):
    vals = []
    cur = sc
    for _ in range(n):
        m = jnp.max(cur, axis=0, keepdims=True)
        vals.append(m)
        cur = jnp.where(cur == m, -jnp.inf, cur)
    return vals


def _sorting_network(n):
    pairs = []
    p = 1
    while p < n:
        k = p
        while k >= 1:
            for j in range(k % p, n - k, 2 * k):
                for i in range(min(k, n - j - k)):
                    if (i + j) // (2 * p) == (i + j + k) // (2 * p):
                        pairs.append((i + j, i + j + k))
            k //= 2
        p *= 2
    return pairs


def _top_values_sorted(sc, n):
    nslab = sc.shape[0] // SUBLANES
    cols = [sc[v * SUBLANES:(v + 1) * SUBLANES, :] for v in range(nslab)]
    for i, j in _sorting_network(nslab):
        cols[i], cols[j] = jnp.maximum(cols[i], cols[j]), jnp.minimum(cols[i], cols[j])
    vals = []
    for r in range(n):
        m = jnp.max(cols[0], axis=0, keepdims=True)
        vals.append(m)
        hit = cols[0] == m
        for k in range(min(nslab, n - 1 - r)):
            nxt = cols[k + 1] if k + 1 < nslab else -jnp.inf
            cols[k] = jnp.where(hit, nxt, cols[k])
    return vals


def _peer_route_kernel(x_ref, wq_ref, keys_ref, xb_ref, thr_ref, e1_ref, s2_ref, e2_ref, *, heads, topk):
    xb = x_ref[...].astype(BF16)
    xb_ref[...] = xb
    q = _dot(xb, wq_ref[...])
    nk = keys_ref.shape[1]
    for h in range(heads):
        s1 = _dot_nt(keys_ref[2 * h], q[:, 2 * h * nk:(2 * h + 1) * nk].astype(BF16))
        s2 = _dot_nt(keys_ref[2 * h + 1], q[:, (2 * h + 1) * nk:(2 * h + 2) * nk].astype(BF16))
        a = _top_values_sorted(s1, topk + 1)
        b = _top_values_sorted(s2, topk + 1)
        cands = [a[i] + b[j] for i in range(topk + 1) for j in range(topk + 1) if (i + 1) * (j + 1) <= topk + 1]
        best = _top_values(jnp.concatenate(cands, axis=0), topk + 1)
        zsum = None
        for r in range(topk):
            e = jnp.exp(best[r] - best[0])
            zsum = e if zsum is None else zsum + e
        thr = 0.5 * (best[topk - 1] + best[topk]) - s1
        outs = ((thr_ref, thr), (e1_ref, jnp.exp(s1 - a[0]) / zsum), (s2_ref, s2), (e2_ref, jnp.exp(s2 - b[0])))
        for ref, val in outs:
            for tg in range(val.shape[1] // LANES):
                ref[0, h, tg] = val[:, tg * LANES:(tg + 1) * LANES]


def _peer_route(x2d, wq, keys, heads, ts):
    t, d = x2d.shape
    nk = keys.shape[1]
    nt = t // ts
    shape = (nt, heads, ts // LANES, nk, LANES)
    ospec = pl.BlockSpec((1,) + shape[1:], lambda i: (i, 0, 0, 0, 0))
    return pl.pallas_call(
        functools.partial(_peer_route_kernel, heads=heads, topk=PEER_TOPK),
        grid=(nt,),
        in_specs=[pl.BlockSpec((ts, d), lambda i: (i, 0)),
                  pl.BlockSpec(wq.shape, lambda i: (0, 0)),
                  pl.BlockSpec(keys.shape, lambda i: (0, 0, 0))],
        out_specs=[pl.BlockSpec((ts, d), lambda i: (i, 0)), ospec, ospec, ospec, ospec],
        out_shape=[jax.ShapeDtypeStruct((t, d), BF16)] + [jax.ShapeDtypeStruct(shape, F32)] * 4,
        compiler_params=_cparams("parallel"),
        name="peer_route",
    )(x2d, wq, keys)


def _gelu(x):
    return 0.5 * x * (1.0 + lax.erf(x * (2.0 ** -0.5)))


GATE_I = 2
GATE_ROWS = 64


def _peer_dense_kernel(x_ref, xb_ref, u_ref, vt_ref, thr_ref, e1_ref, s2_ref, e2_ref, g_ref, b_ref, o_ref,
                       acc_ref, act_ref, hid_ref, *, heads, nk, eb, alpha):
    e = pl.program_id(1)
    ts = xb_ref.shape[0]
    ni = eb // nk
    ntg = ts // LANES

    @pl.when(e == 0)
    def _():
        acc_ref[...] = jnp.zeros(acc_ref.shape, F32)

    hid = _gelu(_dot_nt(u_ref[...], xb_ref[...]))
    for il in range(ni):
        for tg in range(ntg):
            hid_ref[il, tg] = hid[il * nk:(il + 1) * nk, tg * LANES:(tg + 1) * LANES]

    def token_group(tg, carry):
        for ip in range(ni // GATE_I):
            for jc in range(nk // GATE_ROWS):
                js = slice(jc * GATE_ROWS, (jc + 1) * GATE_ROWS)
                gates = [None] * GATE_I
                for h in range(heads):
                    s2 = s2_ref[0, h, tg, js, :]
                    e2 = e2_ref[0, h, tg, js, :]
                    for a in range(GATE_I):
                        il = ip * GATE_I + a
                        thr = thr_ref[0, h, tg, il:il + 1, :]
                        e1 = e1_ref[0, h, tg, il:il + 1, :]
                        term = jnp.where(s2 >= thr, e2, 0.0) * e1
                        gates[a] = term if gates[a] is None else gates[a] + term
                for a in range(GATE_I):
                    il = ip * GATE_I + a
                    act = gates[a] * hid_ref[il, tg, js, :]
                    act_ref[tg, il * nk + jc * GATE_ROWS:il * nk + (jc + 1) * GATE_ROWS, :] = act.astype(BF16)
        return carry

    lax.fori_loop(0, ntg, token_group, 0)
    act = jnp.concatenate([act_ref[tg] for tg in range(ntg)], axis=1)
    acc_ref[...] += _dot(vt_ref[...], act)

    @pl.when(e == pl.num_programs(1) - 1)
    def _():
        o_ref[...] = _layer_norm(alpha * x_ref[...] + acc_ref[...].T, g_ref[...], b_ref[...])


def _peer_dense(x2d, xb, u, vt, thr, e1, s2, e2, g, b, alpha, heads, ts, eb):
    t, d = x2d.shape
    n = u.shape[0]
    nk = s2.shape[3]
    ntg = ts // LANES
    assert (eb // nk) % GATE_I == 0 and nk % GATE_ROWS == 0
    jspec = pl.BlockSpec((1, heads, ntg, nk, LANES), lambda i, e: (i, 0, 0, 0, 0))
    ispec = pl.BlockSpec((1, heads, ntg, eb // nk, LANES), lambda i, e: (i, 0, 0, e, 0))
    return pl.pallas_call(
        functools.partial(_peer_dense_kernel, heads=heads, nk=nk, eb=eb, alpha=alpha),
        grid=(t // ts, n // eb),
        in_specs=[pl.BlockSpec((ts, d), lambda i, e: (i, 0)),
                  pl.BlockSpec((ts, d), lambda i, e: (i, 0)),
                  pl.BlockSpec((eb, d), lambda i, e: (e, 0)),
                  pl.BlockSpec((d, eb), lambda i, e: (0, e)),
                  ispec, ispec, jspec, jspec,
                  pl.BlockSpec((1, d), lambda i, e: (0, 0)),
                  pl.BlockSpec((1, d), lambda i, e: (0, 0))],
        out_specs=pl.BlockSpec((ts, d), lambda i, e: (i, 0)),
        out_shape=jax.ShapeDtypeStruct((t, d), F32),
        scratch_shapes=[pltpu.VMEM((d, ts), F32), pltpu.VMEM((ntg, eb, LANES), BF16),
                        pltpu.VMEM((eb // nk, ntg, nk, LANES), F32)],
        compiler_params=_cparams("parallel", "arbitrary"),
        name="peer_dense",
    )(x2d, xb, u, vt, thr, e1, s2, e2, g, b)


def _peer(h2d, w_q, sub_keys, u_tab, v_tab, g, b, alpha):
    t, d = h2d.shape
    heads, _, nk, dh = sub_keys.shape
    assert nk == LANES and dh == LANES
    ts = min(512, t)
    eb = min(2048, u_tab.shape[0])
    keys = sub_keys.reshape(heads * 2, nk, dh).astype(BF16)
    xb, thr, e1, s2, e2 = _peer_route(h2d, w_q.astype(BF16), keys, heads, ts)
    return _peer_dense(h2d, xb, u_tab.astype(BF16), v_tab.astype(BF16).T, thr, e1, s2, e2,
                       g.reshape(1, d), b.reshape(1, d), alpha, heads, ts, eb)


def _gdn_layer(h2d, batch, seq, w_in, conv_w, a_log, dt_bias, norm_g, w_out, g, b, alpha):
    t, d = h2d.shape
    heads = a_log.shape[0]
    dv = norm_g.shape[0]
    nv = heads * dv
    nqk = (conv_w.shape[1] - nv) // 2
    assert nqk // heads == LANES and dv == LANES and 2 * heads <= LANES
    chunk = min(GDN_KERNEL_CHUNK, seq)
    ts = min(512, seq)
    tc = min(512, seq)
    cw = 2 * nqk + nv
    wqkv = w_in[:, :cw].astype(BF16)
    wz = w_in[:, cw:cw + nv].astype(BF16)
    wbg = jnp.pad(w_in[:, cw + nv:], ((0, 0), (0, LANES - 2 * heads))).astype(BF16)
    alog = jnp.pad(a_log, (heads, LANES - 2 * heads)).reshape(1, LANES)
    dtb = jnp.pad(dt_bias, (heads, LANES - 2 * heads)).reshape(1, LANES)
    q, k, v, z, bg = _gdn_proj(h2d, wqkv, wz, wbg, conv_w, alog, dtb, batch, seq, heads, nqk, nv, ts, chunk)
    o = _gdn_chunk(q, k, v, z, bg, norm_g.reshape(1, dv), batch, seq, heads, tc, chunk)
    return _proj_ln(o, w_out.astype(BF16), h2d, g.reshape(1, d), b.reshape(1, d), alpha, min(512, t))


def _head_pad(w, heads, width):
    r = w.shape[0]
    return jnp.pad(w.reshape(r, heads, width), ((0, 0), (0, 0), (0, LANES - width))).reshape(r, heads * LANES)


def _rope_swap(w, half):
    return jnp.concatenate([-w[..., half:], w[..., :half]], -1)


def _mla_layer(h2d, positions, batch, seq, shared_kv, kv_w_a, kv_norm_g, kv_w_uk, kv_w_uv, w_qa, q_norm_g, w_qb,
               w_out, g, b, alpha):
    t, d = h2d.shape
    rank = kv_norm_g.shape[0]
    rope = kv_w_a.shape[1] - rank
    qrank = w_qa.shape[1]
    heads = (w_qb.shape[1] - kv_w_uk.shape[1]) // rope
    nope = kv_w_uk.shape[1] // heads
    vdim = kv_w_uv.shape[1] // heads
    half = rope // 2
    assert nope + rope <= LANES and LANES % vdim == 0 and vdim < LANES and rank % LANES == 0

    inv_freq = ROPE_THETA ** (-jnp.arange(half, dtype=F32) / half)
    invf = jnp.zeros((LANES,), F32).at[nope:nope + half].set(inv_freq).at[nope + half:nope + rope].set(inv_freq)
    posf = positions.astype(F32).reshape(t, 1)

    kr = kv_w_a[:, rank:]
    place = lambda w: jnp.pad(w, ((0, 0), (nope, LANES - nope - rope)))
    wkva = jnp.concatenate([kv_w_a[:, :rank], place(kr), place(_rope_swap(kr, half))], -1).astype(BF16)
    wuk = _head_pad(kv_w_uk, heads, nope).astype(BF16)
    wq3 = w_qb.reshape(qrank, heads, nope + rope)
    wqb = _head_pad(w_qb, heads, nope + rope).astype(BF16)
    wq_sw = jnp.concatenate([jnp.zeros((qrank, heads, nope), F32), _rope_swap(wq3[..., nope:], half)], -1)
    wqbs = _head_pad(wq_sw.reshape(qrank, heads * (nope + rope)), heads, nope + rope).astype(BF16)

    hpp = LANES // vdim
    wuv3 = kv_w_uv.reshape(rank, heads, vdim)
    wuv = jnp.concatenate([jnp.pad(wuv3[:, h], ((0, 0), ((h % hpp) * vdim, LANES - (h % hpp + 1) * vdim)))
                           for h in range(heads)], -1).astype(BF16)
    ones_at = jnp.array([h * LANES + ((h % hpp + 1) * vdim) % LANES for h in range(heads)])
    vones = jnp.zeros((1, heads * LANES), F32).at[0, ones_at].set(1.0)

    ts = min(512, t)
    q, k, v = _mla_prep(h2d, posf, invf.reshape(1, LANES), wkva, kv_norm_g.reshape(1, rank), wuk, wuv, vones,
                        w_qa.astype(BF16), q_norm_g.reshape(1, qrank), wqb, wqbs, heads, rank, nope, rope, ts)
    if shared_kv is None:
        shared_kv = (k, v)
    tq = min(512, seq)
    o = _flash(q, shared_kv[0], shared_kv[1], batch, seq, heads, vdim, tq, tq)
    h_new = _proj_ln(o, w_out.astype(BF16), h2d, g.reshape(1, d), b.reshape(1, d), alpha, min(512, t))
    return h_new, shared_kv


def kernel(x, positions, a_w_in, a_conv_w, a_a_log, a_dt_bias, a_norm_g, a_w_out, kv_w_a, kv_norm_g, kv_w_uk,
           kv_w_uv, b_w_qa, b_q_norm_g, b_w_qb, b_w_out, peer_w_q, peer_sub_keys, peer_u, peer_v, ln_g, ln_b):
    batch, seq, d = x.shape
    depth = ln_g.shape[0]
    n_a = a_w_in.shape[0]
    alpha = (2 * depth) ** 0.25
    h = x.reshape(batch * seq, d)
    shared_kv = None
    for layer in range(depth):
        if layer < n_a:
            i = layer
            h = _gdn_layer(h, batch, seq, a_w_in[i], a_conv_w[i], a_a_log[i], a_dt_bias[i], a_norm_g[i],
                           a_w_out[i], ln_g[layer, 0], ln_b[layer, 0], alpha)
        else:
            j = layer - n_a
            h, shared_kv = _mla_layer(h, positions, batch, seq, shared_kv, kv_w_a, kv_norm_g, kv_w_uk, kv_w_uv,
                                      b_w_qa[j], b_q_norm_g[j], b_w_qb[j], b_w_out[j],
                                      ln_g[layer, 0], ln_b[layer, 0], alpha)
        h = _peer(h, peer_w_q[layer], peer_sub_keys[layer], peer_u[layer], peer_v[layer],
                  ln_g[layer, 1], ln_b[layer, 1], alpha)
    return h.reshape(batch, seq, d)
```

```python
import functools
import math

import jax
import jax.numpy as jnp
from jax import lax
from jax.experimental import pallas as pl
from jax.experimental.pallas import tpu as pltpu

F32 = jnp.float32
BF16 = jnp.bfloat16
HIGHEST = lax.Precision.HIGHEST

LN_EPS = 1e-5
RMS_EPS = 1e-6
ROPE_THETA = 10000.0
GDN_CONV = 4
PEER_TOPK = 16
LANES = 128
SUBLANES = 8
GDN_KERNEL_CHUNK = 128
GDN_CHUNKS_PER_ITER = 2
VMEM_LIMIT_BYTES = 56 * 1024 * 1024


def _cparams(*sem):
    return pltpu.CompilerParams(dimension_semantics=sem, vmem_limit_bytes=VMEM_LIMIT_BYTES)


def _dot(a, b):
    return jnp.dot(a, b, preferred_element_type=F32)


def _dot_nt(a, b):
    return lax.dot_general(a, b, (((1,), (1,)), ((), ())), preferred_element_type=F32)


def _dot_tn(a, b):
    return lax.dot_general(a, b, (((0,), (0,)), ((), ())), preferred_element_type=F32)


def _dot_hi(a, b):
    return jnp.dot(a, b, precision=HIGHEST, preferred_element_type=F32)


def _sigmoid(x):
    return 1.0 / (1.0 + jnp.exp(-x))


def _softplus(x):
    return jnp.maximum(x, 0.0) + jnp.log1p(jnp.exp(-jnp.abs(x)))


def _layer_norm(y, g, b):
    mu = jnp.mean(y, -1, keepdims=True)
    d = y - mu
    var = jnp.mean(d * d, -1, keepdims=True)
    return d * lax.rsqrt(var + LN_EPS) * g + b


def _rms(x, g):
    return x * lax.rsqrt(jnp.mean(x * x, -1, keepdims=True) + RMS_EPS) * g


def _proj_ln_kernel(a_ref, w_ref, res_ref, g_ref, b_ref, o_ref, *, alpha):
    y = alpha * res_ref[...] + _dot(a_ref[...], w_ref[...])
    o_ref[...] = _layer_norm(y, g_ref[...], b_ref[...])


def _proj_ln(a, w, res, g, b, alpha, tm):
    t, k = a.shape
    d = w.shape[1]
    return pl.pallas_call(
        functools.partial(_proj_ln_kernel, alpha=alpha),
        grid=(t // tm,),
        in_specs=[pl.BlockSpec((tm, k), lambda i: (i, 0)),
                  pl.BlockSpec((k, d), lambda i: (0, 0)),
                  pl.BlockSpec((tm, d), lambda i: (i, 0)),
                  pl.BlockSpec((1, d), lambda i: (0, 0)),
                  pl.BlockSpec((1, d), lambda i: (0, 0))],
        out_specs=pl.BlockSpec((tm, d), lambda i: (i, 0)),
        out_shape=jax.ShapeDtypeStruct((t, d), F32),
        compiler_params=_cparams("parallel"),
        name="proj_ln",
    )(a, w, res, g, b)


def _gdn_proj_kernel(x_ref, wqkv_ref, wz_ref, wbg_ref, convw_ref, alog_ref, dtb_ref, tri_ref,
                     q_ref, k_ref, v_ref, z_ref, bg_ref, buf_ref, *, ts, nqk, nv, heads, chunk):
    s = pl.program_id(1)
    halo = SUBLANES
    xb = x_ref[...].astype(BF16)

    @pl.when(s == 0)
    def _():
        buf_ref[0:halo, :] = jnp.zeros((halo, buf_ref.shape[1]), F32)

    buf_ref[halo:halo + ts, :] = _dot(xb, wqkv_ref[...])
    z_ref[...] = _dot(xb, wz_ref[...]).astype(BF16)

    dk = nqk // heads
    for grp in range((2 * nqk + nv) // LANES):
        c0 = grp * LANES
        acc = None
        for j in range(GDN_CONV):
            r0 = halo - (GDN_CONV - 1) + j
            term = buf_ref[r0:r0 + ts, c0:c0 + LANES] * convw_ref[j:j + 1, c0:c0 + LANES]
            acc = term if acc is None else acc + term
        y = acc * _sigmoid(acc)
        if c0 < 2 * nqk:
            y = y * lax.rsqrt(jnp.sum(y * y, -1, keepdims=True) + RMS_EPS)
        if c0 < nqk:
            q_ref[:, c0:c0 + LANES] = (y * dk ** -0.5).astype(BF16)
        elif c0 < 2 * nqk:
            k_ref[:, c0 - nqk:c0 - nqk + LANES] = y.astype(BF16)
        else:
            v_ref[:, c0 - 2 * nqk:c0 - 2 * nqk + LANES] = y.astype(BF16)

    buf_ref[0:halo, :] = buf_ref[ts:ts + halo, :]

    bgp = _dot(xb, wbg_ref[...])
    lane = lax.broadcasted_iota(jnp.int32, (chunk, LANES), 1)
    for c in range(ts // chunk):
        blk = bgp[c * chunk:(c + 1) * chunk, :]
        beta = _sigmoid(blk)
        g = -jnp.exp(alog_ref[...]) * _softplus(blk + dtb_ref[...])
        gc = _dot_hi(tri_ref[...], g)
        bg_ref[c * chunk:(c + 1) * chunk, :] = jnp.where(lane < heads, beta, gc)


def _gdn_proj(x2d, wqkv, wz, wbg, convw, alog, dtb, batch, seq, heads, nqk, nv, ts, chunk):
    t, d = x2d.shape
    ns = seq // ts
    cw = 2 * nqk + nv
    tri = jnp.tril(jnp.ones((chunk, chunk), F32))
    row = lambda b, s: (b * ns + s, 0)
    fixed = lambda b, s: (0, 0)
    return pl.pallas_call(
        functools.partial(_gdn_proj_kernel, ts=ts, nqk=nqk, nv=nv, heads=heads, chunk=chunk),
        grid=(batch, ns),
        in_specs=[pl.BlockSpec((ts, d), row),
                  pl.BlockSpec((d, cw), fixed),
                  pl.BlockSpec((d, nv), fixed),
                  pl.BlockSpec((d, LANES), fixed),
                  pl.BlockSpec((GDN_CONV, cw), fixed),
                  pl.BlockSpec((1, LANES), fixed),
                  pl.BlockSpec((1, LANES), fixed),
                  pl.BlockSpec((chunk, chunk), fixed)],
        out_specs=[pl.BlockSpec((ts, nqk), row),
                   pl.BlockSpec((ts, nqk), row),
                   pl.BlockSpec((ts, nv), row),
                   pl.BlockSpec((ts, nv), row),
                   pl.BlockSpec((ts, LANES), row)],
        out_shape=[jax.ShapeDtypeStruct((t, nqk), BF16),
                   jax.ShapeDtypeStruct((t, nqk), BF16),
                   jax.ShapeDtypeStruct((t, nv), BF16),
                   jax.ShapeDtypeStruct((t, nv), BF16),
                   jax.ShapeDtypeStruct((t, LANES), F32)],
        scratch_shapes=[pltpu.VMEM((ts + 2 * SUBLANES, cw), F32)],
        compiler_params=_cparams("parallel", "arbitrary"),
        name="gdn_proj",
    )(x2d, wqkv, wz, wbg, convw, alog, dtb, tri)


def _gdn_chunk_kernel(q_ref, k_ref, v_ref, z_ref, bg_ref, ng_ref, o_ref, state_ref, *, tc, heads, chunk):
    @pl.when(pl.program_id(1) == 0)
    def _():
        state_ref[...] = jnp.zeros(state_ref.shape, F32)

    rows = lax.broadcasted_iota(jnp.int32, (chunk, chunk), 0)
    cols = lax.broadcasted_iota(jnp.int32, (chunk, chunk), 1)
    tril = rows >= cols
    strict = rows > cols
    eye = jnp.where(rows == cols, 1.0, 0.0).astype(F32)
    hr = range(heads)

    per = math.gcd(tc // chunk, GDN_CHUNKS_PER_ITER)

    def body(it, carry):
        r0s = [pl.multiple_of((it * per + cc) * chunk, chunk) for cc in range(per)]
        units = [(cc, h) for cc in range(per) for h in hr]
        ur = range(len(units))
        bgs = [bg_ref[pl.ds(r0, chunk), :] for r0 in r0s]
        hs = [slice(h * LANES, (h + 1) * LANES) for _, h in units]
        rows = [pl.ds(r0s[cc], chunk) for cc, _ in units]
        beta = [bgs[cc][:, h:h + 1] for cc, h in units]
        gcol = [bgs[cc][:, heads + h:heads + h + 1] for cc, h in units]
        glast = [g[chunk - 1:chunk, :] for g in gcol]
        eg = [jnp.exp(g) for g in gcol]
        decay = []
        for u in ur:
            gmat = jnp.broadcast_to(gcol[u], (chunk, chunk))
            decay.append(jnp.where(tril, jnp.exp(jnp.where(tril, gmat - gmat.T, 0.0)), 0.0))
        qb = [q_ref[rows[u], hs[u]] for u in ur]
        kb = [k_ref[rows[u], hs[u]] for u in ur]
        kf = [k.astype(F32) for k in kb]
        kbeta = [kf[u] * beta[u] for u in ur]
        vbeta = [v_ref[rows[u], hs[u]].astype(F32) * beta[u] for u in ur]

        kk = [_dot_nt(jnp.concatenate([kbeta[u].astype(BF16), qb[u]], axis=0), kb[u]) for u in ur]
        low = [jnp.where(strict, kk[u][:chunk] * decay[u], 0.0) for u in ur]
        attn = [(kk[u][chunk:] * decay[u]).astype(BF16) for u in ur]

        inv = [eye - l for l in low]
        power = low
        span = 1
        while 2 * span < chunk:
            pb = [p.astype(BF16) for p in power]
            power = [_dot(p, p) for p in pb]
            inv = [inv[u] + _dot(inv[u].astype(BF16), power[u].astype(BF16)) for u in ur]
            span *= 2

        rhs = [jnp.concatenate([vbeta[u], kbeta[u] * eg[u]], axis=1).astype(BF16) for u in ur]
        sol = [_dot(inv[u].astype(BF16), rhs[u]) for u in ur]
        dv = vbeta[0].shape[1]

        st = [state_ref[h] for h in hr]
        for cc in range(per):
            us = [cc * heads + h for h in hr]
            lhs = [jnp.concatenate([sol[u][:, dv:], qb[u].astype(F32) * eg[u]], axis=0).astype(BF16) for u in us]
            ws = [_dot(lhs[h], st[h].astype(BF16)) for h in hr]
            vnb = [(sol[us[h]][:, :dv] - ws[h][:chunk]).astype(BF16) for h in hr]
            o = [ws[h][chunk:] + _dot(attn[us[h]], vnb[h]) for h in hr]
            new_st = []
            for h in hr:
                u = us[h]
                kdec = (kf[u] * jnp.exp(glast[u] - gcol[u])).astype(BF16)
                new_st.append(st[h] * jnp.exp(glast[u]) + _dot_tn(kdec, vnb[h]))
            st = new_st
            for h in hr:
                u = us[h]
                zf = z_ref[rows[u], hs[u]].astype(F32)
                on = _rms(o[h], ng_ref[...]) * (zf * _sigmoid(zf))
                o_ref[rows[u], hs[u]] = on.astype(BF16)
        for h in hr:
            state_ref[h] = st[h]
        return carry

    lax.fori_loop(0, tc // chunk // per, body, 0)


def _gdn_chunk(q, k, v, z, bg, norm_g, batch, seq, heads, tc, chunk):
    t, nqk = q.shape
    nv = v.shape[1]
    ns = seq // tc
    row = lambda b, s: (b * ns + s, 0)
    return pl.pallas_call(
        functools.partial(_gdn_chunk_kernel, tc=tc, heads=heads, chunk=chunk),
        grid=(batch, ns),
        in_specs=[pl.BlockSpec((tc, nqk), row),
                  pl.BlockSpec((tc, nqk), row),
                  pl.BlockSpec((tc, nv), row),
                  pl.BlockSpec((tc, nv), row),
                  pl.BlockSpec((tc, LANES), row),
                  pl.BlockSpec((1, LANES), lambda b, s: (0, 0))],
        out_specs=pl.BlockSpec((tc, nv), row),
        out_shape=jax.ShapeDtypeStruct((t, nv), BF16),
        scratch_shapes=[pltpu.VMEM((heads, nqk // heads, nv // heads), F32)],
        compiler_params=_cparams("parallel", "arbitrary"),
        name="gdn_chunk",
    )(q, k, v, z, bg, norm_g)


def _mla_prep_kernel(x_ref, pos_ref, invf_ref, wkva_ref, gkv_ref, wuk_ref, wuv_ref, vones_ref, wqa_ref, gq_ref,
                     wqb_ref, wqbs_ref, q_ref, k_ref, v_ref, *, heads, rank, nope, rope, scale):
    xb = x_ref[...].astype(BF16)
    ts = xb.shape[0]
    lane = lax.broadcasted_iota(jnp.int32, (ts, LANES), 1)
    ang = pos_ref[...] * invf_ref[...]
    is_rope = (lane >= nope) & (lane < nope + rope)
    cosm = jnp.where(lane < nope, 1.0, jnp.where(is_rope, jnp.cos(ang), 0.0))
    sinm = jnp.where(is_rope, jnp.sin(ang), 0.0)

    kv = _dot(xb, wkva_ref[...])
    ckv = _rms(kv[:, :rank], gkv_ref[...]).astype(BF16)
    krope = kv[:, rank:rank + LANES] * cosm + kv[:, rank + LANES:rank + 2 * LANES] * sinm
    v_ref[...] = (_dot(ckv, wuv_ref[...]) + vones_ref[...]).astype(BF16)
    knope = _dot(ckv, wuk_ref[...])

    qn = _rms(_dot(xb, wqa_ref[...]), gq_ref[...]).astype(BF16)
    qa = _dot(qn, wqb_ref[...])
    qs = _dot(qn, wqbs_ref[...])
    for h in range(heads):
        hs = slice(h * LANES, (h + 1) * LANES)
        k_ref[:, hs] = (knope[:, hs] + krope).astype(BF16)
        q_ref[:, hs] = ((qa[:, hs] * cosm + qs[:, hs] * sinm) * scale).astype(BF16)


def _mla_prep(x2d, posf, invf, wkva, gkv, wuk, wuv, vones, wqa, gq, wqb, wqbs, heads, rank, nope, rope, ts):
    t, d = x2d.shape
    scale = (nope + rope) ** -0.5
    row = lambda i: (i, 0)
    fixed = lambda i: (0, 0)
    full = lambda a: pl.BlockSpec(a.shape, fixed)
    ospec = pl.BlockSpec((ts, heads * LANES), row)
    return pl.pallas_call(
        functools.partial(_mla_prep_kernel, heads=heads, rank=rank, nope=nope, rope=rope, scale=scale),
        grid=(t // ts,),
        in_specs=[pl.BlockSpec((ts, d), row), pl.BlockSpec((ts, 1), row), full(invf), full(wkva), full(gkv),
                  full(wuk), full(wuv), full(vones), full(wqa), full(gq), full(wqb), full(wqbs)],
        out_specs=[ospec, ospec, ospec],
        out_shape=[jax.ShapeDtypeStruct((t, heads * LANES), BF16)] * 3,
        compiler_params=_cparams("parallel"),
        name="mla_prep",
    )(x2d, posf, invf, wkva, gkv, wuk, wuv, vones, wqa, gq, wqb, wqbs)


def _flash_kernel(q_ref, k_ref, v_ref, o_ref, *, tq, tk, vdim):
    qi = pl.program_id(2)
    hpp = LANES // vdim
    lane = lax.broadcasted_iota(jnp.int32, (tq, LANES), 1)
    qs = [q_ref[:, hh * LANES:(hh + 1) * LANES] for hh in range(hpp)]

    def chunk(c, carry, masked):
        k0 = pl.multiple_of(c * tk, tk)
        ss = [_dot_nt(qs[hh], k_ref[pl.ds(k0, tk), hh * LANES:(hh + 1) * LANES]) for hh in range(hpp)]
        if masked:
            keep = lax.broadcasted_iota(jnp.int32, (tq, tk), 1) <= lax.broadcasted_iota(jnp.int32, (tq, tk), 0)
            ss = [jnp.where(keep, s, -jnp.inf) for s in ss]
        out = []
        for hh in range(hpp):
            m, acc = carry[hh]
            m_new = jnp.maximum(m, jnp.max(ss[hh], -1, keepdims=True))
            p = jnp.exp((ss[hh] - m_new).astype(BF16))
            acc = jnp.exp(m - m_new) * acc + _dot(p, v_ref[pl.ds(k0, tk), hh * LANES:(hh + 1) * LANES])
            out.append((m_new, acc))
        return tuple(out)

    init = tuple((jnp.full((tq, 1), -jnp.inf, F32), jnp.zeros((tq, LANES), F32)) for _ in range(hpp))
    carry = lax.fori_loop(0, qi, functools.partial(chunk, masked=False), init)
    carry = chunk(qi, carry, masked=True)
    out = jnp.zeros((tq, LANES), F32)
    for hh in range(hpp):
        acc = carry[hh][1]
        ones_lane = ((hh + 1) * vdim) % LANES
        out = jnp.where((lane >= hh * vdim) & (lane < (hh + 1) * vdim), acc / acc[:, ones_lane:ones_lane + 1], out)
    o_ref[...] = out.astype(BF16)


def _flash(q, k, v, batch, seq, heads, vdim, tq, tk):
    t = q.shape[0]
    hpp = LANES // vdim
    nq = seq // tq
    assert tq == tk
    return pl.pallas_call(
        functools.partial(_flash_kernel, tq=tq, tk=tk, vdim=vdim),
        grid=(batch, heads // hpp, nq),
        in_specs=[pl.BlockSpec((tq, hpp * LANES), lambda b, h, i: (b * nq + i, h)),
                  pl.BlockSpec((seq, hpp * LANES), lambda b, h, i: (b, h)),
                  pl.BlockSpec((seq, hpp * LANES), lambda b, h, i: (b, h))],
        out_specs=pl.BlockSpec((tq, LANES), lambda b, h, i: (b * nq + i, h)),
        out_shape=jax.ShapeDtypeStruct((t, heads * vdim), BF16),
        compiler_params=_cparams("parallel", "parallel", "arbitrary"),
        name="mla_flash",
    )(q, k, v)


def _top_values(sc, n):
    vals = []
    cur = sc
    for _ in range(n):
        m = jnp.max(cur, axis=0, keepdims=True)
        vals.append(m)
        cur = jnp.where(cur == m, -jnp.inf, cur)
    return vals


def _sorting_network(n):
    pairs = []
    p = 1
    while p < n:
        k = p
        while k >= 1:
            for j in range(k % p, n - k, 2 * k):
                for i in range(min(k, n - j - k)):
                    if (i + j) // (2 * p) == (i + j + k) // (2 * p):
                        pairs.append((i + j, i + j + k))
            k //= 2
        p *= 2
    return pairs


def _top_values_sorted(sc, n):
    nslab = sc.shape[0] // SUBLANES
    cols = [sc[v * SUBLANES:(v + 1) * SUBLANES, :] for v in range(nslab)]
    for i, j in _sorting_network(nslab):
        cols[i], cols[j] = jnp.maximum(cols[i], cols[j]), jnp.minimum(cols[i], cols[j])
    vals = []
    for r in range(n):
        m = jnp.max(cols[0], axis=0, keepdims=True)
        vals.append(m)
        hit = cols[0] == m
        for k in range(min(nslab, n - 1 - r)):
            nxt = cols[k + 1] if k + 1 < nslab else -jnp.inf
            cols[k] = jnp.where(hit, nxt, cols[k])
    return vals


def _peer_route_kernel(x_ref, wq_ref, keys_ref, xb_ref, thr_ref, e1_ref, s2_ref, e2_ref, *, heads, topk):
    xb = x_ref[...].astype(BF16)
    xb_ref[...] = xb
    q = _dot(xb, wq_ref[...])
    nk = keys_ref.shape[1]
    for h in range(heads):
        s1 = _dot_nt(keys_ref[2 * h], q[:, 2 * h * nk:(2 * h + 1) * nk].astype(BF16))
        s2 = _dot_nt(keys_ref[2 * h + 1], q[:, (2 * h + 1) * nk:(2 * h + 2) * nk].astype(BF16))
        a = _top_values_sorted(s1, topk + 1)
        b = _top_values_sorted(s2, topk + 1)
        cands = [a[i] + b[j] for i in range(topk + 1) for j in range(topk + 1) if (i + 1) * (j + 1) <= topk + 1]
        best = _top_values(jnp.concatenate(cands, axis=0), topk + 1)
        zsum = None
        for r in range(topk):
            e = jnp.exp(best[r] - best[0])
            zsum = e if zsum is None else zsum + e
        thr = 0.5 * (best[topk - 1] + best[topk]) - s1
        outs = ((thr_ref, thr), (e1_ref, jnp.exp(s1 - a[0]) / zsum), (s2_ref, s2), (e2_ref, jnp.exp(s2 - b[0])))
        for ref, val in outs:
            for tg in range(val.shape[1] // LANES):
                ref[0, h, tg] = val[:, tg * LANES:(tg + 1) * LANES]


def _peer_route(x2d, wq, keys, heads, ts):
    t, d = x2d.shape
    nk = keys.shape[1]
    nt = t // ts
    shape = (nt, heads, ts // LANES, nk, LANES)
    ospec = pl.BlockSpec((1,) + shape[1:], lambda i: (i, 0, 0, 0, 0))
    return pl.pallas_call(
        functools.partial(_peer_route_kernel, heads=heads, topk=PEER_TOPK),
        grid=(nt,),
        in_specs=[pl.BlockSpec((ts, d), lambda i: (i, 0)),
                  pl.BlockSpec(wq.shape, lambda i: (0, 0)),
                  pl.BlockSpec(keys.shape, lambda i: (0, 0, 0))],
        out_specs=[pl.BlockSpec((ts, d), lambda i: (i, 0)), ospec, ospec, ospec, ospec],
        out_shape=[jax.ShapeDtypeStruct((t, d), BF16)] + [jax.ShapeDtypeStruct(shape, F32)] * 4,
        compiler_params=_cparams("parallel"),
        name="peer_route",
    )(x2d, wq, keys)


def _gelu(x):
    return 0.5 * x * (1.0 + lax.erf(x * (2.0 ** -0.5)))


GATE_I = 2
GATE_ROWS = 64


def _peer_dense_kernel(x_ref, xb_ref, u_ref, vt_ref, thr_ref, e1_ref, s2_ref, e2_ref, g_ref, b_ref, o_ref,
                       acc_ref, act_ref, hid_ref, *, heads, nk, eb, alpha):
    e = pl.program_id(1)
    ts = xb_ref.shape[0]
    ni = eb // nk
    ntg = ts // LANES

    @pl.when(e == 0)
    def _():
        acc_ref[...] = jnp.zeros(acc_ref.shape, F32)

    hid = _gelu(_dot_nt(u_ref[...], xb_ref[...]))
    for il in range(ni):
        for tg in range(ntg):
            hid_ref[il, tg] = hid[il * nk:(il + 1) * nk, tg * LANES:(tg + 1) * LANES]

    def token_group(tg, carry):
        for ip in range(ni // GATE_I):
            for jc in range(nk // GATE_ROWS):
                js = slice(jc * GATE_ROWS, (jc + 1) * GATE_ROWS)
                gates = [None] * GATE_I
                for h in range(heads):
                    s2 = s2_ref[0, h, tg, js, :]
                    e2 = e2_ref[0, h, tg, js, :]
                    for a in range(GATE_I):
                        il = ip * GATE_I + a
                        thr = thr_ref[0, h, tg, il:il + 1, :]
                        e1 = e1_ref[0, h, tg, il:il + 1, :]
                        term = jnp.where(s2 >= thr, e2, 0.0) * e1
                        gates[a] = term if gates[a] is None else gates[a] + term
                for a in range(GATE_I):
                    il = ip * GATE_I + a
                    act = gates[a] * hid_ref[il, tg, js, :]
                    act_ref[tg, il * nk + jc * GATE_ROWS:il * nk + (jc + 1) * GATE_ROWS, :] = act.astype(BF16)
        return carry

    lax.fori_loop(0, ntg, token_group, 0)
    act = jnp.concatenate([act_ref[tg] for tg in range(ntg)], axis=1)
    acc_ref[...] += _dot(vt_ref[...], act)

    @pl.when(e == pl.num_programs(1) - 1)
    def _():
        o_ref[...] = _layer_norm(alpha * x_ref[...] + acc_ref[...].T, g_ref[...], b_ref[...])


def _peer_dense(x2d, xb, u, vt, layer, thr, e1, s2, e2, g, b, alpha, heads, ts, eb):
    t, d = x2d.shape
    n = u.shape[1]
    nk = s2.shape[3]
    ntg = ts // LANES
    assert (eb // nk) % GATE_I == 0 and nk % GATE_ROWS == 0
    jspec = pl.BlockSpec((1, heads, ntg, nk, LANES), lambda i, e: (i, 0, 0, 0, 0))
    ispec = pl.BlockSpec((1, heads, ntg, eb // nk, LANES), lambda i, e: (i, 0, 0, e, 0))
    return pl.pallas_call(
        functools.partial(_peer_dense_kernel, heads=heads, nk=nk, eb=eb, alpha=alpha),
        grid=(t // ts, n // eb),
        in_specs=[pl.BlockSpec((ts, d), lambda i, e: (i, 0)),
                  pl.BlockSpec((ts, d), lambda i, e: (i, 0)),
                  pl.BlockSpec((None, eb, d), lambda i, e: (layer, e, 0)),
                  pl.BlockSpec((None, d, eb), lambda i, e: (layer, 0, e)),
                  ispec, ispec, jspec, jspec,
                  pl.BlockSpec((1, d), lambda i, e: (0, 0)),
                  pl.BlockSpec((1, d), lambda i, e: (0, 0))],
        out_specs=pl.BlockSpec((ts, d), lambda i, e: (i, 0)),
        out_shape=jax.ShapeDtypeStruct((t, d), F32),
        scratch_shapes=[pltpu.VMEM((d, ts), F32), pltpu.VMEM((ntg, eb, LANES), BF16),
                        pltpu.VMEM((eb // nk, ntg, nk, LANES), F32)],
        compiler_params=_cparams("parallel", "arbitrary"),
        name="peer_dense",
    )(x2d, xb, u, vt, thr, e1, s2, e2, g, b)


def _peer(h2d, w_q, sub_keys, u_all, vt_all, layer, g, b, alpha):
    t, d = h2d.shape
    heads, _, nk, dh = sub_keys.shape
    assert nk == LANES and dh == LANES
    ts = min(512, t)
    eb = min(2048, u_all.shape[1])
    keys = sub_keys.reshape(heads * 2, nk, dh).astype(BF16)
    xb, thr, e1, s2, e2 = _peer_route(h2d, w_q.astype(BF16), keys, heads, ts)
    return _peer_dense(h2d, xb, u_all, vt_all, layer, thr, e1, s2, e2,
                       g.reshape(1, d), b.reshape(1, d), alpha, heads, ts, eb)


def _gdn_layer(h2d, batch, seq, w_in, conv_w, a_log, dt_bias, norm_g, w_out, g, b, alpha):
    t, d = h2d.shape
    heads = a_log.shape[0]
    dv = norm_g.shape[0]
    nv = heads * dv
    nqk = (conv_w.shape[1] - nv) // 2
    assert nqk // heads == LANES and dv == LANES and 2 * heads <= LANES
    chunk = min(GDN_KERNEL_CHUNK, seq)
    ts = min(512, seq)
    tc = min(512, seq)
    cw = 2 * nqk + nv
    wqkv = w_in[:, :cw].astype(BF16)
    wz = w_in[:, cw:cw + nv].astype(BF16)
    wbg = jnp.pad(w_in[:, cw + nv:], ((0, 0), (0, LANES - 2 * heads))).astype(BF16)
    alog = jnp.pad(a_log, (heads, LANES - 2 * heads)).reshape(1, LANES)
    dtb = jnp.pad(dt_bias, (heads, LANES - 2 * heads)).reshape(1, LANES)
    q, k, v, z, bg = _gdn_proj(h2d, wqkv, wz, wbg, conv_w, alog, dtb, batch, seq, heads, nqk, nv, ts, chunk)
    o = _gdn_chunk(q, k, v, z, bg, norm_g.reshape(1, dv), batch, seq, heads, tc, chunk)
    return _proj_ln(o, w_out.astype(BF16), h2d, g.reshape(1, d), b.reshape(1, d), alpha, min(512, t))


def _head_pad(w, heads, width):
    r = w.shape[0]
    return jnp.pad(w.reshape(r, heads, width), ((0, 0), (0, 0), (0, LANES - width))).reshape(r, heads * LANES)


def _rope_swap(w, half):
    return jnp.concatenate([-w[..., half:], w[..., :half]], -1)


def _mla_layer(h2d, positions, batch, seq, shared_kv, kv_w_a, kv_norm_g, kv_w_uk, kv_w_uv, w_qa, q_norm_g, w_qb,
               w_out, g, b, alpha):
    t, d = h2d.shape
    rank = kv_norm_g.shape[0]
    rope = kv_w_a.shape[1] - rank
    qrank = w_qa.shape[1]
    heads = (w_qb.shape[1] - kv_w_uk.shape[1]) // rope
    nope = kv_w_uk.shape[1] // heads
    vdim = kv_w_uv.shape[1] // heads
    half = rope // 2
    assert nope + rope <= LANES and LANES % vdim == 0 and vdim < LANES and rank % LANES == 0

    inv_freq = ROPE_THETA ** (-jnp.arange(half, dtype=F32) / half)
    invf = jnp.zeros((LANES,), F32).at[nope:nope + half].set(inv_freq).at[nope + half:nope + rope].set(inv_freq)
    posf = positions.astype(F32).reshape(t, 1)

    kr = kv_w_a[:, rank:]
    place = lambda w: jnp.pad(w, ((0, 0), (nope, LANES - nope - rope)))
    wkva = jnp.concatenate([kv_w_a[:, :rank], place(kr), place(_rope_swap(kr, half))], -1).astype(BF16)
    wuk = _head_pad(kv_w_uk, heads, nope).astype(BF16)
    wq3 = w_qb.reshape(qrank, heads, nope + rope)
    wqb = _head_pad(w_qb, heads, nope + rope).astype(BF16)
    wq_sw = jnp.concatenate([jnp.zeros((qrank, heads, nope), F32), _rope_swap(wq3[..., nope:], half)], -1)
    wqbs = _head_pad(wq_sw.reshape(qrank, heads * (nope + rope)), heads, nope + rope).astype(BF16)

    hpp = LANES // vdim
    wuv3 = kv_w_uv.reshape(rank, heads, vdim)
    wuv = jnp.concatenate([jnp.pad(wuv3[:, h], ((0, 0), ((h % hpp) * vdim, LANES - (h % hpp + 1) * vdim)))
                           for h in range(heads)], -1).astype(BF16)
    ones_at = jnp.array([h * LANES + ((h % hpp + 1) * vdim) % LANES for h in range(heads)])
    vones = jnp.zeros((1, heads * LANES), F32).at[0, ones_at].set(1.0)

    ts = min(512, t)
    q, k, v = _mla_prep(h2d, posf, invf.reshape(1, LANES), wkva, kv_norm_g.reshape(1, rank), wuk, wuv, vones,
                        w_qa.astype(BF16), q_norm_g.reshape(1, qrank), wqb, wqbs, heads, rank, nope, rope, ts)
    if shared_kv is None:
        shared_kv = (k, v)
    tq = min(512, seq)
    o = _flash(q, shared_kv[0], shared_kv[1], batch, seq, heads, vdim, tq, tq)
    h_new = _proj_ln(o, w_out.astype(BF16), h2d, g.reshape(1, d), b.reshape(1, d), alpha, min(512, t))
    return h_new, shared_kv


def kernel(x, positions, a_w_in, a_conv_w, a_a_log, a_dt_bias, a_norm_g, a_w_out, kv_w_a, kv_norm_g, kv_w_uk,
           kv_w_uv, b_w_qa, b_q_norm_g, b_w_qb, b_w_out, peer_w_q, peer_sub_keys, peer_u, peer_v, ln_g, ln_b):
    batch, seq, d = x.shape
    depth = ln_g.shape[0]
    n_a = a_w_in.shape[0]
    alpha = (2 * depth) ** 0.25
    h = x.reshape(batch * seq, d)
    u_all = peer_u.astype(BF16)
    vt_all = jnp.swapaxes(peer_v.astype(BF16), 1, 2)
    shared_kv = None
    for layer in range(depth):
        if layer < n_a:
            i = layer
            h = _gdn_layer(h, batch, seq, a_w_in[i], a_conv_w[i], a_a_log[i], a_dt_bias[i], a_norm_g[i],
                           a_w_out[i], ln_g[layer, 0], ln_b[layer, 0], alpha)
        else:
            j = layer - n_a
            h, shared_kv = _mla_layer(h, positions, batch, seq, shared_kv, kv_w_a, kv_norm_g, kv_w_uk, kv_w_uv,
                                      b_w_qa[j], b_q_norm_g[j], b_w_qb[j], b_w_out[j],
                                      ln_g[layer, 0], ln_b[layer, 0], alpha)
        h = _peer(h, peer_w_q[layer], peer_sub_keys[layer], u_all, vt_all, layer,
                  ln_g[layer, 1], ln_b[layer, 1], alpha)
    return h.reshape(batch, seq, d)
```

```python
import functools
import math

import jax
import jax.numpy as jnp
from jax import lax
from jax.experimental import pallas as pl
from jax.experimental.pallas import tpu as pltpu

F32 = jnp.float32
BF16 = jnp.bfloat16
HIGHEST = lax.Precision.HIGHEST

LN_EPS = 1e-5
RMS_EPS = 1e-6
ROPE_THETA = 10000.0
GDN_CONV = 4
PEER_TOPK = 16
LANES = 128
SUBLANES = 8
GDN_KERNEL_CHUNK = 128
GDN_CHUNKS_PER_ITER = 2
VMEM_LIMIT_BYTES = 56 * 1024 * 1024


def _cparams(*sem):
    return pltpu.CompilerParams(dimension_semantics=sem, vmem_limit_bytes=VMEM_LIMIT_BYTES)


def _dot(a, b):
    return jnp.dot(a, b, preferred_element_type=F32)


def _dot_nt(a, b):
    return lax.dot_general(a, b, (((1,), (1,)), ((), ())), preferred_element_type=F32)


def _dot_tn(a, b):
    return lax.dot_general(a, b, (((0,), (0,)), ((), ())), preferred_element_type=F32)


def _dot_hi(a, b):
    return jnp.dot(a, b, precision=HIGHEST, preferred_element_type=F32)


def _sigmoid(x):
    return 1.0 / (1.0 + jnp.exp(-x))


def _softplus(x):
    return jnp.maximum(x, 0.0) + jnp.log1p(jnp.exp(-jnp.abs(x)))


def _layer_norm(y, g, b):
    mu = jnp.mean(y, -1, keepdims=True)
    d = y - mu
    var = jnp.mean(d * d, -1, keepdims=True)
    return d * lax.rsqrt(var + LN_EPS) * g + b


def _rms(x, g):
    return x * lax.rsqrt(jnp.mean(x * x, -1, keepdims=True) + RMS_EPS) * g


def _proj_ln_kernel(a_ref, w_ref, res_ref, g_ref, b_ref, o_ref, *, alpha):
    y = alpha * res_ref[...] + _dot(a_ref[...], w_ref[...])
    o_ref[...] = _layer_norm(y, g_ref[...], b_ref[...])


def _proj_ln(a, w, res, g, b, alpha, tm):
    t, k = a.shape
    d = w.shape[1]
    return pl.pallas_call(
        functools.partial(_proj_ln_kernel, alpha=alpha),
        grid=(t // tm,),
        in_specs=[pl.BlockSpec((tm, k), lambda i: (i, 0)),
                  pl.BlockSpec((k, d), lambda i: (0, 0)),
                  pl.BlockSpec((tm, d), lambda i: (i, 0)),
                  pl.BlockSpec((1, d), lambda i: (0, 0)),
                  pl.BlockSpec((1, d), lambda i: (0, 0))],
        out_specs=pl.BlockSpec((tm, d), lambda i: (i, 0)),
        out_shape=jax.ShapeDtypeStruct((t, d), F32),
        compiler_params=_cparams("parallel"),
        name="proj_ln",
    )(a, w, res, g, b)


def _gdn_proj_kernel(x_ref, wqkv_ref, wz_ref, wbg_ref, convw_ref, alog_ref, dtb_ref, tri_ref,
                     q_ref, k_ref, v_ref, z_ref, bg_ref, buf_ref, *, ts, nqk, nv, heads, chunk):
    s = pl.program_id(1)
    halo = SUBLANES
    xb = x_ref[...].astype(BF16)

    @pl.when(s == 0)
    def _():
        buf_ref[0:halo, :] = jnp.zeros((halo, buf_ref.shape[1]), F32)

    buf_ref[halo:halo + ts, :] = _dot(xb, wqkv_ref[...])
    z_ref[...] = _dot(xb, wz_ref[...]).astype(BF16)

    dk = nqk // heads
    for grp in range((2 * nqk + nv) // LANES):
        c0 = grp * LANES
        acc = None
        for j in range(GDN_CONV):
            r0 = halo - (GDN_CONV - 1) + j
            term = buf_ref[r0:r0 + ts, c0:c0 + LANES] * convw_ref[j:j + 1, c0:c0 + LANES]
            acc = term if acc is None else acc + term
        y = acc * _sigmoid(acc)
        if c0 < 2 * nqk:
            y = y * lax.rsqrt(jnp.sum(y * y, -1, keepdims=True) + RMS_EPS)
        if c0 < nqk:
            q_ref[:, c0:c0 + LANES] = (y * dk ** -0.5).astype(BF16)
        elif c0 < 2 * nqk:
            k_ref[:, c0 - nqk:c0 - nqk + LANES] = y.astype(BF16)
        else:
            v_ref[:, c0 - 2 * nqk:c0 - 2 * nqk + LANES] = y.astype(BF16)

    buf_ref[0:halo, :] = buf_ref[ts:ts + halo, :]

    bgp = _dot(xb, wbg_ref[...])
    lane = lax.broadcasted_iota(jnp.int32, (chunk, LANES), 1)
    for c in range(ts // chunk):
        blk = bgp[c * chunk:(c + 1) * chunk, :]
        beta = _sigmoid(blk)
        g = -jnp.exp(alog_ref[...]) * _softplus(blk + dtb_ref[...])
        gc = _dot_hi(tri_ref[...], g)
        bg_ref[c * chunk:(c + 1) * chunk, :] = jnp.where(lane < heads, beta, gc)


def _gdn_proj(x2d, wqkv, wz, wbg, convw, alog, dtb, batch, seq, heads, nqk, nv, ts, chunk):
    t, d = x2d.shape
    ns = seq // ts
    cw = 2 * nqk + nv
    tri = jnp.tril(jnp.ones((chunk, chunk), F32))
    row = lambda b, s: (b * ns + s, 0)
    fixed = lambda b, s: (0, 0)
    return pl.pallas_call(
        functools.partial(_gdn_proj_kernel, ts=ts, nqk=nqk, nv=nv, heads=heads, chunk=chunk),
        grid=(batch, ns),
        in_specs=[pl.BlockSpec((ts, d), row),
                  pl.BlockSpec((d, cw), fixed),
                  pl.BlockSpec((d, nv), fixed),
                  pl.BlockSpec((d, LANES), fixed),
                  pl.BlockSpec((GDN_CONV, cw), fixed),
                  pl.BlockSpec((1, LANES), fixed),
                  pl.BlockSpec((1, LANES), fixed),
                  pl.BlockSpec((chunk, chunk), fixed)],
        out_specs=[pl.BlockSpec((ts, nqk), row),
                   pl.BlockSpec((ts, nqk), row),
                   pl.BlockSpec((ts, nv), row),
                   pl.BlockSpec((ts, nv), row),
                   pl.BlockSpec((ts, LANES), row)],
        out_shape=[jax.ShapeDtypeStruct((t, nqk), BF16),
                   jax.ShapeDtypeStruct((t, nqk), BF16),
                   jax.ShapeDtypeStruct((t, nv), BF16),
                   jax.ShapeDtypeStruct((t, nv), BF16),
                   jax.ShapeDtypeStruct((t, LANES), F32)],
        scratch_shapes=[pltpu.VMEM((ts + 2 * SUBLANES, cw), F32)],
        compiler_params=_cparams("parallel", "arbitrary"),
        name="gdn_proj",
    )(x2d, wqkv, wz, wbg, convw, alog, dtb, tri)


def _gdn_chunk_kernel(q_ref, k_ref, v_ref, z_ref, bg_ref, ng_ref, o_ref, state_ref, *, tc, heads, chunk):
    @pl.when(pl.program_id(1) == 0)
    def _():
        state_ref[...] = jnp.zeros(state_ref.shape, F32)

    rows = lax.broadcasted_iota(jnp.int32, (chunk, chunk), 0)
    cols = lax.broadcasted_iota(jnp.int32, (chunk, chunk), 1)
    tril = rows >= cols
    strict = rows > cols
    eye = jnp.where(rows == cols, 1.0, 0.0).astype(F32)
    hr = range(heads)

    per = math.gcd(tc // chunk, GDN_CHUNKS_PER_ITER)

    def body(it, carry):
        r0s = [pl.multiple_of((it * per + cc) * chunk, chunk) for cc in range(per)]
        units = [(cc, h) for cc in range(per) for h in hr]
        ur = range(len(units))
        bgs = [bg_ref[pl.ds(r0, chunk), :] for r0 in r0s]
        hs = [slice(h * LANES, (h + 1) * LANES) for _, h in units]
        rows = [pl.ds(r0s[cc], chunk) for cc, _ in units]
        beta = [bgs[cc][:, h:h + 1] for cc, h in units]
        gcol = [bgs[cc][:, heads + h:heads + h + 1] for cc, h in units]
        glast = [g[chunk - 1:chunk, :] for g in gcol]
        eg = [jnp.exp(g) for g in gcol]
        decay = []
        for u in ur:
            gmat = jnp.broadcast_to(gcol[u], (chunk, chunk))
            decay.append(jnp.where(tril, jnp.exp(jnp.where(tril, gmat - gmat.T, 0.0)), 0.0))
        qb = [q_ref[rows[u], hs[u]] for u in ur]
        kb = [k_ref[rows[u], hs[u]] for u in ur]
        kf = [k.astype(F32) for k in kb]
        kbeta = [kf[u] * beta[u] for u in ur]
        vbeta = [v_ref[rows[u], hs[u]].astype(F32) * beta[u] for u in ur]

        kk = [_dot_nt(jnp.concatenate([kbeta[u].astype(BF16), qb[u]], axis=0), kb[u]) for u in ur]
        low = [jnp.where(strict, kk[u][:chunk] * decay[u], 0.0) for u in ur]
        attn = [(kk[u][chunk:] * decay[u]).astype(BF16) for u in ur]

        inv = [eye - l for l in low]
        power = low
        span = 1
        while 2 * span < chunk:
            pb = [p.astype(BF16) for p in power]
            power = [_dot(p, p) for p in pb]
            inv = [inv[u] + _dot(inv[u].astype(BF16), power[u].astype(BF16)) for u in ur]
            span *= 2

        rhs = [jnp.concatenate([vbeta[u], kbeta[u] * eg[u]], axis=1).astype(BF16) for u in ur]
        sol = [_dot(inv[u].astype(BF16), rhs[u]) for u in ur]
        dv = vbeta[0].shape[1]

        st = [state_ref[h] for h in hr]
        for cc in range(per):
            us = [cc * heads + h for h in hr]
            lhs = [jnp.concatenate([sol[u][:, dv:], qb[u].astype(F32) * eg[u]], axis=0).astype(BF16) for u in us]
            ws = [_dot(lhs[h], st[h].astype(BF16)) for h in hr]
            vnb = [(sol[us[h]][:, :dv] - ws[h][:chunk]).astype(BF16) for h in hr]
            o = [ws[h][chunk:] + _dot(attn[us[h]], vnb[h]) for h in hr]
            new_st = []
            for h in hr:
                u = us[h]
                kdec = (kf[u] * jnp.exp(glast[u] - gcol[u])).astype(BF16)
                new_st.append(st[h] * jnp.exp(glast[u]) + _dot_tn(kdec, vnb[h]))
            st = new_st
            for h in hr:
                u = us[h]
                zf = z_ref[rows[u], hs[u]].astype(F32)
                on = _rms(o[h], ng_ref[...]) * (zf * _sigmoid(zf))
                o_ref[rows[u], hs[u]] = on.astype(BF16)
        for h in hr:
            state_ref[h] = st[h]
        return carry

    lax.fori_loop(0, tc // chunk // per, body, 0)


def _gdn_chunk(q, k, v, z, bg, norm_g, batch, seq, heads, tc, chunk):
    t, nqk = q.shape
    nv = v.shape[1]
    ns = seq // tc
    row = lambda b, s: (b * ns + s, 0)
    return pl.pallas_call(
        functools.partial(_gdn_chunk_kernel, tc=tc, heads=heads, chunk=chunk),
        grid=(batch, ns),
        in_specs=[pl.BlockSpec((tc, nqk), row),
                  pl.BlockSpec((tc, nqk), row),
                  pl.BlockSpec((tc, nv), row),
                  pl.BlockSpec((tc, nv), row),
                  pl.BlockSpec((tc, LANES), row),
                  pl.BlockSpec((1, LANES), lambda b, s: (0, 0))],
        out_specs=pl.BlockSpec((tc, nv), row),
        out_shape=jax.ShapeDtypeStruct((t, nv), BF16),
        scratch_shapes=[pltpu.VMEM((heads, nqk // heads, nv // heads), F32)],
        compiler_params=_cparams("parallel", "arbitrary"),
        name="gdn_chunk",
    )(q, k, v, z, bg, norm_g)


def _mla_prep_kernel(x_ref, pos_ref, invf_ref, wkva_ref, gkv_ref, wuk_ref, wuv_ref, vones_ref, wqa_ref, gq_ref,
                     wqb_ref, wqbs_ref, q_ref, k_ref, v_ref, *, heads, rank, nope, rope, scale):
    xb = x_ref[...].astype(BF16)
    ts = xb.shape[0]
    groups = LANES // rope
    rb = ts // groups
    lane = lax.broadcasted_iota(jnp.int32, (rb, LANES), 1)
    pos_c = jnp.zeros((rb, LANES), F32)
    for g in range(groups):
        in_group = (lane >= g * rope) & (lane < (g + 1) * rope)
        pos_c = jnp.where(in_group, pos_ref[g * rb:(g + 1) * rb, :], pos_c)
    ang = pos_c * invf_ref[...]
    cos_c, sin_c = jnp.cos(ang), jnp.sin(ang)
    is_rope = (lane >= nope) & (lane < nope + rope)
    cos_blocks, sin_blocks = [], []
    for g in range(groups):
        shift = (nope - g * rope) % LANES
        cos_g = pltpu.roll(cos_c, shift, 1) if shift else cos_c
        sin_g = pltpu.roll(sin_c, shift, 1) if shift else sin_c
        cos_blocks.append(jnp.where(lane < nope, 1.0, jnp.where(is_rope, cos_g, 0.0)))
        sin_blocks.append(jnp.where(is_rope, sin_g, 0.0))
    cosm = jnp.concatenate(cos_blocks, axis=0)
    sinm = jnp.concatenate(sin_blocks, axis=0)

    kv = _dot(xb, wkva_ref[...])
    ckv = _rms(kv[:, :rank], gkv_ref[...]).astype(BF16)
    krope = kv[:, rank:rank + LANES] * cosm + kv[:, rank + LANES:rank + 2 * LANES] * sinm
    v_ref[...] = (_dot(ckv, wuv_ref[...]) + vones_ref[...]).astype(BF16)
    knope = _dot(ckv, wuk_ref[...])

    qn = _rms(_dot(xb, wqa_ref[...]), gq_ref[...]).astype(BF16)
    qa = _dot(qn, wqb_ref[...])
    qs = _dot(qn, wqbs_ref[...])
    for h in range(heads):
        hs = slice(h * LANES, (h + 1) * LANES)
        k_ref[:, hs] = (knope[:, hs] + krope).astype(BF16)
        q_ref[:, hs] = ((qa[:, hs] * cosm + qs[:, hs] * sinm) * scale).astype(BF16)


def _mla_prep(x2d, posf, invf, wkva, gkv, wuk, wuv, vones, wqa, gq, wqb, wqbs, heads, rank, nope, rope, ts):
    t, d = x2d.shape
    scale = (nope + rope) ** -0.5
    row = lambda i: (i, 0)
    fixed = lambda i: (0, 0)
    full = lambda a: pl.BlockSpec(a.shape, fixed)
    ospec = pl.BlockSpec((ts, heads * LANES), row)
    return pl.pallas_call(
        functools.partial(_mla_prep_kernel, heads=heads, rank=rank, nope=nope, rope=rope, scale=scale),
        grid=(t // ts,),
        in_specs=[pl.BlockSpec((ts, d), row), pl.BlockSpec((ts, 1), row), full(invf), full(wkva), full(gkv),
                  full(wuk), full(wuv), full(vones), full(wqa), full(gq), full(wqb), full(wqbs)],
        out_specs=[ospec, ospec, ospec],
        out_shape=[jax.ShapeDtypeStruct((t, heads * LANES), BF16)] * 3,
        compiler_params=_cparams("parallel"),
        name="mla_prep",
    )(x2d, posf, invf, wkva, gkv, wuk, wuv, vones, wqa, gq, wqb, wqbs)


def _flash_kernel(q_ref, k_ref, v_ref, o_ref, *, tq, tk, vdim):
    qi = pl.program_id(2)
    hpp = LANES // vdim
    lane = lax.broadcasted_iota(jnp.int32, (tq, LANES), 1)
    qs = [q_ref[:, hh * LANES:(hh + 1) * LANES] for hh in range(hpp)]

    def chunk(c, carry, masked):
        k0 = pl.multiple_of(c * tk, tk)
        ss = [_dot_nt(qs[hh], k_ref[pl.ds(k0, tk), hh * LANES:(hh + 1) * LANES]) for hh in range(hpp)]
        if masked:
            keep = lax.broadcasted_iota(jnp.int32, (tq, tk), 1) <= lax.broadcasted_iota(jnp.int32, (tq, tk), 0)
            ss = [jnp.where(keep, s, -jnp.inf) for s in ss]
        out = []
        for hh in range(hpp):
            m, acc = carry[hh]
            m_new = jnp.maximum(m, jnp.max(ss[hh], -1, keepdims=True))
            p = jnp.exp((ss[hh] - m_new).astype(BF16))
            acc = jnp.exp(m - m_new) * acc + _dot(p, v_ref[pl.ds(k0, tk), hh * LANES:(hh + 1) * LANES])
            out.append((m_new, acc))
        return tuple(out)

    init = tuple((jnp.full((tq, 1), -jnp.inf, F32), jnp.zeros((tq, LANES), F32)) for _ in range(hpp))
    carry = lax.fori_loop(0, qi, functools.partial(chunk, masked=False), init)
    carry = chunk(qi, carry, masked=True)
    out = jnp.zeros((tq, LANES), F32)
    for hh in range(hpp):
        acc = carry[hh][1]
        ones_lane = ((hh + 1) * vdim) % LANES
        out = jnp.where((lane >= hh * vdim) & (lane < (hh + 1) * vdim), acc / acc[:, ones_lane:ones_lane + 1], out)
    o_ref[...] = out.astype(BF16)


def _flash(q, k, v, batch, seq, heads, vdim, tq, tk):
    t = q.shape[0]
    hpp = LANES // vdim
    nq = seq // tq
    assert tq == tk
    return pl.pallas_call(
        functools.partial(_flash_kernel, tq=tq, tk=tk, vdim=vdim),
        grid=(batch, heads // hpp, nq),
        in_specs=[pl.BlockSpec((tq, hpp * LANES), lambda b, h, i: (b * nq + i, h)),
                  pl.BlockSpec((seq, hpp * LANES), lambda b, h, i: (b, h)),
                  pl.BlockSpec((seq, hpp * LANES), lambda b, h, i: (b, h))],
        out_specs=pl.BlockSpec((tq, LANES), lambda b, h, i: (b * nq + i, h)),
        out_shape=jax.ShapeDtypeStruct((t, heads * vdim), BF16),
        compiler_params=_cparams("parallel", "parallel", "arbitrary"),
        name="mla_flash",
    )(q, k, v)


def _top_values(sc, n):
    vals = []
    cur = sc
    for _ in range(n):
        m = jnp.max(cur, axis=0, keepdims=True)
        vals.append(m)
        cur = jnp.where(cur == m, -jnp.inf, cur)
    return vals


def _sorting_network(n):
    pairs = []
    p = 1
    while p < n:
        k = p
        while k >= 1:
            for j in range(k % p, n - k, 2 * k):
                for i in range(min(k, n - j - k)):
                    if (i + j) // (2 * p) == (i + j + k) // (2 * p):
                        pairs.append((i + j, i + j + k))
            k //= 2
        p *= 2
    return pairs


def _top_values_sorted(sc, n):
    nslab = sc.shape[0] // SUBLANES
    cols = [sc[v * SUBLANES:(v + 1) * SUBLANES, :] for v in range(nslab)]
    for i, j in _sorting_network(nslab):
        cols[i], cols[j] = jnp.maximum(cols[i], cols[j]), jnp.minimum(cols[i], cols[j])
    vals = []
    for r in range(n):
        m = jnp.max(cols[0], axis=0, keepdims=True)
        vals.append(m)
        hit = cols[0] == m
        for k in range(min(nslab, n - 1 - r)):
            nxt = cols[k + 1] if k + 1 < nslab else -jnp.inf
            cols[k] = jnp.where(hit, nxt, cols[k])
    return vals


def _peer_route_kernel(x_ref, wq_ref, keys_ref, xb_ref, thr_ref, e1_ref, s2_ref, e2_ref, *, heads, topk):
    xb = x_ref[...].astype(BF16)
    xb_ref[...] = xb
    q = _dot(xb, wq_ref[...])
    nk = keys_ref.shape[1]
    for h in range(heads):
        s1 = _dot_nt(keys_ref[2 * h], q[:, 2 * h * nk:(2 * h + 1) * nk].astype(BF16))
        s2 = _dot_nt(keys_ref[2 * h + 1], q[:, (2 * h + 1) * nk:(2 * h + 2) * nk].astype(BF16))
        a = _top_values_sorted(s1, topk + 1)
        b = _top_values_sorted(s2, topk + 1)
        cands = [a[i] + b[j] for i in range(topk + 1) for j in range(topk + 1) if (i + 1) * (j + 1) <= topk + 1]
        best = _top_values(jnp.concatenate(cands, axis=0), topk + 1)
        zsum = None
        for r in range(topk):
            e = jnp.exp(best[r] - best[0])
            zsum = e if zsum is None else zsum + e
        thr = 0.5 * (best[topk - 1] + best[topk]) - s1
        outs = ((thr_ref, thr), (e1_ref, jnp.exp(s1 - a[0]) / zsum), (s2_ref, s2), (e2_ref, jnp.exp(s2 - b[0])))
        for ref, val in outs:
            for tg in range(val.shape[1] // LANES):
                ref[0, h, tg] = val[:, tg * LANES:(tg + 1) * LANES]


def _peer_route(x2d, wq, keys, heads, ts):
    t, d = x2d.shape
    nk = keys.shape[1]
    nt = t // ts
    shape = (nt, heads, ts // LANES, nk, LANES)
    ospec = pl.BlockSpec((1,) + shape[1:], lambda i: (i, 0, 0, 0, 0))
    return pl.pallas_call(
        functools.partial(_peer_route_kernel, heads=heads, topk=PEER_TOPK),
        grid=(nt,),
        in_specs=[pl.BlockSpec((ts, d), lambda i: (i, 0)),
                  pl.BlockSpec(wq.shape, lambda i: (0, 0)),
                  pl.BlockSpec(keys.shape, lambda i: (0, 0, 0))],
        out_specs=[pl.BlockSpec((ts, d), lambda i: (i, 0)), ospec, ospec, ospec, ospec],
        out_shape=[jax.ShapeDtypeStruct((t, d), BF16)] + [jax.ShapeDtypeStruct(shape, F32)] * 4,
        compiler_params=_cparams("parallel"),
        name="peer_route",
    )(x2d, wq, keys)


def _gelu_x2(x):
    return x * (1.0 + lax.erf(x * (2.0 ** -0.5)))


GATE_I = 2
GATE_ROWS = 64


def _peer_dense_kernel(x_ref, xb_ref, u_ref, vt_ref, thr_ref, e1_ref, s2_ref, e2_ref, g_ref, b_ref, o_ref,
                       acc_ref, act_ref, hid_ref, *, heads, nk, eb, alpha):
    e = pl.program_id(1)
    ts = xb_ref.shape[0]
    ni = eb // nk
    ntg = ts // LANES

    @pl.when(e == 0)
    def _():
        acc_ref[...] = jnp.zeros(acc_ref.shape, F32)

    hid = _gelu_x2(_dot_nt(u_ref[...], xb_ref[...]))
    for il in range(ni):
        for tg in range(ntg):
            hid_ref[il, tg] = hid[il * nk:(il + 1) * nk, tg * LANES:(tg + 1) * LANES]

    def token_group(tg, carry):
        for ip in range(ni // GATE_I):
            for jc in range(nk // GATE_ROWS):
                js = slice(jc * GATE_ROWS, (jc + 1) * GATE_ROWS)
                gates = [None] * GATE_I
                for h in range(heads):
                    s2 = s2_ref[0, h, tg, js, :]
                    e2 = e2_ref[0, h, tg, js, :]
                    for a in range(GATE_I):
                        il = ip * GATE_I + a
                        thr = thr_ref[0, h, tg, il:il + 1, :]
                        e1 = e1_ref[0, h, tg, il:il + 1, :]
                        term = jnp.where(s2 >= thr, e2, 0.0) * e1
                        gates[a] = term if gates[a] is None else gates[a] + term
                for a in range(GATE_I):
                    il = ip * GATE_I + a
                    act = gates[a] * hid_ref[il, tg, js, :]
                    act_ref[tg, il * nk + jc * GATE_ROWS:il * nk + (jc + 1) * GATE_ROWS, :] = act.astype(BF16)
        return carry

    lax.fori_loop(0, ntg, token_group, 0)
    act = jnp.concatenate([act_ref[tg] for tg in range(ntg)], axis=1)
    acc_ref[...] += _dot(vt_ref[...], act)

    @pl.when(e == pl.num_programs(1) - 1)
    def _():
        o_ref[...] = _layer_norm(alpha * x_ref[...] + acc_ref[...].T, g_ref[...], b_ref[...])


def _peer_dense(x2d, xb, u, vt, layer, thr, e1, s2, e2, g, b, alpha, heads, ts, eb):
    t, d = x2d.shape
    n = u.shape[1]
    nk = s2.shape[3]
    ntg = ts // LANES
    assert (eb // nk) % GATE_I == 0 and nk % GATE_ROWS == 0
    jspec = pl.BlockSpec((1, heads, ntg, nk, LANES), lambda i, e: (i, 0, 0, 0, 0))
    ispec = pl.BlockSpec((1, heads, ntg, eb // nk, LANES), lambda i, e: (i, 0, 0, e, 0))
    return pl.pallas_call(
        functools.partial(_peer_dense_kernel, heads=heads, nk=nk, eb=eb, alpha=alpha),
        grid=(t // ts, n // eb),
        in_specs=[pl.BlockSpec((ts, d), lambda i, e: (i, 0)),
                  pl.BlockSpec((ts, d), lambda i, e: (i, 0)),
                  pl.BlockSpec((None, eb, d), lambda i, e: (layer, e, 0)),
                  pl.BlockSpec((None, d, eb), lambda i, e: (layer, 0, e)),
                  ispec, ispec, jspec, jspec,
                  pl.BlockSpec((1, d), lambda i, e: (0, 0)),
                  pl.BlockSpec((1, d), lambda i, e: (0, 0))],
        out_specs=pl.BlockSpec((ts, d), lambda i, e: (i, 0)),
        out_shape=jax.ShapeDtypeStruct((t, d), F32),
        scratch_shapes=[pltpu.VMEM((d, ts), F32), pltpu.VMEM((ntg, eb, LANES), BF16),
                        pltpu.VMEM((eb // nk, ntg, nk, LANES), F32)],
        compiler_params=_cparams("parallel", "arbitrary"),
        name="peer_dense",
    )(x2d, xb, u, vt, thr, e1, s2, e2, g, b)


def _peer(h2d, w_q, sub_keys, u_all, vt_all, layer, g, b, alpha):
    t, d = h2d.shape
    heads, _, nk, dh = sub_keys.shape
    assert nk == LANES and dh == LANES
    ts = min(512, t)
    eb = min(2048, u_all.shape[1])
    keys = sub_keys.reshape(heads * 2, nk, dh).astype(BF16)
    xb, thr, e1, s2, e2 = _peer_route(h2d, w_q.astype(BF16), keys, heads, ts)
    return _peer_dense(h2d, xb, u_all, vt_all, layer, thr, e1, s2, e2,
                       g.reshape(1, d), b.reshape(1, d), alpha, heads, ts, eb)


def _gdn_layer(h2d, batch, seq, w_in, conv_w, a_log, dt_bias, norm_g, w_out, g, b, alpha):
    t, d = h2d.shape
    heads = a_log.shape[0]
    dv = norm_g.shape[0]
    nv = heads * dv
    nqk = (conv_w.shape[1] - nv) // 2
    assert nqk // heads == LANES and dv == LANES and 2 * heads <= LANES
    chunk = min(GDN_KERNEL_CHUNK, seq)
    ts = min(512, seq)
    tc = min(512, seq)
    cw = 2 * nqk + nv
    wqkv = w_in[:, :cw].astype(BF16)
    wz = w_in[:, cw:cw + nv].astype(BF16)
    wbg = jnp.pad(w_in[:, cw + nv:], ((0, 0), (0, LANES - 2 * heads))).astype(BF16)
    alog = jnp.pad(a_log, (heads, LANES - 2 * heads)).reshape(1, LANES)
    dtb = jnp.pad(dt_bias, (heads, LANES - 2 * heads)).reshape(1, LANES)
    q, k, v, z, bg = _gdn_proj(h2d, wqkv, wz, wbg, conv_w, alog, dtb, batch, seq, heads, nqk, nv, ts, chunk)
    o = _gdn_chunk(q, k, v, z, bg, norm_g.reshape(1, dv), batch, seq, heads, tc, chunk)
    return _proj_ln(o, w_out.astype(BF16), h2d, g.reshape(1, d), b.reshape(1, d), alpha, min(512, t))


def _head_pad(w, heads, width):
    r = w.shape[0]
    return jnp.pad(w.reshape(r, heads, width), ((0, 0), (0, 0), (0, LANES - width))).reshape(r, heads * LANES)


def _rope_swap(w, half):
    return jnp.concatenate([-w[..., half:], w[..., :half]], -1)


def _mla_layer(h2d, positions, batch, seq, shared_kv, kv_w_a, kv_norm_g, kv_w_uk, kv_w_uv, w_qa, q_norm_g, w_qb,
               w_out, g, b, alpha):
    t, d = h2d.shape
    rank = kv_norm_g.shape[0]
    rope = kv_w_a.shape[1] - rank
    qrank = w_qa.shape[1]
    heads = (w_qb.shape[1] - kv_w_uk.shape[1]) // rope
    nope = kv_w_uk.shape[1] // heads
    vdim = kv_w_uv.shape[1] // heads
    half = rope // 2
    assert nope + rope <= LANES and LANES % vdim == 0 and vdim < LANES and rank % LANES == 0 and LANES % rope == 0

    inv_freq = ROPE_THETA ** (-jnp.arange(half, dtype=F32) / half)
    invf = jnp.tile(inv_freq, LANES // half)
    posf = positions.astype(F32).reshape(t, 1)

    kr = kv_w_a[:, rank:]
    place = lambda w: jnp.pad(w, ((0, 0), (nope, LANES - nope - rope)))
    wkva = jnp.concatenate([kv_w_a[:, :rank], place(kr), place(_rope_swap(kr, half))], -1).astype(BF16)
    wuk = _head_pad(kv_w_uk, heads, nope).astype(BF16)
    wq3 = w_qb.reshape(qrank, heads, nope + rope)
    wqb = _head_pad(w_qb, heads, nope + rope).astype(BF16)
    wq_sw = jnp.concatenate([jnp.zeros((qrank, heads, nope), F32), _rope_swap(wq3[..., nope:], half)], -1)
    wqbs = _head_pad(wq_sw.reshape(qrank, heads * (nope + rope)), heads, nope + rope).astype(BF16)

    hpp = LANES // vdim
    wuv3 = kv_w_uv.reshape(rank, heads, vdim)
    wuv = jnp.concatenate([jnp.pad(wuv3[:, h], ((0, 0), ((h % hpp) * vdim, LANES - (h % hpp + 1) * vdim)))
                           for h in range(heads)], -1).astype(BF16)
    ones_at = jnp.array([h * LANES + ((h % hpp + 1) * vdim) % LANES for h in range(heads)])
    vones = jnp.zeros((1, heads * LANES), F32).at[0, ones_at].set(1.0)

    ts = min(512, t)
    q, k, v = _mla_prep(h2d, posf, invf.reshape(1, LANES), wkva, kv_norm_g.reshape(1, rank), wuk, wuv, vones,
                        w_qa.astype(BF16), q_norm_g.reshape(1, qrank), wqb, wqbs, heads, rank, nope, rope, ts)
    if shared_kv is None:
        shared_kv = (k, v)
    tq = min(512, seq)
    o = _flash(q, shared_kv[0], shared_kv[1], batch, seq, heads, vdim, tq, tq)
    h_new = _proj_ln(o, w_out.astype(BF16), h2d, g.reshape(1, d), b.reshape(1, d), alpha, min(512, t))
    return h_new, shared_kv


def kernel(x, positions, a_w_in, a_conv_w, a_a_log, a_dt_bias, a_norm_g, a_w_out, kv_w_a, kv_norm_g, kv_w_uk,
           kv_w_uv, b_w_qa, b_q_norm_g, b_w_qb, b_w_out, peer_w_q, peer_sub_keys, peer_u, peer_v, ln_g, ln_b):
    batch, seq, d = x.shape
    depth = ln_g.shape[0]
    n_a = a_w_in.shape[0]
    alpha = (2 * depth) ** 0.25
    h = x.reshape(batch * seq, d)
    u_all = peer_u.astype(BF16)
    vt_all = jnp.swapaxes((0.5 * peer_v).astype(BF16), 1, 2)
    shared_kv = None
    for layer in range(depth):
        if layer < n_a:
            i = layer
            h = _gdn_layer(h, batch, seq, a_w_in[i], a_conv_w[i], a_a_log[i], a_dt_bias[i], a_norm_g[i],
                           a_w_out[i], ln_g[layer, 0], ln_b[layer, 0], alpha)
        else:
            j = layer - n_a
            h, shared_kv = _mla_layer(h, positions, batch, seq, shared_kv, kv_w_a, kv_norm_g, kv_w_uk, kv_w_uv,
                                      b_w_qa[j], b_q_norm_g[j], b_w_qb[j], b_w_out[j],
                                      ln_g[layer, 0], ln_b[layer, 0], alpha)
        h = _peer(h, peer_w_q[layer], peer_sub_keys[layer], u_all, vt_all, layer,
                  ln_g[layer, 1], ln_b[layer, 1], alpha)
    return h.reshape(batch, seq, d)
```

```python
import functools
import math

import jax
import jax.numpy as jnp
from jax import lax
from jax.experimental import pallas as pl
from jax.experimental.pallas import tpu as pltpu

F32 = jnp.float32
BF16 = jnp.bfloat16
HIGHEST = lax.Precision.HIGHEST

LN_EPS = 1e-5
RMS_EPS = 1e-6
ROPE_THETA = 10000.0
GDN_CONV = 4
PEER_TOPK = 16
LANES = 128
SUBLANES = 8
GDN_KERNEL_CHUNK = 128
GDN_CHUNKS_PER_ITER = 2
VMEM_LIMIT_BYTES = 56 * 1024 * 1024


def _cparams(*sem):
    return pltpu.CompilerParams(dimension_semantics=sem, vmem_limit_bytes=VMEM_LIMIT_BYTES)


def _dot(a, b):
    return jnp.dot(a, b, preferred_element_type=F32)


def _dot_nt(a, b):
    return lax.dot_general(a, b, (((1,), (1,)), ((), ())), preferred_element_type=F32)


def _dot_tn(a, b):
    return lax.dot_general(a, b, (((0,), (0,)), ((), ())), preferred_element_type=F32)


def _dot_hi(a, b):
    return jnp.dot(a, b, precision=HIGHEST, preferred_element_type=F32)


def _sigmoid(x):
    return 1.0 / (1.0 + jnp.exp(-x))


def _softplus(x):
    return jnp.maximum(x, 0.0) + jnp.log1p(jnp.exp(-jnp.abs(x)))


def _layer_norm(y, g, b):
    mu = jnp.mean(y, -1, keepdims=True)
    d = y - mu
    var = jnp.mean(d * d, -1, keepdims=True)
    return d * lax.rsqrt(var + LN_EPS) * g + b


def _rms(x, g):
    return x * lax.rsqrt(jnp.mean(x * x, -1, keepdims=True) + RMS_EPS) * g


def _proj_ln_kernel(a_ref, w_ref, res_ref, g_ref, b_ref, o_ref, *, alpha):
    y = alpha * res_ref[...] + _dot(a_ref[...], w_ref[...])
    o_ref[...] = _layer_norm(y, g_ref[...], b_ref[...])


def _proj_ln(a, w, res, g, b, alpha, tm):
    t, k = a.shape
    d = w.shape[1]
    return pl.pallas_call(
        functools.partial(_proj_ln_kernel, alpha=alpha),
        grid=(t // tm,),
        in_specs=[pl.BlockSpec((tm, k), lambda i: (i, 0)),
                  pl.BlockSpec((k, d), lambda i: (0, 0)),
                  pl.BlockSpec((tm, d), lambda i: (i, 0)),
                  pl.BlockSpec((1, d), lambda i: (0, 0)),
                  pl.BlockSpec((1, d), lambda i: (0, 0))],
        out_specs=pl.BlockSpec((tm, d), lambda i: (i, 0)),
        out_shape=jax.ShapeDtypeStruct((t, d), F32),
        compiler_params=_cparams("parallel"),
        name="proj_ln",
    )(a, w, res, g, b)


def _gdn_proj_kernel(x_ref, wqkv_ref, wz_ref, wbg_ref, convw_ref, alog_ref, dtb_ref, tri_ref,
                     q_ref, k_ref, v_ref, z_ref, bg_ref, buf_ref, *, ts, nqk, nv, heads, chunk):
    s = pl.program_id(1)
    halo = SUBLANES
    xb = x_ref[...].astype(BF16)

    @pl.when(s == 0)
    def _():
        buf_ref[0:halo, :] = jnp.zeros((halo, buf_ref.shape[1]), F32)

    buf_ref[halo:halo + ts, :] = _dot(xb, wqkv_ref[...])
    z_ref[...] = _dot(xb, wz_ref[...]).astype(BF16)

    dk = nqk // heads
    for grp in range((2 * nqk + nv) // LANES):
        c0 = grp * LANES
        acc = None
        for j in range(GDN_CONV):
            r0 = halo - (GDN_CONV - 1) + j
            term = buf_ref[r0:r0 + ts, c0:c0 + LANES] * convw_ref[j:j + 1, c0:c0 + LANES]
            acc = term if acc is None else acc + term
        y = acc * _sigmoid(acc)
        if c0 < 2 * nqk:
            y = y * lax.rsqrt(jnp.sum(y * y, -1, keepdims=True) + RMS_EPS)
        if c0 < nqk:
            q_ref[:, c0:c0 + LANES] = (y * dk ** -0.5).astype(BF16)
        elif c0 < 2 * nqk:
            k_ref[:, c0 - nqk:c0 - nqk + LANES] = y.astype(BF16)
        else:
            v_ref[:, c0 - 2 * nqk:c0 - 2 * nqk + LANES] = y.astype(BF16)

    buf_ref[0:halo, :] = buf_ref[ts:ts + halo, :]

    bgp = _dot(xb, wbg_ref[...])
    lane = lax.broadcasted_iota(jnp.int32, (chunk, LANES), 1)
    for c in range(ts // chunk):
        blk = bgp[c * chunk:(c + 1) * chunk, :]
        beta = _sigmoid(blk)
        g = -jnp.exp(alog_ref[...]) * _softplus(blk + dtb_ref[...])
        gc = _dot_hi(tri_ref[...], g)
        bg_ref[c * chunk:(c + 1) * chunk, :] = jnp.where(lane < heads, beta, gc)


def _gdn_proj(x2d, wqkv, wz, wbg, convw, alog, dtb, batch, seq, heads, nqk, nv, ts, chunk):
    t, d = x2d.shape
    ns = seq // ts
    cw = 2 * nqk + nv
    tri = jnp.tril(jnp.ones((chunk, chunk), F32))
    row = lambda b, s: (b * ns + s, 0)
    fixed = lambda b, s: (0, 0)
    return pl.pallas_call(
        functools.partial(_gdn_proj_kernel, ts=ts, nqk=nqk, nv=nv, heads=heads, chunk=chunk),
        grid=(batch, ns),
        in_specs=[pl.BlockSpec((ts, d), row),
                  pl.BlockSpec((d, cw), fixed),
                  pl.BlockSpec((d, nv), fixed),
                  pl.BlockSpec((d, LANES), fixed),
                  pl.BlockSpec((GDN_CONV, cw), fixed),
                  pl.BlockSpec((1, LANES), fixed),
                  pl.BlockSpec((1, LANES), fixed),
                  pl.BlockSpec((chunk, chunk), fixed)],
        out_specs=[pl.BlockSpec((ts, nqk), row),
                   pl.BlockSpec((ts, nqk), row),
                   pl.BlockSpec((ts, nv), row),
                   pl.BlockSpec((ts, nv), row),
                   pl.BlockSpec((ts, LANES), row)],
        out_shape=[jax.ShapeDtypeStruct((t, nqk), BF16),
                   jax.ShapeDtypeStruct((t, nqk), BF16),
                   jax.ShapeDtypeStruct((t, nv), BF16),
                   jax.ShapeDtypeStruct((t, nv), BF16),
                   jax.ShapeDtypeStruct((t, LANES), F32)],
        scratch_shapes=[pltpu.VMEM((ts + 2 * SUBLANES, cw), F32)],
        compiler_params=_cparams("parallel", "arbitrary"),
        name="gdn_proj",
    )(x2d, wqkv, wz, wbg, convw, alog, dtb, tri)


def _gdn_chunk_kernel(q_ref, k_ref, v_ref, z_ref, bg_ref, ng_ref, o_ref, state_ref, *, tc, heads, chunk):
    @pl.when(pl.program_id(1) == 0)
    def _():
        state_ref[...] = jnp.zeros(state_ref.shape, F32)

    rows = lax.broadcasted_iota(jnp.int32, (chunk, chunk), 0)
    cols = lax.broadcasted_iota(jnp.int32, (chunk, chunk), 1)
    tril = rows >= cols
    strict = rows > cols
    eye = jnp.where(rows == cols, 1.0, 0.0).astype(F32)
    hr = range(heads)

    per = math.gcd(tc // chunk, GDN_CHUNKS_PER_ITER)

    def body(it, carry):
        r0s = [pl.multiple_of((it * per + cc) * chunk, chunk) for cc in range(per)]
        units = [(cc, h) for cc in range(per) for h in hr]
        ur = range(len(units))
        bgs = [bg_ref[pl.ds(r0, chunk), :] for r0 in r0s]
        hs = [slice(h * LANES, (h + 1) * LANES) for _, h in units]
        rows = [pl.ds(r0s[cc], chunk) for cc, _ in units]
        beta = [bgs[cc][:, h:h + 1] for cc, h in units]
        gcol = [bgs[cc][:, heads + h:heads + h + 1] for cc, h in units]
        glast = [g[chunk - 1:chunk, :] for g in gcol]
        eg = [jnp.exp(g) for g in gcol]
        decay = []
        for u in ur:
            gmat = jnp.broadcast_to(gcol[u], (chunk, chunk))
            decay.append(jnp.where(tril, jnp.exp(jnp.where(tril, gmat - gmat.T, 0.0)), 0.0))
        qb = [q_ref[rows[u], hs[u]] for u in ur]
        kb = [k_ref[rows[u], hs[u]] for u in ur]
        kf = [k.astype(F32) for k in kb]
        kbeta = [kf[u] * beta[u] for u in ur]
        vbeta = [v_ref[rows[u], hs[u]].astype(F32) * beta[u] for u in ur]

        kk = [_dot_nt(jnp.concatenate([kbeta[u].astype(BF16), qb[u]], axis=0), kb[u]) for u in ur]
        low = [jnp.where(strict, kk[u][:chunk] * decay[u], 0.0) for u in ur]
        attn = [(kk[u][chunk:] * decay[u]).astype(BF16) for u in ur]

        inv = [eye - l for l in low]
        power = low
        span = 1
        while 2 * span < chunk:
            pb = [p.astype(BF16) for p in power]
            power = [_dot(p, p) for p in pb]
            inv = [inv[u] + _dot(inv[u].astype(BF16), power[u].astype(BF16)) for u in ur]
            span *= 2

        rhs = [jnp.concatenate([vbeta[u], kbeta[u] * eg[u]], axis=1).astype(BF16) for u in ur]
        sol = [_dot(inv[u].astype(BF16), rhs[u]) for u in ur]
        dv = vbeta[0].shape[1]

        st = [state_ref[h] for h in hr]
        for cc in range(per):
            us = [cc * heads + h for h in hr]
            lhs = [jnp.concatenate([sol[u][:, dv:], qb[u].astype(F32) * eg[u]], axis=0).astype(BF16) for u in us]
            ws = [_dot(lhs[h], st[h].astype(BF16)) for h in hr]
            vnb = [(sol[us[h]][:, :dv] - ws[h][:chunk]).astype(BF16) for h in hr]
            o = [ws[h][chunk:] + _dot(attn[us[h]], vnb[h]) for h in hr]
            new_st = []
            for h in hr:
                u = us[h]
                kdec = (kf[u] * jnp.exp(glast[u] - gcol[u])).astype(BF16)
                new_st.append(st[h] * jnp.exp(glast[u]) + _dot_tn(kdec, vnb[h]))
            st = new_st
            for h in hr:
                u = us[h]
                zf = z_ref[rows[u], hs[u]].astype(F32)
                on = _rms(o[h], ng_ref[...]) * (zf * _sigmoid(zf))
                o_ref[rows[u], hs[u]] = on.astype(BF16)
        for h in hr:
            state_ref[h] = st[h]
        return carry

    lax.fori_loop(0, tc // chunk // per, body, 0)


def _gdn_chunk(q, k, v, z, bg, norm_g, batch, seq, heads, tc, chunk):
    t, nqk = q.shape
    nv = v.shape[1]
    ns = seq // tc
    row = lambda b, s: (b * ns + s, 0)
    return pl.pallas_call(
        functools.partial(_gdn_chunk_kernel, tc=tc, heads=heads, chunk=chunk),
        grid=(batch, ns),
        in_specs=[pl.BlockSpec((tc, nqk), row),
                  pl.BlockSpec((tc, nqk), row),
                  pl.BlockSpec((tc, nv), row),
                  pl.BlockSpec((tc, nv), row),
                  pl.BlockSpec((tc, LANES), row),
                  pl.BlockSpec((1, LANES), lambda b, s: (0, 0))],
        out_specs=pl.BlockSpec((tc, nv), row),
        out_shape=jax.ShapeDtypeStruct((t, nv), BF16),
        scratch_shapes=[pltpu.VMEM((heads, nqk // heads, nv // heads), F32)],
        compiler_params=_cparams("parallel", "arbitrary"),
        name="gdn_chunk",
    )(q, k, v, z, bg, norm_g)


def _mla_prep_kernel(x_ref, pos_ref, invf_ref, wkva_ref, gkv_ref, wuk_ref, wuv_ref, vones_ref, wqa_ref, gq_ref,
                     wqb_ref, wqbs_ref, q_ref, k_ref, v_ref, *, heads, rank, nope, rope, scale):
    xb = x_ref[...].astype(BF16)
    ts = xb.shape[0]
    groups = LANES // rope
    rb = ts // groups
    lane = lax.broadcasted_iota(jnp.int32, (rb, LANES), 1)
    pos_c = jnp.zeros((rb, LANES), F32)
    for g in range(groups):
        in_group = (lane >= g * rope) & (lane < (g + 1) * rope)
        pos_c = jnp.where(in_group, pos_ref[g * rb:(g + 1) * rb, :], pos_c)
    ang = pos_c * invf_ref[...]
    cos_c, sin_c = jnp.cos(ang), jnp.sin(ang)
    is_rope = (lane >= nope) & (lane < nope + rope)
    cos_blocks, sin_blocks = [], []
    for g in range(groups):
        shift = (nope - g * rope) % LANES
        cos_g = pltpu.roll(cos_c, shift, 1) if shift else cos_c
        sin_g = pltpu.roll(sin_c, shift, 1) if shift else sin_c
        cos_blocks.append(jnp.where(lane < nope, 1.0, jnp.where(is_rope, cos_g, 0.0)))
        sin_blocks.append(jnp.where(is_rope, sin_g, 0.0))
    cosm = jnp.concatenate(cos_blocks, axis=0)
    sinm = jnp.concatenate(sin_blocks, axis=0)

    kv = _dot(xb, wkva_ref[...])
    ckv = _rms(kv[:, :rank], gkv_ref[...]).astype(BF16)
    krope = kv[:, rank:rank + LANES] * cosm + kv[:, rank + LANES:rank + 2 * LANES] * sinm
    v_ref[...] = (_dot(ckv, wuv_ref[...]) + vones_ref[...]).astype(BF16)
    knope = _dot(ckv, wuk_ref[...])

    qn = _rms(_dot(xb, wqa_ref[...]), gq_ref[...]).astype(BF16)
    qa = _dot(qn, wqb_ref[...])
    qs = _dot(qn, wqbs_ref[...])
    for h in range(heads):
        hs = slice(h * LANES, (h + 1) * LANES)
        k_ref[:, hs] = (knope[:, hs] + krope).astype(BF16)
        q_ref[:, hs] = ((qa[:, hs] * cosm + qs[:, hs] * sinm) * scale).astype(BF16)


def _mla_prep(x2d, posf, invf, wkva, gkv, wuk, wuv, vones, wqa, gq, wqb, wqbs, heads, rank, nope, rope, ts):
    t, d = x2d.shape
    scale = (nope + rope) ** -0.5
    row = lambda i: (i, 0)
    fixed = lambda i: (0, 0)
    full = lambda a: pl.BlockSpec(a.shape, fixed)
    ospec = pl.BlockSpec((ts, heads * LANES), row)
    return pl.pallas_call(
        functools.partial(_mla_prep_kernel, heads=heads, rank=rank, nope=nope, rope=rope, scale=scale),
        grid=(t // ts,),
        in_specs=[pl.BlockSpec((ts, d), row), pl.BlockSpec((ts, 1), row), full(invf), full(wkva), full(gkv),
                  full(wuk), full(wuv), full(vones), full(wqa), full(gq), full(wqb), full(wqbs)],
        out_specs=[ospec, ospec, ospec],
        out_shape=[jax.ShapeDtypeStruct((t, heads * LANES), BF16)] * 3,
        compiler_params=_cparams("parallel"),
        name="mla_prep",
    )(x2d, posf, invf, wkva, gkv, wuk, wuv, vones, wqa, gq, wqb, wqbs)


FLASH_HEADS = 4


def _flash_kernel(q_ref, k_ref, v_ref, o_ref, *, tq, tk, vdim, nh):
    qi = pl.program_id(2)
    hpp = LANES // vdim
    lane = lax.broadcasted_iota(jnp.int32, (tq, LANES), 1)
    qs = [q_ref[:, hh * LANES:(hh + 1) * LANES] for hh in range(nh)]

    def chunk(c, carry, masked):
        k0 = pl.multiple_of(c * tk, tk)
        ss = [_dot_nt(qs[hh], k_ref[pl.ds(k0, tk), hh * LANES:(hh + 1) * LANES]) for hh in range(nh)]
        if masked:
            keep = lax.broadcasted_iota(jnp.int32, (tq, tk), 1) <= lax.broadcasted_iota(jnp.int32, (tq, tk), 0)
            ss = [jnp.where(keep, s, -jnp.inf) for s in ss]
        out = []
        for hh in range(nh):
            m, acc = carry[hh]
            m_new = jnp.maximum(m, jnp.max(ss[hh], -1, keepdims=True))
            p = jnp.exp((ss[hh] - m_new).astype(BF16))
            acc = jnp.exp(m - m_new) * acc + _dot(p, v_ref[pl.ds(k0, tk), hh * LANES:(hh + 1) * LANES])
            out.append((m_new, acc))
        return tuple(out)

    init = tuple((jnp.full((tq, 1), -jnp.inf, F32), jnp.zeros((tq, LANES), F32)) for _ in range(nh))
    carry = lax.fori_loop(0, qi, functools.partial(chunk, masked=False), init)
    carry = chunk(qi, carry, masked=True)
    for grp in range(nh // hpp):
        out = jnp.zeros((tq, LANES), F32)
        for slot in range(hpp):
            acc = carry[grp * hpp + slot][1]
            ones_lane = ((slot + 1) * vdim) % LANES
            mine = (lane >= slot * vdim) & (lane < (slot + 1) * vdim)
            out = jnp.where(mine, acc / acc[:, ones_lane:ones_lane + 1], out)
        o_ref[:, grp * LANES:(grp + 1) * LANES] = out.astype(BF16)


def _flash(q, k, v, batch, seq, heads, vdim, tq, tk):
    t = q.shape[0]
    hpp = LANES // vdim
    nh = math.gcd(heads, FLASH_HEADS)
    nq = seq // tq
    assert tq == tk and nh % hpp == 0
    return pl.pallas_call(
        functools.partial(_flash_kernel, tq=tq, tk=tk, vdim=vdim, nh=nh),
        grid=(batch, heads // nh, nq),
        in_specs=[pl.BlockSpec((tq, nh * LANES), lambda b, h, i: (b * nq + i, h)),
                  pl.BlockSpec((seq, nh * LANES), lambda b, h, i: (b, h)),
                  pl.BlockSpec((seq, nh * LANES), lambda b, h, i: (b, h))],
        out_specs=pl.BlockSpec((tq, nh // hpp * LANES), lambda b, h, i: (b * nq + i, h)),
        out_shape=jax.ShapeDtypeStruct((t, heads * vdim), BF16),
        compiler_params=_cparams("parallel", "parallel", "arbitrary"),
        name="mla_flash",
    )(q, k, v)


def _top_values(sc, n):
    vals = []
    cur = sc
    for _ in range(n):
        m = jnp.max(cur, axis=0, keepdims=True)
        vals.append(m)
        cur = jnp.where(cur == m, -jnp.inf, cur)
    return vals


def _sorting_network(n):
    pairs = []
    p = 1
    while p < n:
        k = p
        while k >= 1:
            for j in range(k % p, n - k, 2 * k):
                for i in range(min(k, n - j - k)):
                    if (i + j) // (2 * p) == (i + j + k) // (2 * p):
                        pairs.append((i + j, i + j + k))
            k //= 2
        p *= 2
    return pairs


def _top_values_sorted(sc, n):
    nslab = sc.shape[0] // SUBLANES
    cols = [sc[v * SUBLANES:(v + 1) * SUBLANES, :] for v in range(nslab)]
    for i, j in _sorting_network(nslab):
        cols[i], cols[j] = jnp.maximum(cols[i], cols[j]), jnp.minimum(cols[i], cols[j])
    vals = []
    for r in range(n):
        m = jnp.max(cols[0], axis=0, keepdims=True)
        vals.append(m)
        hit = cols[0] == m
        for k in range(min(nslab, n - 1 - r)):
            nxt = cols[k + 1] if k + 1 < nslab else -jnp.inf
            cols[k] = jnp.where(hit, nxt, cols[k])
    return vals


def _peer_route_kernel(x_ref, wq_ref, keys_ref, xb_ref, thr_ref, e1_ref, s2_ref, e2_ref, *, heads, topk):
    xb = x_ref[...].astype(BF16)
    xb_ref[...] = xb
    q = _dot(xb, wq_ref[...])
    nk = keys_ref.shape[1]
    for h in range(heads):
        s1 = _dot_nt(keys_ref[2 * h], q[:, 2 * h * nk:(2 * h + 1) * nk].astype(BF16))
        s2 = _dot_nt(keys_ref[2 * h + 1], q[:, (2 * h + 1) * nk:(2 * h + 2) * nk].astype(BF16))
        a = _top_values_sorted(s1, topk + 1)
        b = _top_values_sorted(s2, topk + 1)
        cands = [a[i] + b[j] for i in range(topk + 1) for j in range(topk + 1) if (i + 1) * (j + 1) <= topk + 1]
        best = _top_values(jnp.concatenate(cands, axis=0), topk + 1)
        zsum = None
        for r in range(topk):
            e = jnp.exp(best[r] - best[0])
            zsum = e if zsum is None else zsum + e
        thr = 0.5 * (best[topk - 1] + best[topk]) - s1
        outs = ((thr_ref, thr), (e1_ref, jnp.exp(s1 - a[0]) / zsum), (s2_ref, s2), (e2_ref, jnp.exp(s2 - b[0])))
        for ref, val in outs:
            for tg in range(val.shape[1] // LANES):
                ref[0, h, tg] = val[:, tg * LANES:(tg + 1) * LANES]


def _peer_route(x2d, wq, keys, heads, ts):
    t, d = x2d.shape
    nk = keys.shape[1]
    nt = t // ts
    shape = (nt, heads, ts // LANES, nk, LANES)
    ospec = pl.BlockSpec((1,) + shape[1:], lambda i: (i, 0, 0, 0, 0))
    return pl.pallas_call(
        functools.partial(_peer_route_kernel, heads=heads, topk=PEER_TOPK),
        grid=(nt,),
        in_specs=[pl.BlockSpec((ts, d), lambda i: (i, 0)),
                  pl.BlockSpec(wq.shape, lambda i: (0, 0)),
                  pl.BlockSpec(keys.shape, lambda i: (0, 0, 0))],
        out_specs=[pl.BlockSpec((ts, d), lambda i: (i, 0)), ospec, ospec, ospec, ospec],
        out_shape=[jax.ShapeDtypeStruct((t, d), BF16)] + [jax.ShapeDtypeStruct(shape, F32)] * 4,
        compiler_params=_cparams("parallel"),
        name="peer_route",
    )(x2d, wq, keys)


def _gelu_x2(x):
    return x * (1.0 + lax.erf(x * (2.0 ** -0.5)))


GATE_I = 2
GATE_ROWS = 64


def _peer_dense_kernel(x_ref, xb_ref, u_ref, vt_ref, thr_ref, e1_ref, s2_ref, e2_ref, g_ref, b_ref, o_ref,
                       acc_ref, act_ref, hid_ref, *, heads, nk, eb, alpha):
    e = pl.program_id(1)
    ts = xb_ref.shape[0]
    ni = eb // nk
    ntg = ts // LANES

    @pl.when(e == 0)
    def _():
        acc_ref[...] = jnp.zeros(acc_ref.shape, F32)

    hid = _gelu_x2(_dot_nt(u_ref[...], xb_ref[...]))
    for il in range(ni):
        for tg in range(ntg):
            hid_ref[il, tg] = hid[il * nk:(il + 1) * nk, tg * LANES:(tg + 1) * LANES]

    def token_group(tg, carry):
        for ip in range(ni // GATE_I):
            for jc in range(nk // GATE_ROWS):
                js = slice(jc * GATE_ROWS, (jc + 1) * GATE_ROWS)
                gates = [None] * GATE_I
                for h in range(heads):
                    s2 = s2_ref[0, h, tg, js, :]
                    e2 = e2_ref[0, h, tg, js, :]
                    for a in range(GATE_I):
                        il = ip * GATE_I + a
                        thr = thr_ref[0, h, tg, il:il + 1, :]
                        e1 = e1_ref[0, h, tg, il:il + 1, :]
                        term = jnp.where(s2 >= thr, e2, 0.0) * e1
                        gates[a] = term if gates[a] is None else gates[a] + term
                for a in range(GATE_I):
                    il = ip * GATE_I + a
                    act = gates[a] * hid_ref[il, tg, js, :]
                    act_ref[tg, il * nk + jc * GATE_ROWS:il * nk + (jc + 1) * GATE_ROWS, :] = act.astype(BF16)
        return carry

    lax.fori_loop(0, ntg, token_group, 0)
    act = jnp.concatenate([act_ref[tg] for tg in range(ntg)], axis=1)
    acc_ref[...] += _dot(vt_ref[...], act)

    @pl.when(e == pl.num_programs(1) - 1)
    def _():
        o_ref[...] = _layer_norm(alpha * x_ref[...] + acc_ref[...].T, g_ref[...], b_ref[...])


def _peer_dense(x2d, xb, u, vt, layer, thr, e1, s2, e2, g, b, alpha, heads, ts, eb):
    t, d = x2d.shape
    n = u.shape[1]
    nk = s2.shape[3]
    ntg = ts // LANES
    assert (eb // nk) % GATE_I == 0 and nk % GATE_ROWS == 0
    jspec = pl.BlockSpec((1, heads, ntg, nk, LANES), lambda i, e: (i, 0, 0, 0, 0))
    ispec = pl.BlockSpec((1, heads, ntg, eb // nk, LANES), lambda i, e: (i, 0, 0, e, 0))
    return pl.pallas_call(
        functools.partial(_peer_dense_kernel, heads=heads, nk=nk, eb=eb, alpha=alpha),
        grid=(t // ts, n // eb),
        in_specs=[pl.BlockSpec((ts, d), lambda i, e: (i, 0)),
                  pl.BlockSpec((ts, d), lambda i, e: (i, 0)),
                  pl.BlockSpec((None, eb, d), lambda i, e: (layer, e, 0)),
                  pl.BlockSpec((None, d, eb), lambda i, e: (layer, 0, e)),
                  ispec, ispec, jspec, jspec,
                  pl.BlockSpec((1, d), lambda i, e: (0, 0)),
                  pl.BlockSpec((1, d), lambda i, e: (0, 0))],
        out_specs=pl.BlockSpec((ts, d), lambda i, e: (i, 0)),
        out_shape=jax.ShapeDtypeStruct((t, d), F32),
        scratch_shapes=[pltpu.VMEM((d, ts), F32), pltpu.VMEM((ntg, eb, LANES), BF16),
                        pltpu.VMEM((eb // nk, ntg, nk, LANES), F32)],
        compiler_params=_cparams("parallel", "arbitrary"),
        name="peer_dense",
    )(x2d, xb, u, vt, thr, e1, s2, e2, g, b)


def _peer(h2d, w_q, sub_keys, u_all, vt_all, layer, g, b, alpha):
    t, d = h2d.shape
    heads, _, nk, dh = sub_keys.shape
    assert nk == LANES and dh == LANES
    ts = min(512, t)
    eb = min(2048, u_all.shape[1])
    keys = sub_keys.reshape(heads * 2, nk, dh).astype(BF16)
    xb, thr, e1, s2, e2 = _peer_route(h2d, w_q.astype(BF16), keys, heads, ts)
    return _peer_dense(h2d, xb, u_all, vt_all, layer, thr, e1, s2, e2,
                       g.reshape(1, d), b.reshape(1, d), alpha, heads, ts, eb)


def _gdn_layer(h2d, batch, seq, w_in, conv_w, a_log, dt_bias, norm_g, w_out, g, b, alpha):
    t, d = h2d.shape
    heads = a_log.shape[0]
    dv = norm_g.shape[0]
    nv = heads * dv
    nqk = (conv_w.shape[1] - nv) // 2
    assert nqk // heads == LANES and dv == LANES and 2 * heads <= LANES
    chunk = min(GDN_KERNEL_CHUNK, seq)
    ts = min(512, seq)
    tc = min(512, seq)
    cw = 2 * nqk + nv
    wqkv = w_in[:, :cw].astype(BF16)
    wz = w_in[:, cw:cw + nv].astype(BF16)
    wbg = jnp.pad(w_in[:, cw + nv:], ((0, 0), (0, LANES - 2 * heads))).astype(BF16)
    alog = jnp.pad(a_log, (heads, LANES - 2 * heads)).reshape(1, LANES)
    dtb = jnp.pad(dt_bias, (heads, LANES - 2 * heads)).reshape(1, LANES)
    q, k, v, z, bg = _gdn_proj(h2d, wqkv, wz, wbg, conv_w, alog, dtb, batch, seq, heads, nqk, nv, ts, chunk)
    o = _gdn_chunk(q, k, v, z, bg, norm_g.reshape(1, dv), batch, seq, heads, tc, chunk)
    return _proj_ln(o, w_out.astype(BF16), h2d, g.reshape(1, d), b.reshape(1, d), alpha, min(512, t))


def _head_pad(w, heads, width):
    r = w.shape[0]
    return jnp.pad(w.reshape(r, heads, width), ((0, 0), (0, 0), (0, LANES - width))).reshape(r, heads * LANES)


def _rope_swap(w, half):
    return jnp.concatenate([-w[..., half:], w[..., :half]], -1)


def _mla_layer(h2d, positions, batch, seq, shared_kv, kv_w_a, kv_norm_g, kv_w_uk, kv_w_uv, w_qa, q_norm_g, w_qb,
               w_out, g, b, alpha):
    t, d = h2d.shape
    rank = kv_norm_g.shape[0]
    rope = kv_w_a.shape[1] - rank
    qrank = w_qa.shape[1]
    heads = (w_qb.shape[1] - kv_w_uk.shape[1]) // rope
    nope = kv_w_uk.shape[1] // heads
    vdim = kv_w_uv.shape[1] // heads
    half = rope // 2
    assert nope + rope <= LANES and LANES % vdim == 0 and vdim < LANES and rank % LANES == 0 and LANES % rope == 0

    inv_freq = ROPE_THETA ** (-jnp.arange(half, dtype=F32) / half)
    invf = jnp.tile(inv_freq, LANES // half)
    posf = positions.astype(F32).reshape(t, 1)

    kr = kv_w_a[:, rank:]
    place = lambda w: jnp.pad(w, ((0, 0), (nope, LANES - nope - rope)))
    wkva = jnp.concatenate([kv_w_a[:, :rank], place(kr), place(_rope_swap(kr, half))], -1).astype(BF16)
    wuk = _head_pad(kv_w_uk, heads, nope).astype(BF16)
    wq3 = w_qb.reshape(qrank, heads, nope + rope)
    wqb = _head_pad(w_qb, heads, nope + rope).astype(BF16)
    wq_sw = jnp.concatenate([jnp.zeros((qrank, heads, nope), F32), _rope_swap(wq3[..., nope:], half)], -1)
    wqbs = _head_pad(wq_sw.reshape(qrank, heads * (nope + rope)), heads, nope + rope).astype(BF16)

    hpp = LANES // vdim
    wuv3 = kv_w_uv.reshape(rank, heads, vdim)
    wuv = jnp.concatenate([jnp.pad(wuv3[:, h], ((0, 0), ((h % hpp) * vdim, LANES - (h % hpp + 1) * vdim)))
                           for h in range(heads)], -1).astype(BF16)
    ones_at = jnp.array([h * LANES + ((h % hpp + 1) * vdim) % LANES for h in range(heads)])
    vones = jnp.zeros((1, heads * LANES), F32).at[0, ones_at].set(1.0)

    ts = min(512, t)
    q, k, v = _mla_prep(h2d, posf, invf.reshape(1, LANES), wkva, kv_norm_g.reshape(1, rank), wuk, wuv, vones,
                        w_qa.astype(BF16), q_norm_g.reshape(1, qrank), wqb, wqbs, heads, rank, nope, rope, ts)
    if shared_kv is None:
        shared_kv = (k, v)
    tq = min(512, seq)
    o = _flash(q, shared_kv[0], shared_kv[1], batch, seq, heads, vdim, tq, tq)
    h_new = _proj_ln(o, w_out.astype(BF16), h2d, g.reshape(1, d), b.reshape(1, d), alpha, min(512, t))
    return h_new, shared_kv


def kernel(x, positions, a_w_in, a_conv_w, a_a_log, a_dt_bias, a_norm_g, a_w_out, kv_w_a, kv_norm_g, kv_w_uk,
           kv_w_uv, b_w_qa, b_q_norm_g, b_w_qb, b_w_out, peer_w_q, peer_sub_keys, peer_u, peer_v, ln_g, ln_b):
    batch, seq, d = x.shape
    depth = ln_g.shape[0]
    n_a = a_w_in.shape[0]
    alpha = (2 * depth) ** 0.25
    h = x.reshape(batch * seq, d)
    u_all = peer_u.astype(BF16)
    vt_all = jnp.swapaxes((0.5 * peer_v).astype(BF16), 1, 2)
    shared_kv = None
    for layer in range(depth):
        if layer < n_a:
            i = layer
            h = _gdn_layer(h, batch, seq, a_w_in[i], a_conv_w[i], a_a_log[i], a_dt_bias[i], a_norm_g[i],
                           a_w_out[i], ln_g[layer, 0], ln_b[layer, 0], alpha)
        else:
            j = layer - n_a
            h, shared_kv = _mla_layer(h, positions, batch, seq, shared_kv, kv_w_a, kv_norm_g, kv_w_uk, kv_w_uv,
                                      b_w_qa[j], b_q_norm_g[j], b_w_qb[j], b_w_out[j],
                                      ln_g[layer, 0], ln_b[layer, 0], alpha)
        h = _peer(h, peer_w_q[layer], peer_sub_keys[layer], u_all, vt_all, layer,
                  ln_g[layer, 1], ln_b[layer, 1], alpha)
    return h.reshape(batch, seq, d)
```

```python
import functools
import math

import jax
import jax.numpy as jnp
from jax import lax
from jax.experimental import pallas as pl
from jax.experimental.pallas import tpu as pltpu

F32 = jnp.float32
BF16 = jnp.bfloat16
HIGHEST = lax.Precision.HIGHEST

LN_EPS = 1e-5
RMS_EPS = 1e-6
ROPE_THETA = 10000.0
GDN_CONV = 4
PEER_TOPK = 16
LANES = 128
SUBLANES = 8
GDN_KERNEL_CHUNK = 128
GDN_CHUNKS_PER_ITER = 2
VMEM_LIMIT_BYTES = 56 * 1024 * 1024


def _cparams(*sem):
    return pltpu.CompilerParams(dimension_semantics=sem, vmem_limit_bytes=VMEM_LIMIT_BYTES)


def _dot(a, b):
    return jnp.dot(a, b, preferred_element_type=F32)


def _dot_nt(a, b):
    return lax.dot_general(a, b, (((1,), (1,)), ((), ())), preferred_element_type=F32)


def _dot_tn(a, b):
    return lax.dot_general(a, b, (((0,), (0,)), ((), ())), preferred_element_type=F32)


def _dot_hi(a, b):
    return jnp.dot(a, b, precision=HIGHEST, preferred_element_type=F32)


def _sigmoid(x):
    return 1.0 / (1.0 + jnp.exp(-x))


def _softplus(x):
    return jnp.maximum(x, 0.0) + jnp.log1p(jnp.exp(-jnp.abs(x)))


def _layer_norm(y, g, b):
    mu = jnp.mean(y, -1, keepdims=True)
    d = y - mu
    var = jnp.mean(d * d, -1, keepdims=True)
    return d * lax.rsqrt(var + LN_EPS) * g + b


def _rms(x, g):
    return x * lax.rsqrt(jnp.mean(x * x, -1, keepdims=True) + RMS_EPS) * g


def _proj_ln_kernel(a_ref, w_ref, res_ref, g_ref, b_ref, o_ref, *, alpha):
    y = alpha * res_ref[...] + _dot(a_ref[...], w_ref[...])
    o_ref[...] = _layer_norm(y, g_ref[...], b_ref[...])


def _proj_ln(a, w, res, g, b, alpha, tm):
    t, k = a.shape
    d = w.shape[1]
    return pl.pallas_call(
        functools.partial(_proj_ln_kernel, alpha=alpha),
        grid=(t // tm,),
        in_specs=[pl.BlockSpec((tm, k), lambda i: (i, 0)),
                  pl.BlockSpec((k, d), lambda i: (0, 0)),
                  pl.BlockSpec((tm, d), lambda i: (i, 0)),
                  pl.BlockSpec((1, d), lambda i: (0, 0)),
                  pl.BlockSpec((1, d), lambda i: (0, 0))],
        out_specs=pl.BlockSpec((tm, d), lambda i: (i, 0)),
        out_shape=jax.ShapeDtypeStruct((t, d), F32),
        compiler_params=_cparams("parallel"),
        name="proj_ln",
    )(a, w, res, g, b)


def _gdn_proj_kernel(x_ref, wqkv_ref, wz_ref, wbg_ref, convw_ref, alog_ref, dtb_ref, tri_ref,
                     q_ref, k_ref, v_ref, z_ref, bg_ref, buf_ref, *, ts, nqk, nv, heads, chunk):
    s = pl.program_id(1)
    halo = SUBLANES
    xb = x_ref[...].astype(BF16)

    @pl.when(s == 0)
    def _():
        buf_ref[0:halo, :] = jnp.zeros((halo, buf_ref.shape[1]), F32)

    buf_ref[halo:halo + ts, :] = _dot(xb, wqkv_ref[...])
    z_ref[...] = _dot(xb, wz_ref[...]).astype(BF16)

    dk = nqk // heads
    for grp in range((2 * nqk + nv) // LANES):
        c0 = grp * LANES
        acc = None
        for j in range(GDN_CONV):
            r0 = halo - (GDN_CONV - 1) + j
            term = buf_ref[r0:r0 + ts, c0:c0 + LANES] * convw_ref[j:j + 1, c0:c0 + LANES]
            acc = term if acc is None else acc + term
        y = acc * _sigmoid(acc)
        if c0 < 2 * nqk:
            y = y * lax.rsqrt(jnp.sum(y * y, -1, keepdims=True) + RMS_EPS)
        if c0 < nqk:
            q_ref[:, c0:c0 + LANES] = (y * dk ** -0.5).astype(BF16)
        elif c0 < 2 * nqk:
            k_ref[:, c0 - nqk:c0 - nqk + LANES] = y.astype(BF16)
        else:
            v_ref[:, c0 - 2 * nqk:c0 - 2 * nqk + LANES] = y.astype(BF16)

    buf_ref[0:halo, :] = buf_ref[ts:ts + halo, :]

    bgp = _dot(xb, wbg_ref[...])
    lane = lax.broadcasted_iota(jnp.int32, (chunk, LANES), 1)
    for c in range(ts // chunk):
        blk = bgp[c * chunk:(c + 1) * chunk, :]
        beta = _sigmoid(blk)
        g = -jnp.exp(alog_ref[...]) * _softplus(blk + dtb_ref[...])
        gc = _dot_hi(tri_ref[...], g)
        bg_ref[c * chunk:(c + 1) * chunk, :] = jnp.where(lane < heads, beta, gc)


def _gdn_proj(x2d, wqkv, wz, wbg, convw, alog, dtb, batch, seq, heads, nqk, nv, ts, chunk):
    t, d = x2d.shape
    ns = seq // ts
    cw = 2 * nqk + nv
    tri = jnp.tril(jnp.ones((chunk, chunk), F32))
    row = lambda b, s: (b * ns + s, 0)
    fixed = lambda b, s: (0, 0)
    return pl.pallas_call(
        functools.partial(_gdn_proj_kernel, ts=ts, nqk=nqk, nv=nv, heads=heads, chunk=chunk),
        grid=(batch, ns),
        in_specs=[pl.BlockSpec((ts, d), row),
                  pl.BlockSpec((d, cw), fixed),
                  pl.BlockSpec((d, nv), fixed),
                  pl.BlockSpec((d, LANES), fixed),
                  pl.BlockSpec((GDN_CONV, cw), fixed),
                  pl.BlockSpec((1, LANES), fixed),
                  pl.BlockSpec((1, LANES), fixed),
                  pl.BlockSpec((chunk, chunk), fixed)],
        out_specs=[pl.BlockSpec((ts, nqk), row),
                   pl.BlockSpec((ts, nqk), row),
                   pl.BlockSpec((ts, nv), row),
                   pl.BlockSpec((ts, nv), row),
                   pl.BlockSpec((ts, LANES), row)],
        out_shape=[jax.ShapeDtypeStruct((t, nqk), BF16),
                   jax.ShapeDtypeStruct((t, nqk), BF16),
                   jax.ShapeDtypeStruct((t, nv), BF16),
                   jax.ShapeDtypeStruct((t, nv), BF16),
                   jax.ShapeDtypeStruct((t, LANES), F32)],
        scratch_shapes=[pltpu.VMEM((ts + 2 * SUBLANES, cw), F32)],
        compiler_params=_cparams("parallel", "arbitrary"),
        name="gdn_proj",
    )(x2d, wqkv, wz, wbg, convw, alog, dtb, tri)


def _gdn_chunk_kernel(q_ref, k_ref, v_ref, z_ref, bg_ref, ng_ref, o_ref, state_ref, *, tc, heads, chunk):
    @pl.when(pl.program_id(1) == 0)
    def _():
        state_ref[...] = jnp.zeros(state_ref.shape, F32)

    rows = lax.broadcasted_iota(jnp.int32, (chunk, chunk), 0)
    cols = lax.broadcasted_iota(jnp.int32, (chunk, chunk), 1)
    tril = rows >= cols
    strict = rows > cols
    eye = jnp.where(rows == cols, 1.0, 0.0).astype(F32)
    hr = range(heads)

    per = math.gcd(tc // chunk, GDN_CHUNKS_PER_ITER)

    def body(it, carry):
        r0s = [pl.multiple_of((it * per + cc) * chunk, chunk) for cc in range(per)]
        units = [(cc, h) for cc in range(per) for h in hr]
        ur = range(len(units))
        bgs = [bg_ref[pl.ds(r0, chunk), :] for r0 in r0s]
        hs = [slice(h * LANES, (h + 1) * LANES) for _, h in units]
        rows = [pl.ds(r0s[cc], chunk) for cc, _ in units]
        beta = [bgs[cc][:, h:h + 1] for cc, h in units]
        gcol = [bgs[cc][:, heads + h:heads + h + 1] for cc, h in units]
        glast = [g[chunk - 1:chunk, :] for g in gcol]
        eg = [jnp.exp(g) for g in gcol]
        decay = []
        for u in ur:
            gmat = jnp.broadcast_to(gcol[u], (chunk, chunk))
            decay.append(jnp.where(tril, jnp.exp(jnp.where(tril, gmat - gmat.T, 0.0)), 0.0))
        qb = [q_ref[rows[u], hs[u]] for u in ur]
        kb = [k_ref[rows[u], hs[u]] for u in ur]
        kf = [k.astype(F32) for k in kb]
        kbeta = [kf[u] * beta[u] for u in ur]
        vbeta = [v_ref[rows[u], hs[u]].astype(F32) * beta[u] for u in ur]

        kk = [_dot_nt(jnp.concatenate([kbeta[u].astype(BF16), qb[u]], axis=0), kb[u]) for u in ur]
        low = [jnp.where(strict, kk[u][:chunk] * decay[u], 0.0) for u in ur]
        attn = [(kk[u][chunk:] * decay[u]).astype(BF16) for u in ur]

        inv = [eye - l for l in low]
        power = low
        span = 1
        while 2 * span < chunk:
            pb = [p.astype(BF16) for p in power]
            power = [_dot(p, p) for p in pb]
            inv = [inv[u] + _dot(inv[u].astype(BF16), power[u].astype(BF16)) for u in ur]
            span *= 2

        rhs = [jnp.concatenate([vbeta[u], kbeta[u] * eg[u]], axis=1).astype(BF16) for u in ur]
        sol = [_dot(inv[u].astype(BF16), rhs[u]) for u in ur]
        dv = vbeta[0].shape[1]

        st = [state_ref[h] for h in hr]
        for cc in range(per):
            us = [cc * heads + h for h in hr]
            lhs = [jnp.concatenate([sol[u][:, dv:], qb[u].astype(F32) * eg[u]], axis=0).astype(BF16) for u in us]
            ws = [_dot(lhs[h], st[h].astype(BF16)) for h in hr]
            vnb = [(sol[us[h]][:, :dv] - ws[h][:chunk]).astype(BF16) for h in hr]
            o = [ws[h][chunk:] + _dot(attn[us[h]], vnb[h]) for h in hr]
            new_st = []
            for h in hr:
                u = us[h]
                kdec = (kf[u] * jnp.exp(glast[u] - gcol[u])).astype(BF16)
                new_st.append(st[h] * jnp.exp(glast[u]) + _dot_tn(kdec, vnb[h]))
            st = new_st
            for h in hr:
                u = us[h]
                zf = z_ref[rows[u], hs[u]].astype(F32)
                on = _rms(o[h], ng_ref[...]) * (zf * _sigmoid(zf))
                o_ref[rows[u], hs[u]] = on.astype(BF16)
        for h in hr:
            state_ref[h] = st[h]
        return carry

    lax.fori_loop(0, tc // chunk // per, body, 0)


def _gdn_chunk(q, k, v, z, bg, norm_g, batch, seq, heads, tc, chunk):
    t, nqk = q.shape
    nv = v.shape[1]
    ns = seq // tc
    row = lambda b, s: (b * ns + s, 0)
    return pl.pallas_call(
        functools.partial(_gdn_chunk_kernel, tc=tc, heads=heads, chunk=chunk),
        grid=(batch, ns),
        in_specs=[pl.BlockSpec((tc, nqk), row),
                  pl.BlockSpec((tc, nqk), row),
                  pl.BlockSpec((tc, nv), row),
                  pl.BlockSpec((tc, nv), row),
                  pl.BlockSpec((tc, LANES), row),
                  pl.BlockSpec((1, LANES), lambda b, s: (0, 0))],
        out_specs=pl.BlockSpec((tc, nv), row),
        out_shape=jax.ShapeDtypeStruct((t, nv), BF16),
        scratch_shapes=[pltpu.VMEM((heads, nqk // heads, nv // heads), F32)],
        compiler_params=_cparams("parallel", "arbitrary"),
        name="gdn_chunk",
    )(q, k, v, z, bg, norm_g)


def _mla_prep_kernel(x_ref, pos_ref, invf_ref, wkva_ref, gkv_ref, wuk_ref, wuv_ref, vones_ref, wqa_ref, gq_ref,
                     wqb_ref, wqbs_ref, q_ref, k_ref, v_ref, *, heads, rank, nope, rope, scale):
    xb = x_ref[...].astype(BF16)
    ts = xb.shape[0]
    groups = LANES // rope
    rb = ts // groups
    lane = lax.broadcasted_iota(jnp.int32, (rb, LANES), 1)
    pos_c = jnp.zeros((rb, LANES), F32)
    for g in range(groups):
        in_group = (lane >= g * rope) & (lane < (g + 1) * rope)
        pos_c = jnp.where(in_group, pos_ref[g * rb:(g + 1) * rb, :], pos_c)
    ang = pos_c * invf_ref[...]
    cos_c, sin_c = jnp.cos(ang), jnp.sin(ang)
    is_rope = (lane >= nope) & (lane < nope + rope)
    cos_blocks, sin_blocks = [], []
    for g in range(groups):
        shift = (nope - g * rope) % LANES
        cos_g = pltpu.roll(cos_c, shift, 1) if shift else cos_c
        sin_g = pltpu.roll(sin_c, shift, 1) if shift else sin_c
        cos_blocks.append(jnp.where(lane < nope, 1.0, jnp.where(is_rope, cos_g, 0.0)))
        sin_blocks.append(jnp.where(is_rope, sin_g, 0.0))
    cosm = jnp.concatenate(cos_blocks, axis=0)
    sinm = jnp.concatenate(sin_blocks, axis=0)

    kv = _dot(xb, wkva_ref[...])
    ckv = _rms(kv[:, :rank], gkv_ref[...]).astype(BF16)
    krope = kv[:, rank:rank + LANES] * cosm + kv[:, rank + LANES:rank + 2 * LANES] * sinm
    v_ref[...] = (_dot(ckv, wuv_ref[...]) + vones_ref[...]).astype(BF16)
    knope = _dot(ckv, wuk_ref[...])

    qn = _rms(_dot(xb, wqa_ref[...]), gq_ref[...]).astype(BF16)
    qa = _dot(qn, wqb_ref[...])
    qs = _dot(qn, wqbs_ref[...])
    for h in range(heads):
        hs = slice(h * LANES, (h + 1) * LANES)
        k_ref[:, hs] = (knope[:, hs] + krope).astype(BF16)
        q_ref[:, hs] = ((qa[:, hs] * cosm + qs[:, hs] * sinm) * scale).astype(BF16)


def _mla_prep(x2d, posf, invf, wkva, gkv, wuk, wuv, vones, wqa, gq, wqb, wqbs, heads, rank, nope, rope, ts):
    t, d = x2d.shape
    scale = (nope + rope) ** -0.5
    row = lambda i: (i, 0)
    fixed = lambda i: (0, 0)
    full = lambda a: pl.BlockSpec(a.shape, fixed)
    ospec = pl.BlockSpec((ts, heads * LANES), row)
    return pl.pallas_call(
        functools.partial(_mla_prep_kernel, heads=heads, rank=rank, nope=nope, rope=rope, scale=scale),
        grid=(t // ts,),
        in_specs=[pl.BlockSpec((ts, d), row), pl.BlockSpec((ts, 1), row), full(invf), full(wkva), full(gkv),
                  full(wuk), full(wuv), full(vones), full(wqa), full(gq), full(wqb), full(wqbs)],
        out_specs=[ospec, ospec, ospec],
        out_shape=[jax.ShapeDtypeStruct((t, heads * LANES), BF16)] * 3,
        compiler_params=_cparams("parallel"),
        name="mla_prep",
    )(x2d, posf, invf, wkva, gkv, wuk, wuv, vones, wqa, gq, wqb, wqbs)


FLASH_HEADS = 4


def _flash_kernel(q_ref, k_ref, v_ref, o_ref, *, tq, tk, vdim, nh):
    qi = pl.program_id(2)
    hpp = LANES // vdim
    lane = lax.broadcasted_iota(jnp.int32, (tq, LANES), 1)
    qs = [q_ref[:, hh * LANES:(hh + 1) * LANES] for hh in range(nh)]

    def chunk(c, carry, masked):
        k0 = pl.multiple_of(c * tk, tk)
        ss = [_dot_nt(qs[hh], k_ref[pl.ds(k0, tk), hh * LANES:(hh + 1) * LANES]) for hh in range(nh)]
        if masked:
            keep = lax.broadcasted_iota(jnp.int32, (tq, tk), 1) <= lax.broadcasted_iota(jnp.int32, (tq, tk), 0)
            ss = [jnp.where(keep, s, -jnp.inf) for s in ss]
        out = []
        for hh in range(nh):
            m, acc = carry[hh]
            m_new = jnp.maximum(m, jnp.max(ss[hh], -1, keepdims=True))
            p = jnp.exp((ss[hh] - m_new).astype(BF16))
            acc = jnp.exp(m - m_new) * acc + _dot(p, v_ref[pl.ds(k0, tk), hh * LANES:(hh + 1) * LANES])
            out.append((m_new, acc))
        return tuple(out)

    init = tuple((jnp.full((tq, 1), -jnp.inf, F32), jnp.zeros((tq, LANES), F32)) for _ in range(nh))
    carry = lax.fori_loop(0, qi, functools.partial(chunk, masked=False), init)
    carry = chunk(qi, carry, masked=True)
    for grp in range(nh // hpp):
        out = jnp.zeros((tq, LANES), F32)
        for slot in range(hpp):
            acc = carry[grp * hpp + slot][1]
            ones_lane = ((slot + 1) * vdim) % LANES
            mine = (lane >= slot * vdim) & (lane < (slot + 1) * vdim)
            out = jnp.where(mine, acc / acc[:, ones_lane:ones_lane + 1], out)
        o_ref[:, grp * LANES:(grp + 1) * LANES] = out.astype(BF16)


def _flash(q, k, v, batch, seq, heads, vdim, tq, tk):
    t = q.shape[0]
    hpp = LANES // vdim
    nh = math.gcd(heads, FLASH_HEADS)
    nq = seq // tq
    assert tq == tk and nh % hpp == 0
    return pl.pallas_call(
        functools.partial(_flash_kernel, tq=tq, tk=tk, vdim=vdim, nh=nh),
        grid=(batch, heads // nh, nq),
        in_specs=[pl.BlockSpec((tq, nh * LANES), lambda b, h, i: (b * nq + i, h)),
                  pl.BlockSpec((seq, nh * LANES), lambda b, h, i: (b, h)),
                  pl.BlockSpec((seq, nh * LANES), lambda b, h, i: (b, h))],
        out_specs=pl.BlockSpec((tq, nh // hpp * LANES), lambda b, h, i: (b * nq + i, h)),
        out_shape=jax.ShapeDtypeStruct((t, heads * vdim), BF16),
        compiler_params=_cparams("parallel", "parallel", "arbitrary"),
        name="mla_flash",
    )(q, k, v)


def _top_values(sc, n):
    vals = []
    cur = sc
    for _ in range(n):
        m = jnp.max(cur, axis=0, keepdims=True)
        vals.append(m)
        cur = jnp.where(cur == m, -jnp.inf, cur)
    return vals


def _sorting_network(n):
    pairs = []
    p = 1
    while p < n:
        k = p
        while k >= 1:
            for j in range(k % p, n - k, 2 * k):
                for i in range(min(k, n - j - k)):
                    if (i + j) // (2 * p) == (i + j + k) // (2 * p):
                        pairs.append((i + j, i + j + k))
            k //= 2
        p *= 2
    return pairs


def _top_values_sorted(sc, n):
    nslab = sc.shape[0] // SUBLANES
    cols = [sc[v * SUBLANES:(v + 1) * SUBLANES, :] for v in range(nslab)]
    for i, j in _sorting_network(nslab):
        cols[i], cols[j] = jnp.maximum(cols[i], cols[j]), jnp.minimum(cols[i], cols[j])
    vals = []
    for r in range(n):
        m = jnp.max(cols[0], axis=0, keepdims=True)
        vals.append(m)
        hit = cols[0] == m
        for k in range(min(nslab, n - 1 - r)):
            nxt = cols[k + 1] if k + 1 < nslab else -jnp.inf
            cols[k] = jnp.where(hit, nxt, cols[k])
    return vals


def _peer_route_kernel(x_ref, wq_ref, keys_ref, xb_ref, thr_ref, e1_ref, s2_ref, e2_ref, *, heads, topk):
    xb = x_ref[...].astype(BF16)
    xb_ref[...] = xb
    q = _dot(xb, wq_ref[...])
    nk = keys_ref.shape[1]
    for h in range(heads):
        s1 = _dot_nt(keys_ref[2 * h], q[:, 2 * h * nk:(2 * h + 1) * nk].astype(BF16))
        s2 = _dot_nt(keys_ref[2 * h + 1], q[:, (2 * h + 1) * nk:(2 * h + 2) * nk].astype(BF16))
        a = _top_values_sorted(s1, topk + 1)
        b = _top_values_sorted(s2, topk + 1)
        cands = [a[i] + b[j] for i in range(topk + 1) for j in range(topk + 1) if (i + 1) * (j + 1) <= topk + 1]
        best = _top_values(jnp.concatenate(cands, axis=0), topk + 1)
        zsum = None
        for r in range(topk):
            e = jnp.exp(best[r] - best[0])
            zsum = e if zsum is None else zsum + e
        thr = 0.5 * (best[topk - 1] + best[topk]) - s1
        outs = ((thr_ref, thr), (e1_ref, jnp.exp(s1 - a[0]) / zsum), (s2_ref, s2), (e2_ref, jnp.exp(s2 - b[0])))
        for ref, val in outs:
            for tg in range(val.shape[1] // LANES):
                ref[0, h, tg] = val[:, tg * LANES:(tg + 1) * LANES]


def _peer_route(x2d, wq, keys, heads, ts):
    t, d = x2d.shape
    nk = keys.shape[1]
    nt = t // ts
    shape = (nt, heads, ts // LANES, nk, LANES)
    ospec = pl.BlockSpec((1,) + shape[1:], lambda i: (i, 0, 0, 0, 0))
    return pl.pallas_call(
        functools.partial(_peer_route_kernel, heads=heads, topk=PEER_TOPK),
        grid=(nt,),
        in_specs=[pl.BlockSpec((ts, d), lambda i: (i, 0)),
                  pl.BlockSpec(wq.shape, lambda i: (0, 0)),
                  pl.BlockSpec(keys.shape, lambda i: (0, 0, 0))],
        out_specs=[pl.BlockSpec((ts, d), lambda i: (i, 0)), ospec, ospec, ospec, ospec],
        out_shape=[jax.ShapeDtypeStruct((t, d), BF16)] + [jax.ShapeDtypeStruct(shape, F32)] * 4,
        compiler_params=_cparams("parallel"),
        name="peer_route",
    )(x2d, wq, keys)


def _gelu_x2(x):
    return x * (1.0 + lax.erf(x * (2.0 ** -0.5)))


GATE_I = 2
GATE_ROWS = 128


def _peer_dense_kernel(x_ref, xb_ref, u_ref, vt_ref, thr_ref, e1_ref, s2_ref, e2_ref, g_ref, b_ref, o_ref,
                       acc_ref, act_ref, hid_ref, *, heads, nk, eb, alpha):
    e = pl.program_id(1)
    ts = xb_ref.shape[0]
    ni = eb // nk
    ntg = ts // LANES

    @pl.when(e == 0)
    def _():
        acc_ref[...] = jnp.zeros(acc_ref.shape, F32)

    hid = _gelu_x2(_dot_nt(u_ref[...], xb_ref[...]))
    for il in range(ni):
        for tg in range(ntg):
            hid_ref[il, tg] = hid[il * nk:(il + 1) * nk, tg * LANES:(tg + 1) * LANES]

    def token_group(tg, carry):
        for ip in range(ni // GATE_I):
            for jc in range(nk // GATE_ROWS):
                js = slice(jc * GATE_ROWS, (jc + 1) * GATE_ROWS)
                gates = [None] * GATE_I
                for h in range(heads):
                    s2 = s2_ref[0, h, tg, js, :]
                    e2 = e2_ref[0, h, tg, js, :]
                    for a in range(GATE_I):
                        il = ip * GATE_I + a
                        thr = thr_ref[0, h, tg, il:il + 1, :]
                        e1 = e1_ref[0, h, tg, il:il + 1, :]
                        term = jnp.where(s2 >= thr, e2, 0.0) * e1
                        gates[a] = term if gates[a] is None else gates[a] + term
                for a in range(GATE_I):
                    il = ip * GATE_I + a
                    act = gates[a] * hid_ref[il, tg, js, :]
                    act_ref[tg, il * nk + jc * GATE_ROWS:il * nk + (jc + 1) * GATE_ROWS, :] = act.astype(BF16)
        return carry

    lax.fori_loop(0, ntg, token_group, 0)
    act = jnp.concatenate([act_ref[tg] for tg in range(ntg)], axis=1)
    acc_ref[...] += _dot(vt_ref[...], act)

    @pl.when(e == pl.num_programs(1) - 1)
    def _():
        o_ref[...] = _layer_norm(alpha * x_ref[...] + acc_ref[...].T, g_ref[...], b_ref[...])


def _peer_dense(x2d, xb, u, vt, layer, thr, e1, s2, e2, g, b, alpha, heads, ts, eb):
    t, d = x2d.shape
    n = u.shape[1]
    nk = s2.shape[3]
    ntg = ts // LANES
    assert (eb // nk) % GATE_I == 0 and nk % GATE_ROWS == 0
    jspec = pl.BlockSpec((1, heads, ntg, nk, LANES), lambda i, e: (i, 0, 0, 0, 0))
    ispec = pl.BlockSpec((1, heads, ntg, eb // nk, LANES), lambda i, e: (i, 0, 0, e, 0))
    return pl.pallas_call(
        functools.partial(_peer_dense_kernel, heads=heads, nk=nk, eb=eb, alpha=alpha),
        grid=(t // ts, n // eb),
        in_specs=[pl.BlockSpec((ts, d), lambda i, e: (i, 0)),
                  pl.BlockSpec((ts, d), lambda i, e: (i, 0)),
                  pl.BlockSpec((None, eb, d), lambda i, e: (layer, e, 0)),
                  pl.BlockSpec((None, d, eb), lambda i, e: (layer, 0, e)),
                  ispec, ispec, jspec, jspec,
                  pl.BlockSpec((1, d), lambda i, e: (0, 0)),
                  pl.BlockSpec((1, d), lambda i, e: (0, 0))],
        out_specs=pl.BlockSpec((ts, d), lambda i, e: (i, 0)),
        out_shape=jax.ShapeDtypeStruct((t, d), F32),
        scratch_shapes=[pltpu.VMEM((d, ts), F32), pltpu.VMEM((ntg, eb, LANES), BF16),
                        pltpu.VMEM((eb // nk, ntg, nk, LANES), F32)],
        compiler_params=_cparams("parallel", "arbitrary"),
        name="peer_dense",
    )(x2d, xb, u, vt, thr, e1, s2, e2, g, b)


def _peer(h2d, w_q, sub_keys, u_all, vt_all, layer, g, b, alpha):
    t, d = h2d.shape
    heads, _, nk, dh = sub_keys.shape
    assert nk == LANES and dh == LANES
    ts = min(512, t)
    eb = min(2048, u_all.shape[1])
    keys = sub_keys.reshape(heads * 2, nk, dh).astype(BF16)
    xb, thr, e1, s2, e2 = _peer_route(h2d, w_q.astype(BF16), keys, heads, ts)
    return _peer_dense(h2d, xb, u_all, vt_all, layer, thr, e1, s2, e2,
                       g.reshape(1, d), b.reshape(1, d), alpha, heads, ts, eb)


def _gdn_layer(h2d, batch, seq, w_in, conv_w, a_log, dt_bias, norm_g, w_out, g, b, alpha):
    t, d = h2d.shape
    heads = a_log.shape[0]
    dv = norm_g.shape[0]
    nv = heads * dv
    nqk = (conv_w.shape[1] - nv) // 2
    assert nqk // heads == LANES and dv == LANES and 2 * heads <= LANES
    chunk = min(GDN_KERNEL_CHUNK, seq)
    ts = min(512, seq)
    tc = min(512, seq)
    cw = 2 * nqk + nv
    wqkv = w_in[:, :cw].astype(BF16)
    wz = w_in[:, cw:cw + nv].astype(BF16)
    wbg = jnp.pad(w_in[:, cw + nv:], ((0, 0), (0, LANES - 2 * heads))).astype(BF16)
    alog = jnp.pad(a_log, (heads, LANES - 2 * heads)).reshape(1, LANES)
    dtb = jnp.pad(dt_bias, (heads, LANES - 2 * heads)).reshape(1, LANES)
    q, k, v, z, bg = _gdn_proj(h2d, wqkv, wz, wbg, conv_w, alog, dtb, batch, seq, heads, nqk, nv, ts, chunk)
    o = _gdn_chunk(q, k, v, z, bg, norm_g.reshape(1, dv), batch, seq, heads, tc, chunk)
    return _proj_ln(o, w_out.astype(BF16), h2d, g.reshape(1, d), b.reshape(1, d), alpha, min(512, t))


def _head_pad(w, heads, width):
    r = w.shape[0]
    return jnp.pad(w.reshape(r, heads, width), ((0, 0), (0, 0), (0, LANES - width))).reshape(r, heads * LANES)


def _rope_swap(w, half):
    return jnp.concatenate([-w[..., half:], w[..., :half]], -1)


def _mla_layer(h2d, positions, batch, seq, shared_kv, kv_w_a, kv_norm_g, kv_w_uk, kv_w_uv, w_qa, q_norm_g, w_qb,
               w_out, g, b, alpha):
    t, d = h2d.shape
    rank = kv_norm_g.shape[0]
    rope = kv_w_a.shape[1] - rank
    qrank = w_qa.shape[1]
    heads = (w_qb.shape[1] - kv_w_uk.shape[1]) // rope
    nope = kv_w_uk.shape[1] // heads
    vdim = kv_w_uv.shape[1] // heads
    half = rope // 2
    assert nope + rope <= LANES and LANES % vdim == 0 and vdim < LANES and rank % LANES == 0 and LANES % rope == 0

    inv_freq = ROPE_THETA ** (-jnp.arange(half, dtype=F32) / half)
    invf = jnp.tile(inv_freq, LANES // half)
    posf = positions.astype(F32).reshape(t, 1)

    kr = kv_w_a[:, rank:]
    place = lambda w: jnp.pad(w, ((0, 0), (nope, LANES - nope - rope)))
    wkva = jnp.concatenate([kv_w_a[:, :rank], place(kr), place(_rope_swap(kr, half))], -1).astype(BF16)
    wuk = _head_pad(kv_w_uk, heads, nope).astype(BF16)
    wq3 = w_qb.reshape(qrank, heads, nope + rope)
    wqb = _head_pad(w_qb, heads, nope + rope).astype(BF16)
    wq_sw = jnp.concatenate([jnp.zeros((qrank, heads, nope), F32), _rope_swap(wq3[..., nope:], half)], -1)
    wqbs = _head_pad(wq_sw.reshape(qrank, heads * (nope + rope)), heads, nope + rope).astype(BF16)

    hpp = LANES // vdim
    wuv3 = kv_w_uv.reshape(rank, heads, vdim)
    wuv = jnp.concatenate([jnp.pad(wuv3[:, h], ((0, 0), ((h % hpp) * vdim, LANES - (h % hpp + 1) * vdim)))
                           for h in range(heads)], -1).astype(BF16)
    ones_at = jnp.array([h * LANES + ((h % hpp + 1) * vdim) % LANES for h in range(heads)])
    vones = jnp.zeros((1, heads * LANES), F32).at[0, ones_at].set(1.0)

    ts = min(512, t)
    q, k, v = _mla_prep(h2d, posf, invf.reshape(1, LANES), wkva, kv_norm_g.reshape(1, rank), wuk, wuv, vones,
                        w_qa.astype(BF16), q_norm_g.reshape(1, qrank), wqb, wqbs, heads, rank, nope, rope, ts)
    if shared_kv is None:
        shared_kv = (k, v)
    tq = min(512, seq)
    o = _flash(q, shared_kv[0], shared_kv[1], batch, seq, heads, vdim, tq, tq)
    h_new = _proj_ln(o, w_out.astype(BF16), h2d, g.reshape(1, d), b.reshape(1, d), alpha, min(512, t))
    return h_new, shared_kv


def kernel(x, positions, a_w_in, a_conv_w, a_a_log, a_dt_bias, a_norm_g, a_w_out, kv_w_a, kv_norm_g, kv_w_uk,
           kv_w_uv, b_w_qa, b_q_norm_g, b_w_qb, b_w_out, peer_w_q, peer_sub_keys, peer_u, peer_v, ln_g, ln_b):
    batch, seq, d = x.shape
    depth = ln_g.shape[0]
    n_a = a_w_in.shape[0]
    alpha = (2 * depth) ** 0.25
    h = x.reshape(batch * seq, d)
    u_all = peer_u.astype(BF16)
    vt_all = jnp.swapaxes((0.5 * peer_v).astype(BF16), 1, 2)
    shared_kv = None
    for layer in range(depth):
        if layer < n_a:
            i = layer
            h = _gdn_layer(h, batch, seq, a_w_in[i], a_conv_w[i], a_a_log[i], a_dt_bias[i], a_norm_g[i],
                           a_w_out[i], ln_g[layer, 0], ln_b[layer, 0], alpha)
        else:
            j = layer - n_a
            h, shared_kv = _mla_layer(h, positions, batch, seq, shared_kv, kv_w_a, kv_norm_g, kv_w_uk, kv_w_uv,
                                      b_w_qa[j], b_q_norm_g[j], b_w_qb[j], b_w_out[j],
                                      ln_g[layer, 0], ln_b[layer, 0], alpha)
        h = _peer(h, peer_w_q[layer], peer_sub_keys[layer], u_all, vt_all, layer,
                  ln_g[layer, 1], ln_b[layer, 1], alpha)
    return h.reshape(batch, seq, d)
```
